```python
import jax, jax.numpy as jnp
from jax import lax
import numpy as np

D_MODEL = 1024
BATCH = 2
SEQ = 16384
DEPTH = 4

N_MIXERS = 2
MEM_LEN = 256
HEAD_DIM = 64
MEM_HEADS = 4
MEM_WIDTH = MEM_HEADS * HEAD_DIM
TOK_WIDTH = D_MODEL - MEM_WIDTH
MOBA_HEADS = TOK_WIDTH // HEAD_DIM
MOBA_BLOCK = 256
MOBA_TOPK = 3
Q_CHUNK = 128
POOL_WINDOWS = (2, 4, 8, 16)
POOL_GROUPS = len(POOL_WINDOWS)
POOL_GROUP_WIDTH = TOK_WIDTH // POOL_GROUPS
D_FF = 4 * D_MODEL
ROPE_THETA = 10000.0
EPS = 1e-6
NEG = -1e30
N_POOL_LAYERS = (DEPTH + 1) // 2
N_MOBA_LAYERS = DEPTH // 2

kernel_name = "hybrid_pool_moba_memxattn_trunk"


def rms_norm(x, g):
    xf = x.astype(jnp.float32)
    y = xf * lax.rsqrt(jnp.mean(xf * xf, axis=-1, keepdims=True) + EPS)
    return (y * g.astype(jnp.float32)).astype(x.dtype)


def rope_tables(seq_len):
    pos = jnp.arange(seq_len, dtype=jnp.float32)
    inv = ROPE_THETA ** (-jnp.arange(0, HEAD_DIM, 2, dtype=jnp.float32) / HEAD_DIM)
    ang = pos[:, None] * inv[None, :]
    return jnp.cos(ang), jnp.sin(ang)


def apply_rope(x, cos, sin):
    x1, x2 = jnp.split(x, 2, axis=-1)
    c = cos.astype(x.dtype)
    s = sin.astype(x.dtype)
    return jnp.concatenate([x1 * c - x2 * s, x2 * c + x1 * s], axis=-1)


def causal_window_mean(u, w):
    S = u.shape[1]
    c = jnp.cumsum(u.astype(jnp.float32), axis=1)
    c = jnp.pad(c, ((0, 0), (1, 0), (0, 0)))
    hi = c[:, 1:]
    lo = jnp.pad(c[:, :S + 1 - w], ((0, 0), (w - 1, 0), (0, 0)))
    cnt = jnp.minimum(jnp.arange(S) + 1, w).astype(jnp.float32)[None, :, None]
    return (hi - lo) / cnt


def pool_mixer(u, w_group, scale):
    B, S, _ = u.shape
    ug = u.reshape(B, S, POOL_GROUPS, POOL_GROUP_WIDTH)
    pooled = jnp.stack([causal_window_mean(ug[:, :, g], w) for g, w in enumerate(POOL_WINDOWS)], axis=2)
    d = (pooled - ug.astype(jnp.float32)).astype(u.dtype)
    y = jnp.einsum('bsgc,gcd->bsgd', d, w_group)
    return y.reshape(B, S, TOK_WIDTH) * scale


def moba_mixer(qkv, q_gain, k_gain, cos, sin):
    B, S, _ = qkv.shape
    q, k, v = jnp.split(qkv, 3, axis=-1)

    def heads(t):
        return t.reshape(B, S, MOBA_HEADS, HEAD_DIM).transpose(0, 2, 1, 3)

    q = apply_rope(rms_norm(heads(q), q_gain), cos, sin)
    k = apply_rope(rms_norm(heads(k), k_gain), cos, sin)
    v = heads(v)
    H = MOBA_HEADS
    nb = -(-S // MOBA_BLOCK)
    pad = nb * MOBA_BLOCK - S
    kb = jnp.pad(k, ((0, 0), (0, 0), (0, pad), (0, 0))).reshape(B, H, nb, MOBA_BLOCK, HEAD_DIM)
    vb = jnp.pad(v, ((0, 0), (0, 0), (0, pad), (0, 0))).reshape(B, H, nb, MOBA_BLOCK, HEAD_DIM)
    k_mean = jnp.mean(kb.astype(jnp.float32), axis=3)
    topk = min(MOBA_TOPK, nb)
    n_chunks = S // Q_CHUNK
    scale = HEAD_DIM ** -0.5
    bi = jnp.arange(B)[:, None, None, None]
    hi = jnp.arange(H)[None, :, None, None]
    block_ids = jnp.arange(nb)
    key_off = jnp.arange(MOBA_BLOCK)
    q_off = jnp.arange(Q_CHUNK)

    def chunk(ci):
        q0 = ci * Q_CHUNK
        own = q0 // MOBA_BLOCK
        qc = lax.dynamic_slice_in_dim(q, q0, Q_CHUNK, axis=2)
        g = jnp.einsum('bhcd,bhnd->bhcn', qc.astype(jnp.float32), k_mean)
        g = jnp.where(block_ids < own, g, -jnp.inf)
        _, idx = lax.top_k(g, topk)
        valid = idx < own
        k_sel = kb[bi, hi, idx]
        v_sel = vb[bi, hi, idx]
        s_sel = jnp.einsum('bhcd,bhcjkd->bhcjk', qc, k_sel).astype(jnp.float32) * scale
        s_sel = jnp.where(valid[..., None], s_sel, NEG)
        k_own = lax.dynamic_index_in_dim(kb, own, axis=2, keepdims=False)
        v_own = lax.dynamic_index_in_dim(vb, own, axis=2, keepdims=False)
        s_own = jnp.einsum('bhcd,bhkd->bhck', qc, k_own).astype(jnp.float32) * scale
        qpos = q0 + q_off
        kpos = own * MOBA_BLOCK + key_off
        s_own = jnp.where(kpos[None, :] <= qpos[:, None], s_own, NEG)
        s = jnp.concatenate([s_sel.reshape(B, H, Q_CHUNK, topk * MOBA_BLOCK), s_own], axis=-1)
        p = jax.nn.softmax(s, axis=-1).astype(v.dtype)
        p_sel = p[..., :topk * MOBA_BLOCK].reshape(B, H, Q_CHUNK, topk, MOBA_BLOCK)
        p_own = p[..., topk * MOBA_BLOCK:]
        return (jnp.einsum('bhcjk,bhcjkd->bhcd', p_sel, v_sel)
                + jnp.einsum('bhck,bhkd->bhcd', p_own, v_own))

    out = lax.map(chunk, jnp.arange(n_chunks))
    out = jnp.transpose(out, (1, 2, 0, 3, 4)).reshape(B, H, S, HEAD_DIM)
    return out.transpose(0, 2, 1, 3).reshape(B, S, TOK_WIDTH)


def mem_cross_attention(qm, mem_n, w_mem_kv, q_gain, k_gain):
    B, S, _ = qm.shape
    M = mem_n.shape[1]
    k, v = jnp.split(mem_n @ w_mem_kv, 2, axis=-1)
    q = rms_norm(qm.reshape(B, S, MEM_HEADS, HEAD_DIM), q_gain)
    k = rms_norm(k.reshape(B, M, MEM_HEADS, HEAD_DIM), k_gain)
    v = v.reshape(B, M, MEM_HEADS, HEAD_DIM)
    s = jnp.einsum('bshd,bmhd->bhsm', q, k).astype(jnp.float32) * (HEAD_DIM ** -0.5)
    p = jax.nn.softmax(s, axis=-1).astype(v.dtype)
    o = jnp.einsum('bhsm,bmhd->bshd', p, v)
    return o.reshape(B, S, MEM_WIDTH)


def setup_inputs(seed: int = 0) -> dict:
    key = jax.random.key(seed)
    ks = jax.random.split(key, 20)
    f32 = jnp.float32

    def w(k, shape, fan_in):
        return jax.random.normal(k, shape, f32) * (fan_in ** -0.5)

    def gain(k, shape):
        return 1.0 + 0.1 * jax.random.normal(k, shape, f32)

    return {
        "x": jax.random.normal(ks[0], (BATCH, SEQ, D_MODEL), f32),
        "mem": jax.random.normal(ks[1], (BATCH, MEM_LEN, D_MODEL), f32),
        "g_mix": gain(ks[2], (DEPTH, D_MODEL)),
        "g_mem": gain(ks[3], (DEPTH, D_MODEL)),
        "g_mlp": gain(ks[4], (DEPTH, D_MODEL)),
        "w_in_pool": w(ks[5], (N_POOL_LAYERS, D_MODEL, TOK_WIDTH + MEM_WIDTH), D_MODEL),
        "w_pool_group": w(ks[6], (N_POOL_LAYERS, POOL_GROUPS, POOL_GROUP_WIDTH, POOL_GROUP_WIDTH), POOL_GROUP_WIDTH),
        "pool_scale": gain(ks[7], (N_POOL_LAYERS, TOK_WIDTH)),
        "w_in_moba": w(ks[8], (N_MOBA_LAYERS, D_MODEL, 3 * TOK_WIDTH + MEM_WIDTH), D_MODEL),
        "moba_q_gain": gain(ks[9], (N_MOBA_LAYERS, HEAD_DIM)),
        "moba_k_gain": gain(ks[10], (N_MOBA_LAYERS, HEAD_DIM)),
        "w_mem_kv": w(ks[11], (DEPTH, D_MODEL, 2 * MEM_WIDTH), D_MODEL),
        "mem_q_gain": gain(ks[12], (DEPTH, HEAD_DIM)),
        "mem_k_gain": gain(ks[13], (DEPTH, HEAD_DIM)),
        "w_out": w(ks[14], (DEPTH, TOK_WIDTH + MEM_WIDTH, D_MODEL), TOK_WIDTH + MEM_WIDTH),
        "w_ff1": w(ks[15], (DEPTH, D_MODEL, D_FF), D_MODEL),
        "w_ff2": w(ks[16], (DEPTH, D_FF, D_MODEL), D_FF),
    }


def reference(x, mem, g_mix, g_mem, g_mlp, w_in_pool, w_pool_group, pool_scale,
              w_in_moba, moba_q_gain, moba_k_gain, w_mem_kv, mem_q_gain, mem_k_gain,
              w_out, w_ff1, w_ff2):
    S = x.shape[1]
    cos, sin = rope_tables(S)
    for i in range(DEPTH):
        j = i // N_MIXERS
        u = rms_norm(x, g_mix[i])
        mem_n = rms_norm(mem, g_mem[i])
        if i % N_MIXERS == 0:
            h = u @ w_in_pool[j]
            tok = pool_mixer(h[..., :TOK_WIDTH], w_pool_group[j], pool_scale[j])
            qm = h[..., TOK_WIDTH:]
        else:
            h = u @ w_in_moba[j]
            tok = moba_mixer(h[..., :3 * TOK_WIDTH], moba_q_gain[j], moba_k_gain[j], cos, sin)
            qm = h[..., 3 * TOK_WIDTH:]
        mo = mem_cross_attention(qm, mem_n, w_mem_kv[i], mem_q_gain[i], mem_k_gain[i])
        x = x + jnp.concatenate([tok, mo], axis=-1) @ w_out[i]
        u = rms_norm(x, g_mlp[i])
        x = x + jnp.square(jax.nn.relu(u @ w_ff1[i])) @ w_ff2[i]
    return x
```

```python
import functools

import jax
import jax.numpy as jnp
from jax import lax
from jax.experimental import pallas as pl
from jax.experimental.pallas import tpu as pltpu

D_MODEL = 1024
HEAD_DIM = 64
HALF_DIM = HEAD_DIM // 2
MEM_HEADS = 4
MEM_WIDTH = MEM_HEADS * HEAD_DIM
TOK_WIDTH = D_MODEL - MEM_WIDTH
MOBA_HEADS = TOK_WIDTH // HEAD_DIM
MOBA_BLOCK = 256
MOBA_TOPK = 3
POOL_WINDOWS = (2, 4, 8, 16)
POOL_GROUP_WIDTH = TOK_WIDTH // len(POOL_WINDOWS)
POOL_HALO = 16
D_FF = 4 * D_MODEL
ROPE_THETA = 10000.0
EPS = 1e-6
NEG = -1e30
SM_SCALE = HEAD_DIM ** -0.5

LANES = 128
VMEM_LIMIT_BYTES = 56 * 1024 * 1024

TOKEN_TILE = 512

F32 = jnp.float32
BF16 = jnp.bfloat16
NT_DIMS = (((1,), (1,)), ((), ()))


def _params():
    return pltpu.CompilerParams(
        dimension_semantics=("arbitrary", "arbitrary"), vmem_limit_bytes=VMEM_LIMIT_BYTES)


def _rms_rows(x, g):
    ms = jnp.mean(x * x, axis=-1, keepdims=True)
    return x * lax.rsqrt(ms + EPS) * g


def _head_rms_t(h, gain_col):
    ms = jnp.mean(h * h, axis=0, keepdims=True)
    return h * lax.rsqrt(ms + EPS) * gain_col


def _rope_t(h, cos, sin):
    h1, h2 = h[:HALF_DIM], h[HALF_DIM:]
    return jnp.concatenate([h1 * cos - h2 * sin, h2 * cos + h1 * sin], axis=0)


def _pair_operand(q_bf, slot):
    z = jnp.zeros_like(q_bf)
    return jnp.concatenate([q_bf, z] if slot == 0 else [z, q_bf], axis=0)


def _mem_attention_t(qm_t, mq_gain, mk_ref, mvt_ref):
    outs = []
    for h in range(MEM_HEADS):
        pair, slot = divmod(h, 2)
        q = _head_rms_t(qm_t[h * HEAD_DIM:(h + 1) * HEAD_DIM], mq_gain)
        q2 = _pair_operand((q * SM_SCALE).astype(BF16), slot)
        k_pair = mk_ref[0, :, pair * LANES:(pair + 1) * LANES]
        s = jnp.dot(k_pair, q2, preferred_element_type=F32)
        m = jnp.max(s, axis=0, keepdims=True)
        e = jnp.exp(s - m)
        l = jnp.sum(e, axis=0, keepdims=True)
        v_t = mvt_ref[0, h * HEAD_DIM:(h + 1) * HEAD_DIM, :]
        o = jnp.dot(v_t, e.astype(BF16), preferred_element_type=F32)
        outs.append(o / l)
    return jnp.concatenate(outs, axis=0).T


def _mem_kv_kernel(mem_ref, g_ref, wkv_t_ref, kg_ref, mk_ref, mvt_ref):
    mem_n = _rms_rows(mem_ref[0], g_ref[...]).astype(BF16)
    kv_t = lax.dot_general(wkv_t_ref[...], mem_n, NT_DIMS, preferred_element_type=F32)
    k_heads = [_head_rms_t(kv_t[h * HEAD_DIM:(h + 1) * HEAD_DIM], kg_ref[...]) for h in range(MEM_HEADS)]
    mk_ref[0] = jnp.concatenate(k_heads, axis=0).T.astype(BF16)
    mvt_ref[0] = kv_t[MEM_WIDTH:].astype(BF16)


def _mem_kv(mem, g, wkv_t, k_gain):
    B, M, D = mem.shape
    return pl.pallas_call(
        _mem_kv_kernel,
        grid=(B, 1),
        in_specs=[
            pl.BlockSpec((1, M, D), lambda b, _: (b, 0, 0)),
            pl.BlockSpec((1, D), lambda b, _: (0, 0)),
            pl.BlockSpec((2 * MEM_WIDTH, D), lambda b, _: (0, 0)),
            pl.BlockSpec((HEAD_DIM, 1), lambda b, _: (0, 0)),
        ],
        out_specs=[
            pl.BlockSpec((1, M, MEM_WIDTH), lambda b, _: (b, 0, 0)),
            pl.BlockSpec((1, MEM_WIDTH, M), lambda b, _: (b, 0, 0)),
        ],
        out_shape=[
            jax.ShapeDtypeStruct((B, M, MEM_WIDTH), BF16),
            jax.ShapeDtypeStruct((B, MEM_WIDTH, M), BF16),
        ],
        compiler_params=_params(),
        name="mem_kv",
    )(mem, g, wkv_t, k_gain)


def _pool_in_kernel(x_ref, g_ref, w_tok_ref, wm_t_ref, wg_ref, scale_ref, mqg_ref, mk_ref, mvt_ref,
                    tok_ref, mo_ref, halo_ref):
    t = pl.program_id(1)
    T = x_ref.shape[1]

    @pl.when(t == 0)
    def _():
        halo_ref[...] = jnp.zeros_like(halo_ref)

    u = _rms_rows(x_ref[0], g_ref[...]).astype(BF16)
    h = jnp.dot(u, w_tok_ref[...], preferred_element_type=F32)

    ext = jnp.concatenate([halo_ref[...], h], axis=0)
    halo_ref[...] = h[T - POOL_HALO:]
    s2 = ext + pltpu.roll(ext, 1, 0)
    s4 = s2 + pltpu.roll(s2, 2, 0)
    s8 = s4 + pltpu.roll(s4, 4, 0)
    s16 = s8 + pltpu.roll(s8, 8, 0)
    lane = lax.broadcasted_iota(jnp.int32, (T, TOK_WIDTH), 1)
    pos = t * T + lax.broadcasted_iota(jnp.int32, (T, TOK_WIDTH), 0)
    sums = (s2, s4, s8, s16)
    win_sum = sums[-1][POOL_HALO:]
    window = jnp.full((T, TOK_WIDTH), POOL_WINDOWS[-1], jnp.int32)
    for gi in range(len(POOL_WINDOWS) - 2, -1, -1):
        in_group = lane < (gi + 1) * POOL_GROUP_WIDTH
        win_sum = jnp.where(in_group, sums[gi][POOL_HALO:], win_sum)
        window = jnp.where(in_group, POOL_WINDOWS[gi], window)
    count = jnp.minimum(pos + 1, window).astype(F32)
    d = (win_sum / count - h).astype(BF16)
    y = jnp.dot(d, wg_ref[...], preferred_element_type=F32) * scale_ref[...]
    tok_ref[0] = y.astype(BF16)

    qm_t = lax.dot_general(wm_t_ref[...], u, NT_DIMS, preferred_element_type=F32)
    mo_ref[0] = _mem_attention_t(qm_t, mqg_ref[...], mk_ref, mvt_ref).astype(BF16)


def _pool_in(x, g, w_tok, wm_t, wg, scale, mq_gain, mk, mvt):
    B, S, D = x.shape
    M = mk.shape[1]
    T = min(TOKEN_TILE, S)
    const = lambda b, t: (0, 0)
    return pl.pallas_call(
        _pool_in_kernel,
        grid=(B, S // T),
        in_specs=[
            pl.BlockSpec((1, T, D), lambda b, t: (b, t, 0)),
            pl.BlockSpec((1, D), const),
            pl.BlockSpec((D, TOK_WIDTH), const),
            pl.BlockSpec((MEM_WIDTH, D), const),
            pl.BlockSpec((TOK_WIDTH, TOK_WIDTH), const),
            pl.BlockSpec((1, TOK_WIDTH), const),
            pl.BlockSpec((HEAD_DIM, 1), const),
            pl.BlockSpec((1, M, MEM_WIDTH), lambda b, t: (b, 0, 0)),
            pl.BlockSpec((1, MEM_WIDTH, M), lambda b, t: (b, 0, 0)),
        ],
        out_specs=[
            pl.BlockSpec((1, T, TOK_WIDTH), lambda b, t: (b, t, 0)),
            pl.BlockSpec((1, T, MEM_WIDTH), lambda b, t: (b, t, 0)),
        ],
        out_shape=[
            jax.ShapeDtypeStruct((B, S, TOK_WIDTH), BF16),
            jax.ShapeDtypeStruct((B, S, MEM_WIDTH), BF16),
        ],
        scratch_shapes=[pltpu.VMEM((POOL_HALO, TOK_WIDTH), F32)],
        compiler_params=_params(),
        name="pool_in",
    )(x, g, w_tok, wm_t, wg, scale, mq_gain, mk, mvt)


def _selection_bias(gate, t, T):
    nb = gate.shape[0]
    blk = lax.broadcasted_iota(jnp.int32, (nb, T), 0)
    pos = t * T + lax.broadcasted_iota(jnp.int32, (nb, T), 1)
    own = lax.shift_right_logical(pos, MOBA_BLOCK.bit_length() - 1)
    g = jnp.where(blk < own, gate, -jnp.inf)
    bias = jnp.full((nb, T), NEG, F32)
    for _ in range(MOBA_TOPK):
        mx = jnp.max(g, axis=0, keepdims=True)
        idx = jnp.min(jnp.where(g == mx, blk, nb), axis=0, keepdims=True)
        idx = jnp.where(mx > -jnp.inf, idx, -1)
        pick = blk == idx
        bias = jnp.where(pick, 0.0, bias)
        g = jnp.where(pick, -jnp.inf, g)
    return bias


def _moba_in_kernel(x_ref, g_ref, wqkv_t_ref, wm_t_ref, cos_ref, sin_ref, qg_ref, kg_ref, mqg_ref,
                    mk_ref, mvt_ref, qt_ref, k_ref, vt_ref, bias_ref, mo_ref, kmean_ref):
    t = pl.program_id(1)
    T = x_ref.shape[1]
    blocks_per_tile = T // MOBA_BLOCK

    @pl.when(t == 0)
    def _():
        kmean_ref[...] = jnp.zeros_like(kmean_ref)

    u = _rms_rows(x_ref[0], g_ref[...]).astype(BF16)
    cos, sin = cos_ref[...], sin_ref[...]

    def proj_t(lo, hi):
        return lax.dot_general(wqkv_t_ref[lo:hi, :], u, NT_DIMS, preferred_element_type=F32)

    k_t = proj_t(TOK_WIDTH, 2 * TOK_WIDTH)
    k_heads = [_rope_t(_head_rms_t(k_t[h * HEAD_DIM:(h + 1) * HEAD_DIM], kg_ref[...]), cos, sin)
               for h in range(MOBA_HEADS)]
    k_nat = jnp.concatenate(k_heads, axis=0).T
    k_ref[0] = k_nat.astype(BF16)
    for c in range(blocks_per_tile):
        kmean_ref[pl.ds(t * blocks_per_tile + c, 1), :] = jnp.mean(
            k_nat[c * MOBA_BLOCK:(c + 1) * MOBA_BLOCK], axis=0, keepdims=True)

    v_t = proj_t(2 * TOK_WIDTH, 3 * TOK_WIDTH)
    for c in range(blocks_per_tile):
        vt_ref[0, c] = v_t[:, c * MOBA_BLOCK:(c + 1) * MOBA_BLOCK].astype(BF16)

    q_t = proj_t(0, TOK_WIDTH)
    for h in range(MOBA_HEADS):
        rows = slice(h * HEAD_DIM, (h + 1) * HEAD_DIM)
        q = _rope_t(_head_rms_t(q_t[rows], qg_ref[...]), cos, sin)
        qt_ref[0, rows, :] = (q * SM_SCALE).astype(BF16)
        gate = jnp.dot(kmean_ref[:, rows], q, preferred_element_type=F32,
                       precision=lax.Precision.HIGHEST)
        bias_ref[0, h] = _selection_bias(gate, t, T)

    qm_t = lax.dot_general(wm_t_ref[...], u, NT_DIMS, preferred_element_type=F32)
    mo_ref[0] = _mem_attention_t(qm_t, mqg_ref[...], mk_ref, mvt_ref).astype(BF16)


def _moba_in(x, g, wqkv_t, wm_t, cos_t, sin_t, q_gain, k_gain, mq_gain, mk, mvt):
    B, S, D = x.shape
    M = mk.shape[1]
    T = min(TOKEN_TILE, S)
    nb = S // MOBA_BLOCK
    const = lambda b, t: (0, 0)
    return pl.pallas_call(
        _moba_in_kernel,
        grid=(B, S // T),
        in_specs=[
            pl.BlockSpec((1, T, D), lambda b, t: (b, t, 0)),
            pl.BlockSpec((1, D), const),
            pl.BlockSpec((3 * TOK_WIDTH, D), const),
            pl.BlockSpec((MEM_WIDTH, D), const),
            pl.BlockSpec((HALF_DIM, T), lambda b, t: (0, t)),
            pl.BlockSpec((HALF_DIM, T), lambda b, t: (0, t)),
            pl.BlockSpec((HEAD_DIM, 1), const),
            pl.BlockSpec((HEAD_DIM, 1), const),
            pl.BlockSpec((HEAD_DIM, 1), const),
            pl.BlockSpec((1, M, MEM_WIDTH), lambda b, t: (b, 0, 0)),
            pl.BlockSpec((1, MEM_WIDTH, M), lambda b, t: (b, 0, 0)),
        ],
        out_specs=[
            pl.BlockSpec((1, TOK_WIDTH, T), lambda b, t: (b, 0, t)),
            pl.BlockSpec((1, T, TOK_WIDTH), lambda b, t: (b, t, 0)),
            pl.BlockSpec((1, T // MOBA_BLOCK, TOK_WIDTH, MOBA_BLOCK), lambda b, t: (b, t, 0, 0)),
            pl.BlockSpec((1, MOBA_HEADS, nb, T), lambda b, t: (b, 0, 0, t)),
            pl.BlockSpec((1, T, MEM_WIDTH), lambda b, t: (b, t, 0)),
        ],
        out_shape=[
            jax.ShapeDtypeStruct((B, TOK_WIDTH, S), BF16),
            jax.ShapeDtypeStruct((B, S, TOK_WIDTH), BF16),
            jax.ShapeDtypeStruct((B, nb, TOK_WIDTH, MOBA_BLOCK), BF16),
            jax.ShapeDtypeStruct((B, MOBA_HEADS, nb, S), F32),
            jax.ShapeDtypeStruct((B, S, MEM_WIDTH), BF16),
        ],
        scratch_shapes=[pltpu.VMEM((nb, TOK_WIDTH), F32)],
        compiler_params=_params(),
        name="moba_in",
    )(x, g, wqkv_t, wm_t, cos_t, sin_t, q_gain, k_gain, mq_gain, mk, mvt)


def _moba_attn_kernel(qt_ref, k_ref, vt_ref, bias_ref, o_ref):
    i = pl.program_id(2)
    TQ = qt_ref.shape[2]
    q_pair = qt_ref[0]
    row = lax.broadcasted_iota(jnp.int32, q_pair.shape, 0)
    zero = jnp.zeros_like(q_pair)
    q_heads = (jnp.where(row < HEAD_DIM, q_pair, zero), jnp.where(row >= HEAD_DIM, q_pair, zero))

    def block(j):
        k_j = k_ref[0, pl.ds(pl.multiple_of(j * MOBA_BLOCK, MOBA_BLOCK), MOBA_BLOCK), :]
        return k_j, vt_ref[0, j]

    k_own, v_own = block(i)
    key_idx = lax.broadcasted_iota(jnp.int32, (MOBA_BLOCK, TQ), 0)
    qry_idx = lax.broadcasted_iota(jnp.int32, (MOBA_BLOCK, TQ), 1)
    causal = key_idx <= qry_idx
    init = []
    for a in range(2):
        s = jnp.dot(k_own, q_heads[a], preferred_element_type=F32)
        s = jnp.where(causal, s, NEG)
        m = jnp.max(s, axis=0, keepdims=True)
        p = jnp.exp(s - m)
        l = jnp.sum(p, axis=0, keepdims=True)
        acc = jnp.dot(v_own[a * HEAD_DIM:(a + 1) * HEAD_DIM], p.astype(BF16), preferred_element_type=F32)
        init.append((m, l, acc))

    def body(j, carry):
        k_j, v_j = block(j)
        out = []
        for a in range(2):
            m, l, acc = carry[a]
            s = jnp.dot(k_j, q_heads[a], preferred_element_type=F32) + bias_ref[0, a, pl.ds(j, 1), :]
            m_new = jnp.maximum(m, jnp.max(s, axis=0, keepdims=True))
            alpha = jnp.exp(m - m_new)
            p = jnp.exp(s - m_new)
            l_new = alpha * l + jnp.sum(p, axis=0, keepdims=True)
            pv = jnp.dot(v_j[a * HEAD_DIM:(a + 1) * HEAD_DIM], p.astype(BF16), preferred_element_type=F32)
            out.append((m_new, l_new, alpha * acc + pv))
        return tuple(out)

    final = lax.fori_loop(0, i, body, tuple(init))
    o_t = jnp.concatenate([acc / l for (_, l, acc) in final], axis=0)
    o_ref[0] = o_t.T.astype(BF16)


def _moba_attn(qt, k, vt, bias):
    B, _, S = qt.shape
    nb = S // MOBA_BLOCK
    pairs = MOBA_HEADS // 2
    return pl.pallas_call(
        _moba_attn_kernel,
        grid=(B, pairs, nb),
        in_specs=[
            pl.BlockSpec((1, LANES, MOBA_BLOCK), lambda b, p, i: (b, p, i)),
            pl.BlockSpec((1, S, LANES), lambda b, p, i: (b, 0, p)),
            pl.BlockSpec((1, nb, LANES, MOBA_BLOCK), lambda b, p, i: (b, 0, p, 0)),
            pl.BlockSpec((1, 2, nb, MOBA_BLOCK), lambda b, p, i: (b, p, 0, i)),
        ],
        out_specs=pl.BlockSpec((1, MOBA_BLOCK, LANES), lambda b, p, i: (b, i, p)),
        out_shape=jax.ShapeDtypeStruct((B, S, TOK_WIDTH), BF16),
        compiler_params=pltpu.CompilerParams(
            dimension_semantics=("arbitrary", "arbitrary", "arbitrary"), vmem_limit_bytes=VMEM_LIMIT_BYTES),
        name="moba_attn",
    )(qt, k, vt, bias)


def _post_kernel(x_ref, tok_ref, mo_ref, wo_ref, g_ref, w1_ref, w2_ref, o_ref):
    mixed = jnp.concatenate([tok_ref[0], mo_ref[0]], axis=-1)
    x1 = x_ref[0] + jnp.dot(mixed, wo_ref[...], preferred_element_type=F32)
    u = _rms_rows(x1, g_ref[...]).astype(BF16)
    h = jnp.dot(u, w1_ref[...], preferred_element_type=F32)
    h = jnp.square(jnp.maximum(h, 0.0)).astype(BF16)
    o_ref[0] = x1 + jnp.dot(h, w2_ref[...], preferred_element_type=F32)


def _post(x, tok, mo, wo, g, w1, w2):
    B, S, D = x.shape
    T = min(TOKEN_TILE, S)
    const = lambda b, t: (0, 0)
    resident = functools.partial(pl.BlockSpec, index_map=const, pipeline_mode=pl.Buffered(1))
    return pl.pallas_call(
        _post_kernel,
        grid=(B, S // T),
        in_specs=[
            pl.BlockSpec((1, T, D), lambda b, t: (b, t, 0)),
            pl.BlockSpec((1, T, TOK_WIDTH), lambda b, t: (b, t, 0)),
            pl.BlockSpec((1, T, MEM_WIDTH), lambda b, t: (b, t, 0)),
            resident((D, D)),
            pl.BlockSpec((1, D), const),
            resident((D, D_FF)),
            resident((D_FF, D)),
        ],
        out_specs=pl.BlockSpec((1, T, D), lambda b, t: (b, t, 0)),
        out_shape=jax.ShapeDtypeStruct((B, S, D), F32),
        compiler_params=_params(),
        name="post",
    )(x, tok, mo, wo, g, w1, w2)


def _rope_tables_t(seq_len):
    pos = jnp.arange(seq_len, dtype=F32)
    inv = ROPE_THETA ** (-jnp.arange(0, HEAD_DIM, 2, dtype=F32) / HEAD_DIM)
    ang = pos[:, None] * inv[None, :]
    return jnp.cos(ang).T, jnp.sin(ang).T


def _block_diag(w_group):
    G, C, _ = w_group.shape
    eye = jnp.eye(G, dtype=w_group.dtype)
    return (eye[:, None, :, None] * w_group[:, :, None, :]).reshape(G * C, G * C)


def kernel(x, mem, g_mix, g_mem, g_mlp, w_in_pool, w_pool_group, pool_scale, w_in_moba, moba_q_gain,
           moba_k_gain, w_mem_kv, mem_q_gain, mem_k_gain, w_out, w_ff1, w_ff2):
    depth = g_mix.shape[0]
    S = x.shape[1]
    cos_t, sin_t = _rope_tables_t(S)
    col = lambda v: v.reshape(-1, 1)
    row = lambda v: v.reshape(1, -1)
    for i in range(depth):
        j = i // 2
        mk, mvt = _mem_kv(mem, row(g_mem[i]), w_mem_kv[i].T.astype(BF16), col(mem_k_gain[i]))
        if i % 2 == 0:
            w = w_in_pool[j]
            tok, mo = _pool_in(
                x, row(g_mix[i]), w[:, :TOK_WIDTH].astype(BF16), w[:, TOK_WIDTH:].T.astype(BF16),
                _block_diag(w_pool_group[j]).astype(BF16), row(pool_scale[j]), col(mem_q_gain[i]), mk, mvt)
        else:
            w = w_in_moba[j]
            qt, k, vt, bias, mo = _moba_in(
                x, row(g_mix[i]), w[:, :3 * TOK_WIDTH].T.astype(BF16), w[:, 3 * TOK_WIDTH:].T.astype(BF16),
                cos_t, sin_t, col(moba_q_gain[j]), col(moba_k_gain[j]), col(mem_q_gain[i]), mk, mvt)
            tok = _moba_attn(qt, k, vt, bias)
        x = _post(x, tok, mo, w_out[i].astype(BF16), row(g_mlp[i]), w_ff1[i].astype(BF16), w_ff2[i].astype(BF16))
    return x
```

```python
import functools

import jax
import jax.numpy as jnp
from jax import lax
from jax.experimental import pallas as pl
from jax.experimental.pallas import tpu as pltpu

D_MODEL = 1024
HEAD_DIM = 64
HALF_DIM = HEAD_DIM // 2
MEM_HEADS = 4
MEM_WIDTH = MEM_HEADS * HEAD_DIM
TOK_WIDTH = D_MODEL - MEM_WIDTH
MOBA_HEADS = TOK_WIDTH // HEAD_DIM
MOBA_BLOCK = 256
MOBA_TOPK = 3
POOL_WINDOWS = (2, 4, 8, 16)
POOL_GROUP_WIDTH = TOK_WIDTH // len(POOL_WINDOWS)
POOL_HALO = 16
D_FF = 4 * D_MODEL
ROPE_THETA = 10000.0
EPS = 1e-6
NEG = -1e30
SM_SCALE = HEAD_DIM ** -0.5

LANES = 128
VMEM_LIMIT_BYTES = 56 * 1024 * 1024

TOKEN_TILE = 512
KV_GROUP = 4

F32 = jnp.float32
BF16 = jnp.bfloat16
NT_DIMS = (((1,), (1,)), ((), ()))


def _params():
    return pltpu.CompilerParams(
        dimension_semantics=("arbitrary", "arbitrary"), vmem_limit_bytes=VMEM_LIMIT_BYTES)


def _rms_rows(x, g):
    ms = jnp.mean(x * x, axis=-1, keepdims=True)
    return x * lax.rsqrt(ms + EPS) * g


def _head_rms_t(h, gain_col):
    ms = jnp.mean(h * h, axis=0, keepdims=True)
    return h * lax.rsqrt(ms + EPS) * gain_col


def _rope_t(h, cos, sin):
    h1, h2 = h[:HALF_DIM], h[HALF_DIM:]
    return jnp.concatenate([h1 * cos - h2 * sin, h2 * cos + h1 * sin], axis=0)


def _pair_operand(q_bf, slot):
    z = jnp.zeros_like(q_bf)
    return jnp.concatenate([q_bf, z] if slot == 0 else [z, q_bf], axis=0)


def _mem_attention_t(qm_t, mq_gain, mk_ref, mvt_ref):
    outs = []
    for h in range(MEM_HEADS):
        pair, slot = divmod(h, 2)
        q = _head_rms_t(qm_t[h * HEAD_DIM:(h + 1) * HEAD_DIM], mq_gain)
        q2 = _pair_operand((q * SM_SCALE).astype(BF16), slot)
        k_pair = mk_ref[0, :, pair * LANES:(pair + 1) * LANES]
        s = jnp.dot(k_pair, q2, preferred_element_type=F32)
        m = jnp.max(s, axis=0, keepdims=True)
        e = jnp.exp(s - m)
        l = jnp.sum(e, axis=0, keepdims=True)
        v_t = mvt_ref[0, h * HEAD_DIM:(h + 1) * HEAD_DIM, :]
        o = jnp.dot(v_t, e.astype(BF16), preferred_element_type=F32)
        outs.append(o / l)
    return jnp.concatenate(outs, axis=0).T


def _mem_kv_kernel(mem_ref, g_ref, wkv_t_ref, kg_ref, mk_ref, mvt_ref):
    mem_n = _rms_rows(mem_ref[0], g_ref[...]).astype(BF16)
    kv_t = lax.dot_general(wkv_t_ref[...], mem_n, NT_DIMS, preferred_element_type=F32)
    k_heads = [_head_rms_t(kv_t[h * HEAD_DIM:(h + 1) * HEAD_DIM], kg_ref[...]) for h in range(MEM_HEADS)]
    mk_ref[0] = jnp.concatenate(k_heads, axis=0).T.astype(BF16)
    mvt_ref[0] = kv_t[MEM_WIDTH:].astype(BF16)


def _mem_kv(mem, g, wkv_t, k_gain):
    B, M, D = mem.shape
    return pl.pallas_call(
        _mem_kv_kernel,
        grid=(B, 1),
        in_specs=[
            pl.BlockSpec((1, M, D), lambda b, _: (b, 0, 0)),
            pl.BlockSpec((1, D), lambda b, _: (0, 0)),
            pl.BlockSpec((2 * MEM_WIDTH, D), lambda b, _: (0, 0)),
            pl.BlockSpec((HEAD_DIM, 1), lambda b, _: (0, 0)),
        ],
        out_specs=[
            pl.BlockSpec((1, M, MEM_WIDTH), lambda b, _: (b, 0, 0)),
            pl.BlockSpec((1, MEM_WIDTH, M), lambda b, _: (b, 0, 0)),
        ],
        out_shape=[
            jax.ShapeDtypeStruct((B, M, MEM_WIDTH), BF16),
            jax.ShapeDtypeStruct((B, MEM_WIDTH, M), BF16),
        ],
        compiler_params=_params(),
        name="mem_kv",
    )(mem, g, wkv_t, k_gain)


def _pool_in_kernel(x_ref, g_ref, w_tok_ref, wm_t_ref, wg_ref, scale_ref, mqg_ref, mk_ref, mvt_ref,
                    tok_ref, mo_ref, halo_ref):
    t = pl.program_id(1)
    T = x_ref.shape[1]

    @pl.when(t == 0)
    def _():
        halo_ref[...] = jnp.zeros_like(halo_ref)

    u = _rms_rows(x_ref[0], g_ref[...]).astype(BF16)
    h = jnp.dot(u, w_tok_ref[...], preferred_element_type=F32)

    ext = jnp.concatenate([halo_ref[...], h], axis=0)
    halo_ref[...] = h[T - POOL_HALO:]
    s2 = ext + pltpu.roll(ext, 1, 0)
    s4 = s2 + pltpu.roll(s2, 2, 0)
    s8 = s4 + pltpu.roll(s4, 4, 0)
    s16 = s8 + pltpu.roll(s8, 8, 0)
    lane = lax.broadcasted_iota(jnp.int32, (T, TOK_WIDTH), 1)
    pos = t * T + lax.broadcasted_iota(jnp.int32, (T, TOK_WIDTH), 0)
    sums = (s2, s4, s8, s16)
    win_sum = sums[-1][POOL_HALO:]
    window = jnp.full((T, TOK_WIDTH), POOL_WINDOWS[-1], jnp.int32)
    for gi in range(len(POOL_WINDOWS) - 2, -1, -1):
        in_group = lane < (gi + 1) * POOL_GROUP_WIDTH
        win_sum = jnp.where(in_group, sums[gi][POOL_HALO:], win_sum)
        window = jnp.where(in_group, POOL_WINDOWS[gi], window)
    count = jnp.minimum(pos + 1, window).astype(F32)
    d = (win_sum / count - h).astype(BF16)
    y = jnp.dot(d, wg_ref[...], preferred_element_type=F32) * scale_ref[...]
    tok_ref[0] = y.astype(BF16)

    qm_t = lax.dot_general(wm_t_ref[...], u, NT_DIMS, preferred_element_type=F32)
    mo_ref[0] = _mem_attention_t(qm_t, mqg_ref[...], mk_ref, mvt_ref).astype(BF16)


def _pool_in(x, g, w_tok, wm_t, wg, scale, mq_gain, mk, mvt):
    B, S, D = x.shape
    M = mk.shape[1]
    T = min(TOKEN_TILE, S)
    const = lambda b, t: (0, 0)
    return pl.pallas_call(
        _pool_in_kernel,
        grid=(B, S // T),
        in_specs=[
            pl.BlockSpec((1, T, D), lambda b, t: (b, t, 0)),
            pl.BlockSpec((1, D), const),
            pl.BlockSpec((D, TOK_WIDTH), const),
            pl.BlockSpec((MEM_WIDTH, D), const),
            pl.BlockSpec((TOK_WIDTH, TOK_WIDTH), const),
            pl.BlockSpec((1, TOK_WIDTH), const),
            pl.BlockSpec((HEAD_DIM, 1), const),
            pl.BlockSpec((1, M, MEM_WIDTH), lambda b, t: (b, 0, 0)),
            pl.BlockSpec((1, MEM_WIDTH, M), lambda b, t: (b, 0, 0)),
        ],
        out_specs=[
            pl.BlockSpec((1, T, TOK_WIDTH), lambda b, t: (b, t, 0)),
            pl.BlockSpec((1, T, MEM_WIDTH), lambda b, t: (b, t, 0)),
        ],
        out_shape=[
            jax.ShapeDtypeStruct((B, S, TOK_WIDTH), BF16),
            jax.ShapeDtypeStruct((B, S, MEM_WIDTH), BF16),
        ],
        scratch_shapes=[pltpu.VMEM((POOL_HALO, TOK_WIDTH), F32)],
        compiler_params=_params(),
        name="pool_in",
    )(x, g, w_tok, wm_t, wg, scale, mq_gain, mk, mvt)


def _selection_bias(gate, t, T):
    nb = gate.shape[0]
    blk = lax.broadcasted_iota(jnp.int32, (nb, T), 0)
    pos = t * T + lax.broadcasted_iota(jnp.int32, (nb, T), 1)
    own = lax.shift_right_logical(pos, MOBA_BLOCK.bit_length() - 1)
    g = jnp.where(blk < own, gate, -jnp.inf)
    bias = jnp.full((nb, T), NEG, F32)
    for _ in range(MOBA_TOPK):
        mx = jnp.max(g, axis=0, keepdims=True)
        idx = jnp.min(jnp.where(g == mx, blk, nb), axis=0, keepdims=True)
        idx = jnp.where(mx > -jnp.inf, idx, -1)
        pick = blk == idx
        bias = jnp.where(pick, 0.0, bias)
        g = jnp.where(pick, -jnp.inf, g)
    return bias


def _moba_in_kernel(x_ref, g_ref, wqkv_t_ref, wm_t_ref, cos_ref, sin_ref, qg_ref, kg_ref, mqg_ref,
                    mk_ref, mvt_ref, qt_ref, k_ref, vt_ref, bias_ref, mo_ref, kmean_ref):
    t = pl.program_id(1)
    T = x_ref.shape[1]
    blocks_per_tile = T // MOBA_BLOCK

    @pl.when(t == 0)
    def _():
        kmean_ref[...] = jnp.zeros_like(kmean_ref)

    u = _rms_rows(x_ref[0], g_ref[...]).astype(BF16)
    cos, sin = cos_ref[...], sin_ref[...]

    def proj_t(lo, hi):
        return lax.dot_general(wqkv_t_ref[lo:hi, :], u, NT_DIMS, preferred_element_type=F32)

    k_t = proj_t(TOK_WIDTH, 2 * TOK_WIDTH)
    k_heads = [_rope_t(_head_rms_t(k_t[h * HEAD_DIM:(h + 1) * HEAD_DIM], kg_ref[...]), cos, sin)
               for h in range(MOBA_HEADS)]
    k_nat = jnp.concatenate(k_heads, axis=0).T
    k_ref[0] = k_nat.astype(BF16)
    for c in range(blocks_per_tile):
        kmean_ref[pl.ds(t * blocks_per_tile + c, 1), :] = jnp.mean(
            k_nat[c * MOBA_BLOCK:(c + 1) * MOBA_BLOCK], axis=0, keepdims=True)

    v_t = proj_t(2 * TOK_WIDTH, 3 * TOK_WIDTH)
    for c in range(blocks_per_tile):
        vt_ref[0, c] = v_t[:, c * MOBA_BLOCK:(c + 1) * MOBA_BLOCK].astype(BF16)

    q_t = proj_t(0, TOK_WIDTH)
    for h in range(MOBA_HEADS):
        rows = slice(h * HEAD_DIM, (h + 1) * HEAD_DIM)
        q = _rope_t(_head_rms_t(q_t[rows], qg_ref[...]), cos, sin)
        qt_ref[0, rows, :] = (q * SM_SCALE).astype(BF16)
        gate = jnp.dot(kmean_ref[:, rows], q, preferred_element_type=F32,
                       precision=lax.Precision.HIGHEST)
        bias_ref[0, h] = _selection_bias(gate, t, T)

    qm_t = lax.dot_general(wm_t_ref[...], u, NT_DIMS, preferred_element_type=F32)
    mo_ref[0] = _mem_attention_t(qm_t, mqg_ref[...], mk_ref, mvt_ref).astype(BF16)


def _moba_in(x, g, wqkv_t, wm_t, cos_t, sin_t, q_gain, k_gain, mq_gain, mk, mvt):
    B, S, D = x.shape
    M = mk.shape[1]
    T = min(TOKEN_TILE, S)
    nb = S // MOBA_BLOCK
    const = lambda b, t: (0, 0)
    return pl.pallas_call(
        _moba_in_kernel,
        grid=(B, S // T),
        in_specs=[
            pl.BlockSpec((1, T, D), lambda b, t: (b, t, 0)),
            pl.BlockSpec((1, D), const),
            pl.BlockSpec((3 * TOK_WIDTH, D), const),
            pl.BlockSpec((MEM_WIDTH, D), const),
            pl.BlockSpec((HALF_DIM, T), lambda b, t: (0, t)),
            pl.BlockSpec((HALF_DIM, T), lambda b, t: (0, t)),
            pl.BlockSpec((HEAD_DIM, 1), const),
            pl.BlockSpec((HEAD_DIM, 1), const),
            pl.BlockSpec((HEAD_DIM, 1), const),
            pl.BlockSpec((1, M, MEM_WIDTH), lambda b, t: (b, 0, 0)),
            pl.BlockSpec((1, MEM_WIDTH, M), lambda b, t: (b, 0, 0)),
        ],
        out_specs=[
            pl.BlockSpec((1, TOK_WIDTH, T), lambda b, t: (b, 0, t)),
            pl.BlockSpec((1, T, TOK_WIDTH), lambda b, t: (b, t, 0)),
            pl.BlockSpec((1, T // MOBA_BLOCK, TOK_WIDTH, MOBA_BLOCK), lambda b, t: (b, t, 0, 0)),
            pl.BlockSpec((1, MOBA_HEADS, nb, T), lambda b, t: (b, 0, 0, t)),
            pl.BlockSpec((1, T, MEM_WIDTH), lambda b, t: (b, t, 0)),
        ],
        out_shape=[
            jax.ShapeDtypeStruct((B, TOK_WIDTH, S), BF16),
            jax.ShapeDtypeStruct((B, S, TOK_WIDTH), BF16),
            jax.ShapeDtypeStruct((B, nb, TOK_WIDTH, MOBA_BLOCK), BF16),
            jax.ShapeDtypeStruct((B, MOBA_HEADS, nb, S), F32),
            jax.ShapeDtypeStruct((B, S, MEM_WIDTH), BF16),
        ],
        scratch_shapes=[pltpu.VMEM((nb, TOK_WIDTH), F32)],
        compiler_params=_params(),
        name="moba_in",
    )(x, g, wqkv_t, wm_t, cos_t, sin_t, q_gain, k_gain, mq_gain, mk, mvt)


def _moba_attn_kernel(qt_ref, k_ref, vt_ref, bias_ref, o_ref):
    i = pl.program_id(2)
    TQ = qt_ref.shape[2]
    q_pair = qt_ref[0]
    row = lax.broadcasted_iota(jnp.int32, q_pair.shape, 0)
    zero = jnp.zeros_like(q_pair)
    q_heads = (jnp.where(row < HEAD_DIM, q_pair, zero), jnp.where(row >= HEAD_DIM, q_pair, zero))

    def block(j):
        k_j = k_ref[0, pl.ds(pl.multiple_of(j * MOBA_BLOCK, MOBA_BLOCK), MOBA_BLOCK), :]
        return k_j, vt_ref[0, j]

    k_own, v_own = block(i)
    key_idx = lax.broadcasted_iota(jnp.int32, (MOBA_BLOCK, TQ), 0)
    qry_idx = lax.broadcasted_iota(jnp.int32, (MOBA_BLOCK, TQ), 1)
    causal = key_idx <= qry_idx
    init = []
    for a in range(2):
        s = jnp.dot(k_own, q_heads[a], preferred_element_type=F32)
        s = jnp.where(causal, s, NEG)
        m = jnp.max(s, axis=0, keepdims=True)
        p = jnp.exp(s - m)
        l = jnp.sum(p, axis=0, keepdims=True)
        acc = jnp.dot(v_own[a * HEAD_DIM:(a + 1) * HEAD_DIM], p.astype(BF16), preferred_element_type=F32)
        init.append((m, l, acc))

    group_keys = KV_GROUP * MOBA_BLOCK

    def body(g, carry):
        j0 = pl.multiple_of(g * KV_GROUP, KV_GROUP)
        k_g = k_ref[0, pl.ds(pl.multiple_of(g * group_keys, group_keys), group_keys), :]
        v_g = jnp.concatenate([vt_ref[0, j0 + c] for c in range(KV_GROUP)], axis=1)
        out = []
        scores = [jnp.dot(k_g, q_heads[a], preferred_element_type=F32) for a in range(2)]
        for a in range(2):
            m, l, acc = carry[a]
            s = scores[a]
            s = jnp.concatenate(
                [s[c * MOBA_BLOCK:(c + 1) * MOBA_BLOCK] + bias_ref[0, a, pl.ds(j0 + c, 1), :]
                 for c in range(KV_GROUP)], axis=0)
            m_new = jnp.maximum(m, jnp.max(s, axis=0, keepdims=True))
            alpha = jnp.exp(m - m_new)
            p = jnp.exp(s - m_new)
            l_new = alpha * l + jnp.sum(p, axis=0, keepdims=True)
            pv = jnp.dot(v_g[a * HEAD_DIM:(a + 1) * HEAD_DIM], p.astype(BF16), preferred_element_type=F32)
            out.append((m_new, l_new, alpha * acc + pv))
        return tuple(out)

    final = lax.fori_loop(0, (i + KV_GROUP - 1) // KV_GROUP, body, tuple(init))
    o_t = jnp.concatenate([acc / l for (_, l, acc) in final], axis=0)
    o_ref[0] = o_t.T.astype(BF16)


def _moba_attn(qt, k, vt, bias):
    B, _, S = qt.shape
    nb = S // MOBA_BLOCK
    assert nb % KV_GROUP == 0, "key blocks are read in whole groups"
    pairs = MOBA_HEADS // 2
    return pl.pallas_call(
        _moba_attn_kernel,
        grid=(B, pairs, nb),
        in_specs=[
            pl.BlockSpec((1, LANES, MOBA_BLOCK), lambda b, p, i: (b, p, i)),
            pl.BlockSpec((1, S, LANES), lambda b, p, i: (b, 0, p)),
            pl.BlockSpec((1, nb, LANES, MOBA_BLOCK), lambda b, p, i: (b, 0, p, 0)),
            pl.BlockSpec((1, 2, nb, MOBA_BLOCK), lambda b, p, i: (b, p, 0, i)),
        ],
        out_specs=pl.BlockSpec((1, MOBA_BLOCK, LANES), lambda b, p, i: (b, i, p)),
        out_shape=jax.ShapeDtypeStruct((B, S, TOK_WIDTH), BF16),
        compiler_params=pltpu.CompilerParams(
            dimension_semantics=("arbitrary", "arbitrary", "arbitrary"), vmem_limit_bytes=VMEM_LIMIT_BYTES),
        name="moba_attn",
    )(qt, k, vt, bias)


def _post_kernel(x_ref, tok_ref, mo_ref, wo_ref, g_ref, w1_ref, w2_ref, o_ref):
    mixed = jnp.concatenate([tok_ref[0], mo_ref[0]], axis=-1)
    x1 = x_ref[0] + jnp.dot(mixed, wo_ref[...], preferred_element_type=F32)
    u = _rms_rows(x1, g_ref[...]).astype(BF16)
    h = jnp.dot(u, w1_ref[...], preferred_element_type=F32)
    h = jnp.square(jnp.maximum(h, 0.0)).astype(BF16)
    o_ref[0] = x1 + jnp.dot(h, w2_ref[...], preferred_element_type=F32)


def _post(x, tok, mo, wo, g, w1, w2):
    B, S, D = x.shape
    T = min(TOKEN_TILE, S)
    const = lambda b, t: (0, 0)
    resident = functools.partial(pl.BlockSpec, index_map=const, pipeline_mode=pl.Buffered(1))
    return pl.pallas_call(
        _post_kernel,
        grid=(B, S // T),
        in_specs=[
            pl.BlockSpec((1, T, D), lambda b, t: (b, t, 0)),
            pl.BlockSpec((1, T, TOK_WIDTH), lambda b, t: (b, t, 0)),
            pl.BlockSpec((1, T, MEM_WIDTH), lambda b, t: (b, t, 0)),
            resident((D, D)),
            pl.BlockSpec((1, D), const),
            resident((D, D_FF)),
            resident((D_FF, D)),
        ],
        out_specs=pl.BlockSpec((1, T, D), lambda b, t: (b, t, 0)),
        out_shape=jax.ShapeDtypeStruct((B, S, D), F32),
        compiler_params=_params(),
        name="post",
    )(x, tok, mo, wo, g, w1, w2)


def _rope_tables_t(seq_len):
    pos = jnp.arange(seq_len, dtype=F32)
    inv = ROPE_THETA ** (-jnp.arange(0, HEAD_DIM, 2, dtype=F32) / HEAD_DIM)
    ang = pos[:, None] * inv[None, :]
    return jnp.cos(ang).T, jnp.sin(ang).T


def _block_diag(w_group):
    G, C, _ = w_group.shape
    eye = jnp.eye(G, dtype=w_group.dtype)
    return (eye[:, None, :, None] * w_group[:, :, None, :]).reshape(G * C, G * C)


def kernel(x, mem, g_mix, g_mem, g_mlp, w_in_pool, w_pool_group, pool_scale, w_in_moba, moba_q_gain,
           moba_k_gain, w_mem_kv, mem_q_gain, mem_k_gain, w_out, w_ff1, w_ff2):
    depth = g_mix.shape[0]
    S = x.shape[1]
    cos_t, sin_t = _rope_tables_t(S)
    col = lambda v: v.reshape(-1, 1)
    row = lambda v: v.reshape(1, -1)
    for i in range(depth):
        j = i // 2
        mk, mvt = _mem_kv(mem, row(g_mem[i]), w_mem_kv[i].T.astype(BF16), col(mem_k_gain[i]))
        if i % 2 == 0:
            w = w_in_pool[j]
            tok, mo = _pool_in(
                x, row(g_mix[i]), w[:, :TOK_WIDTH].astype(BF16), w[:, TOK_WIDTH:].T.astype(BF16),
                _block_diag(w_pool_group[j]).astype(BF16), row(pool_scale[j]), col(mem_q_gain[i]), mk, mvt)
        else:
            w = w_in_moba[j]
            qt, k, vt, bias, mo = _moba_in(
                x, row(g_mix[i]), w[:, :3 * TOK_WIDTH].T.astype(BF16), w[:, 3 * TOK_WIDTH:].T.astype(BF16),
                cos_t, sin_t, col(moba_q_gain[j]), col(moba_k_gain[j]), col(mem_q_gain[i]), mk, mvt)
            tok = _moba_attn(qt, k, vt, bias)
        x = _post(x, tok, mo, w_out[i].astype(BF16), row(g_mlp[i]), w_ff1[i].astype(BF16), w_ff2[i].astype(BF16))
    return x
```

```python
import functools

import jax
import jax.numpy as jnp
from jax import lax
from jax.experimental import pallas as pl
from jax.experimental.pallas import tpu as pltpu

D_MODEL = 1024
HEAD_DIM = 64
HALF_DIM = HEAD_DIM // 2
MEM_HEADS = 4
MEM_WIDTH = MEM_HEADS * HEAD_DIM
TOK_WIDTH = D_MODEL - MEM_WIDTH
MOBA_HEADS = TOK_WIDTH // HEAD_DIM
MOBA_BLOCK = 256
MOBA_TOPK = 3
POOL_WINDOWS = (2, 4, 8, 16)
POOL_GROUP_WIDTH = TOK_WIDTH // len(POOL_WINDOWS)
POOL_HALO = 16
D_FF = 4 * D_MODEL
ROPE_THETA = 10000.0
EPS = 1e-6
NEG = -1e30
SM_SCALE = HEAD_DIM ** -0.5

LANES = 128
VMEM_LIMIT_BYTES = 56 * 1024 * 1024

TOKEN_TILE = 512
KV_GROUP = 4

F32 = jnp.float32
BF16 = jnp.bfloat16
NT_DIMS = (((1,), (1,)), ((), ()))


def _params():
    return pltpu.CompilerParams(
        dimension_semantics=("arbitrary", "arbitrary"), vmem_limit_bytes=VMEM_LIMIT_BYTES)


def _rms_rows(x, g):
    ms = jnp.mean(x * x, axis=-1, keepdims=True)
    return x * lax.rsqrt(ms + EPS) * g


def _head_rms_t(h, gain_col):
    ms = jnp.mean(h * h, axis=0, keepdims=True)
    return h * lax.rsqrt(ms + EPS) * gain_col


def _rope_t(h, cos, sin):
    h1, h2 = h[:HALF_DIM], h[HALF_DIM:]
    return jnp.concatenate([h1 * cos - h2 * sin, h2 * cos + h1 * sin], axis=0)


def _pair_operand(q_bf, slot):
    z = jnp.zeros_like(q_bf)
    return jnp.concatenate([q_bf, z] if slot == 0 else [z, q_bf], axis=0)


def _mem_attention_t(qm_t, mq_gain, mk_ref, mvt_ref):
    outs = []
    for h in range(MEM_HEADS):
        pair, slot = divmod(h, 2)
        q = _head_rms_t(qm_t[h * HEAD_DIM:(h + 1) * HEAD_DIM], mq_gain)
        q2 = _pair_operand((q * SM_SCALE).astype(BF16), slot)
        k_pair = mk_ref[0, :, pair * LANES:(pair + 1) * LANES]
        s = jnp.dot(k_pair, q2, preferred_element_type=F32)
        m = jnp.max(s, axis=0, keepdims=True)
        e = jnp.exp(s - m)
        l = jnp.sum(e, axis=0, keepdims=True)
        v_t = mvt_ref[0, h * HEAD_DIM:(h + 1) * HEAD_DIM, :]
        o = jnp.dot(v_t, e.astype(BF16), preferred_element_type=F32)
        outs.append(o / l)
    return jnp.concatenate(outs, axis=0).T


def _mem_kv_kernel(mem_ref, g_ref, wkv_t_ref, kg_ref, mk_ref, mvt_ref):
    mem_n = _rms_rows(mem_ref[0], g_ref[...]).astype(BF16)
    kv_t = lax.dot_general(wkv_t_ref[...], mem_n, NT_DIMS, preferred_element_type=F32)
    k_heads = [_head_rms_t(kv_t[h * HEAD_DIM:(h + 1) * HEAD_DIM], kg_ref[...]) for h in range(MEM_HEADS)]
    mk_ref[0] = jnp.concatenate(k_heads, axis=0).T.astype(BF16)
    mvt_ref[0] = kv_t[MEM_WIDTH:].astype(BF16)


def _mem_kv(mem, g, wkv_t, k_gain):
    B, M, D = mem.shape
    return pl.pallas_call(
        _mem_kv_kernel,
        grid=(B, 1),
        in_specs=[
            pl.BlockSpec((1, M, D), lambda b, _: (b, 0, 0)),
            pl.BlockSpec((1, D), lambda b, _: (0, 0)),
            pl.BlockSpec((2 * MEM_WIDTH, D), lambda b, _: (0, 0)),
            pl.BlockSpec((HEAD_DIM, 1), lambda b, _: (0, 0)),
        ],
        out_specs=[
            pl.BlockSpec((1, M, MEM_WIDTH), lambda b, _: (b, 0, 0)),
            pl.BlockSpec((1, MEM_WIDTH, M), lambda b, _: (b, 0, 0)),
        ],
        out_shape=[
            jax.ShapeDtypeStruct((B, M, MEM_WIDTH), BF16),
            jax.ShapeDtypeStruct((B, MEM_WIDTH, M), BF16),
        ],
        compiler_params=_params(),
        name="mem_kv",
    )(mem, g, wkv_t, k_gain)


def _pool_in_kernel(x_ref, g_ref, w_tok_ref, wm_t_ref, wg_ref, scale_ref, mqg_ref, mk_ref, mvt_ref,
                    tok_ref, mo_ref, halo_ref):
    t = pl.program_id(1)
    T = x_ref.shape[1]

    @pl.when(t == 0)
    def _():
        halo_ref[...] = jnp.zeros_like(halo_ref)

    u = _rms_rows(x_ref[0], g_ref[...]).astype(BF16)
    h = jnp.dot(u, w_tok_ref[...], preferred_element_type=F32)

    ext = jnp.concatenate([halo_ref[...], h], axis=0)
    halo_ref[...] = h[T - POOL_HALO:]
    s2 = ext + pltpu.roll(ext, 1, 0)
    s4 = s2 + pltpu.roll(s2, 2, 0)
    s8 = s4 + pltpu.roll(s4, 4, 0)
    s16 = s8 + pltpu.roll(s8, 8, 0)
    lane = lax.broadcasted_iota(jnp.int32, (T, TOK_WIDTH), 1)
    pos = t * T + lax.broadcasted_iota(jnp.int32, (T, TOK_WIDTH), 0)
    sums = (s2, s4, s8, s16)
    win_sum = sums[-1][POOL_HALO:]
    window = jnp.full((T, TOK_WIDTH), POOL_WINDOWS[-1], jnp.int32)
    for gi in range(len(POOL_WINDOWS) - 2, -1, -1):
        in_group = lane < (gi + 1) * POOL_GROUP_WIDTH
        win_sum = jnp.where(in_group, sums[gi][POOL_HALO:], win_sum)
        window = jnp.where(in_group, POOL_WINDOWS[gi], window)
    count = jnp.minimum(pos + 1, window).astype(F32)
    d = (win_sum / count - h).astype(BF16)
    y = jnp.dot(d, wg_ref[...], preferred_element_type=F32) * scale_ref[...]
    tok_ref[0] = y.astype(BF16)

    qm_t = lax.dot_general(wm_t_ref[...], u, NT_DIMS, preferred_element_type=F32)
    mo_ref[0] = _mem_attention_t(qm_t, mqg_ref[...], mk_ref, mvt_ref).astype(BF16)


def _pool_in(x, g, w_tok, wm_t, wg, scale, mq_gain, mk, mvt):
    B, S, D = x.shape
    M = mk.shape[1]
    T = min(TOKEN_TILE, S)
    const = lambda b, t: (0, 0)
    return pl.pallas_call(
        _pool_in_kernel,
        grid=(B, S // T),
        in_specs=[
            pl.BlockSpec((1, T, D), lambda b, t: (b, t, 0)),
            pl.BlockSpec((1, D), const),
            pl.BlockSpec((D, TOK_WIDTH), const),
            pl.BlockSpec((MEM_WIDTH, D), const),
            pl.BlockSpec((TOK_WIDTH, TOK_WIDTH), const),
            pl.BlockSpec((1, TOK_WIDTH), const),
            pl.BlockSpec((HEAD_DIM, 1), const),
            pl.BlockSpec((1, M, MEM_WIDTH), lambda b, t: (b, 0, 0)),
            pl.BlockSpec((1, MEM_WIDTH, M), lambda b, t: (b, 0, 0)),
        ],
        out_specs=[
            pl.BlockSpec((1, T, TOK_WIDTH), lambda b, t: (b, t, 0)),
            pl.BlockSpec((1, T, MEM_WIDTH), lambda b, t: (b, t, 0)),
        ],
        out_shape=[
            jax.ShapeDtypeStruct((B, S, TOK_WIDTH), BF16),
            jax.ShapeDtypeStruct((B, S, MEM_WIDTH), BF16),
        ],
        scratch_shapes=[pltpu.VMEM((POOL_HALO, TOK_WIDTH), F32)],
        compiler_params=_params(),
        name="pool_in",
    )(x, g, w_tok, wm_t, wg, scale, mq_gain, mk, mvt)


def _selection_bias(gate, t, T):
    nb = gate.shape[0]
    blk = lax.broadcasted_iota(jnp.int32, (nb, T), 0)
    pos = t * T + lax.broadcasted_iota(jnp.int32, (nb, T), 1)
    own = lax.shift_right_logical(pos, MOBA_BLOCK.bit_length() - 1)
    g = jnp.where(blk < own, gate, -jnp.inf)
    bias = jnp.full((nb, T), NEG, F32)
    for _ in range(MOBA_TOPK):
        mx = jnp.max(g, axis=0, keepdims=True)
        idx = jnp.min(jnp.where(g == mx, blk, nb), axis=0, keepdims=True)
        idx = jnp.where(mx > -jnp.inf, idx, -1)
        pick = blk == idx
        bias = jnp.where(pick, 0.0, bias)
        g = jnp.where(pick, -jnp.inf, g)
    return bias


def _moba_in_kernel(x_ref, g_ref, wqkv_t_ref, wm_t_ref, cos_ref, sin_ref, qg_ref, kg_ref, mqg_ref,
                    mk_ref, mvt_ref, qt_ref, k_ref, vt_ref, bias_ref, mo_ref, kmean_ref):
    t = pl.program_id(1)
    T = x_ref.shape[1]
    blocks_per_tile = T // MOBA_BLOCK

    @pl.when(t == 0)
    def _():
        kmean_ref[...] = jnp.zeros_like(kmean_ref)

    u = _rms_rows(x_ref[0], g_ref[...]).astype(BF16)
    cos, sin = cos_ref[...], sin_ref[...]

    def proj_t(lo, hi):
        return lax.dot_general(wqkv_t_ref[lo:hi, :], u, NT_DIMS, preferred_element_type=F32)

    k_t = proj_t(TOK_WIDTH, 2 * TOK_WIDTH)
    k_heads = [_rope_t(_head_rms_t(k_t[h * HEAD_DIM:(h + 1) * HEAD_DIM], kg_ref[...]), cos, sin)
               for h in range(MOBA_HEADS)]
    k_nat = jnp.concatenate(k_heads, axis=0).T
    k_ref[0] = k_nat.astype(BF16)
    for c in range(blocks_per_tile):
        kmean_ref[pl.ds(t * blocks_per_tile + c, 1), :] = jnp.mean(
            k_nat[c * MOBA_BLOCK:(c + 1) * MOBA_BLOCK], axis=0, keepdims=True)

    v_t = proj_t(2 * TOK_WIDTH, 3 * TOK_WIDTH)
    for c in range(blocks_per_tile):
        vt_ref[0, c] = v_t[:, c * MOBA_BLOCK:(c + 1) * MOBA_BLOCK].astype(BF16)

    q_t = proj_t(0, TOK_WIDTH)
    for h in range(MOBA_HEADS):
        rows = slice(h * HEAD_DIM, (h + 1) * HEAD_DIM)
        q = _rope_t(_head_rms_t(q_t[rows], qg_ref[...]), cos, sin)
        qt_ref[0, rows, :] = (q * SM_SCALE).astype(BF16)
        gate = jnp.dot(kmean_ref[:, rows], q, preferred_element_type=F32,
                       precision=lax.Precision.HIGHEST)
        bias_ref[0, h] = _selection_bias(gate, t, T)

    qm_t = lax.dot_general(wm_t_ref[...], u, NT_DIMS, preferred_element_type=F32)
    mo_ref[0] = _mem_attention_t(qm_t, mqg_ref[...], mk_ref, mvt_ref).astype(BF16)


def _moba_in(x, g, wqkv_t, wm_t, cos_t, sin_t, q_gain, k_gain, mq_gain, mk, mvt):
    B, S, D = x.shape
    M = mk.shape[1]
    T = min(TOKEN_TILE, S)
    nb = S // MOBA_BLOCK
    const = lambda b, t: (0, 0)
    return pl.pallas_call(
        _moba_in_kernel,
        grid=(B, S // T),
        in_specs=[
            pl.BlockSpec((1, T, D), lambda b, t: (b, t, 0)),
            pl.BlockSpec((1, D), const),
            pl.BlockSpec((3 * TOK_WIDTH, D), const),
            pl.BlockSpec((MEM_WIDTH, D), const),
            pl.BlockSpec((HALF_DIM, T), lambda b, t: (0, t)),
            pl.BlockSpec((HALF_DIM, T), lambda b, t: (0, t)),
            pl.BlockSpec((HEAD_DIM, 1), const),
            pl.BlockSpec((HEAD_DIM, 1), const),
            pl.BlockSpec((HEAD_DIM, 1), const),
            pl.BlockSpec((1, M, MEM_WIDTH), lambda b, t: (b, 0, 0)),
            pl.BlockSpec((1, MEM_WIDTH, M), lambda b, t: (b, 0, 0)),
        ],
        out_specs=[
            pl.BlockSpec((1, TOK_WIDTH, T), lambda b, t: (b, 0, t)),
            pl.BlockSpec((1, T, TOK_WIDTH), lambda b, t: (b, t, 0)),
            pl.BlockSpec((1, T // MOBA_BLOCK, TOK_WIDTH, MOBA_BLOCK), lambda b, t: (b, t, 0, 0)),
            pl.BlockSpec((1, MOBA_HEADS, nb, T), lambda b, t: (b, 0, 0, t)),
            pl.BlockSpec((1, T, MEM_WIDTH), lambda b, t: (b, t, 0)),
        ],
        out_shape=[
            jax.ShapeDtypeStruct((B, TOK_WIDTH, S), BF16),
            jax.ShapeDtypeStruct((B, S, TOK_WIDTH), BF16),
            jax.ShapeDtypeStruct((B, nb, TOK_WIDTH, MOBA_BLOCK), BF16),
            jax.ShapeDtypeStruct((B, MOBA_HEADS, nb, S), F32),
            jax.ShapeDtypeStruct((B, S, MEM_WIDTH), BF16),
        ],
        scratch_shapes=[pltpu.VMEM((nb, TOK_WIDTH), F32)],
        compiler_params=_params(),
        name="moba_in",
    )(x, g, wqkv_t, wm_t, cos_t, sin_t, q_gain, k_gain, mq_gain, mk, mvt)


def _moba_attn_kernel(qt_ref, k_ref, vt_ref, bias_ref, o_ref, s0_ref, s1_ref, bm0_ref, bm1_ref):
    i = pl.program_id(2)
    TQ = qt_ref.shape[2]
    group_keys = KV_GROUP * MOBA_BLOCK
    total_groups = k_ref.shape[1] // group_keys
    live_groups = (i + KV_GROUP - 1) // KV_GROUP

    q_pair = qt_ref[0]
    row = lax.broadcasted_iota(jnp.int32, q_pair.shape, 0)
    zero = jnp.zeros_like(q_pair)
    q_heads = (jnp.where(row < HEAD_DIM, q_pair, zero), jnp.where(row >= HEAD_DIM, q_pair, zero))

    def produce(g, s_ref, bm_ref):
        gc = jnp.minimum(g, total_groups - 1)
        j0 = gc * KV_GROUP
        k_g = k_ref[0, pl.ds(pl.multiple_of(gc * group_keys, group_keys), group_keys), :]
        penalty = jnp.where(g >= live_groups, NEG, 0.0)
        for a in range(2):
            s = jnp.dot(k_g, q_heads[a], preferred_element_type=F32)
            s = jnp.concatenate(
                [s[c * MOBA_BLOCK:(c + 1) * MOBA_BLOCK] + (bias_ref[0, a, pl.ds(j0 + c, 1), :] + penalty)
                 for c in range(KV_GROUP)], axis=0)
            s_ref[a] = s
            bm_ref[a] = jnp.max(s, axis=0, keepdims=True)

    def consume(g, s_ref, bm_ref, carry):
        gc = jnp.minimum(g, total_groups - 1)
        j0 = gc * KV_GROUP
        v_g = jnp.concatenate([vt_ref[0, j0 + c] for c in range(KV_GROUP)], axis=1)
        out = []
        for a in range(2):
            m, l, acc = carry[a]
            m_new = jnp.maximum(m, bm_ref[a])
            alpha = jnp.exp(m - m_new)
            p = jnp.exp(s_ref[a] - m_new)
            l_new = alpha * l + jnp.sum(p, axis=0, keepdims=True)
            pv = jnp.dot(v_g[a * HEAD_DIM:(a + 1) * HEAD_DIM], p.astype(BF16), preferred_element_type=F32)
            out.append((m_new, l_new, alpha * acc + pv))
        return tuple(out)

    produce(0, s0_ref, bm0_ref)

    k_own = k_ref[0, pl.ds(pl.multiple_of(i * MOBA_BLOCK, MOBA_BLOCK), MOBA_BLOCK), :]
    v_own = vt_ref[0, i]
    key_idx = lax.broadcasted_iota(jnp.int32, (MOBA_BLOCK, TQ), 0)
    qry_idx = lax.broadcasted_iota(jnp.int32, (MOBA_BLOCK, TQ), 1)
    causal = key_idx <= qry_idx
    init = []
    for a in range(2):
        s = jnp.dot(k_own, q_heads[a], preferred_element_type=F32)
        s = jnp.where(causal, s, NEG)
        m = jnp.max(s, axis=0, keepdims=True)
        p = jnp.exp(s - m)
        l = jnp.sum(p, axis=0, keepdims=True)
        acc = jnp.dot(v_own[a * HEAD_DIM:(a + 1) * HEAD_DIM], p.astype(BF16), preferred_element_type=F32)
        init.append((m, l, acc))

    def body(step, carry):
        g = 2 * step
        produce(g + 1, s1_ref, bm1_ref)
        carry = consume(g, s0_ref, bm0_ref, carry)
        produce(g + 2, s0_ref, bm0_ref)
        return consume(g + 1, s1_ref, bm1_ref, carry)

    final = lax.fori_loop(0, (live_groups + 1) // 2, body, tuple(init))
    o_t = jnp.concatenate([acc / l for (_, l, acc) in final], axis=0)
    o_ref[0] = o_t.T.astype(BF16)


def _moba_attn(qt, k, vt, bias):
    B, _, S = qt.shape
    nb = S // MOBA_BLOCK
    assert nb % KV_GROUP == 0, "key blocks are read in whole groups"
    pairs = MOBA_HEADS // 2
    return pl.pallas_call(
        _moba_attn_kernel,
        grid=(B, pairs, nb),
        in_specs=[
            pl.BlockSpec((1, LANES, MOBA_BLOCK), lambda b, p, i: (b, p, i)),
            pl.BlockSpec((1, S, LANES), lambda b, p, i: (b, 0, p)),
            pl.BlockSpec((1, nb, LANES, MOBA_BLOCK), lambda b, p, i: (b, 0, p, 0)),
            pl.BlockSpec((1, 2, nb, MOBA_BLOCK), lambda b, p, i: (b, p, 0, i)),
        ],
        out_specs=pl.BlockSpec((1, MOBA_BLOCK, LANES), lambda b, p, i: (b, i, p)),
        out_shape=jax.ShapeDtypeStruct((B, S, TOK_WIDTH), BF16),
        scratch_shapes=[
            pltpu.VMEM((2, KV_GROUP * MOBA_BLOCK, MOBA_BLOCK), F32),
            pltpu.VMEM((2, KV_GROUP * MOBA_BLOCK, MOBA_BLOCK), F32),
            pltpu.VMEM((2, 1, MOBA_BLOCK), F32),
            pltpu.VMEM((2, 1, MOBA_BLOCK), F32),
        ],
        compiler_params=pltpu.CompilerParams(
            dimension_semantics=("arbitrary", "arbitrary", "arbitrary"), vmem_limit_bytes=VMEM_LIMIT_BYTES),
        name="moba_attn",
    )(qt, k, vt, bias)


def _post_kernel(x_ref, tok_ref, mo_ref, wo_ref, g_ref, w1_ref, w2_ref, o_ref):
    mixed = jnp.concatenate([tok_ref[0], mo_ref[0]], axis=-1)
    x1 = x_ref[0] + jnp.dot(mixed, wo_ref[...], preferred_element_type=F32)
    u = _rms_rows(x1, g_ref[...]).astype(BF16)
    h = jnp.dot(u, w1_ref[...], preferred_element_type=F32)
    h = jnp.square(jnp.maximum(h, 0.0)).astype(BF16)
    o_ref[0] = x1 + jnp.dot(h, w2_ref[...], preferred_element_type=F32)


def _post(x, tok, mo, wo, g, w1, w2):
    B, S, D = x.shape
    T = min(TOKEN_TILE, S)
    const = lambda b, t: (0, 0)
    resident = functools.partial(pl.BlockSpec, index_map=const, pipeline_mode=pl.Buffered(1))
    return pl.pallas_call(
        _post_kernel,
        grid=(B, S // T),
        in_specs=[
            pl.BlockSpec((1, T, D), lambda b, t: (b, t, 0)),
            pl.BlockSpec((1, T, TOK_WIDTH), lambda b, t: (b, t, 0)),
            pl.BlockSpec((1, T, MEM_WIDTH), lambda b, t: (b, t, 0)),
            resident((D, D)),
            pl.BlockSpec((1, D), const),
            resident((D, D_FF)),
            resident((D_FF, D)),
        ],
        out_specs=pl.BlockSpec((1, T, D), lambda b, t: (b, t, 0)),
        out_shape=jax.ShapeDtypeStruct((B, S, D), F32),
        compiler_params=_params(),
        name="post",
    )(x, tok, mo, wo, g, w1, w2)


def _rope_tables_t(seq_len):
    pos = jnp.arange(seq_len, dtype=F32)
    inv = ROPE_THETA ** (-jnp.arange(0, HEAD_DIM, 2, dtype=F32) / HEAD_DIM)
    ang = pos[:, None] * inv[None, :]
    return jnp.cos(ang).T, jnp.sin(ang).T


def _block_diag(w_group):
    G, C, _ = w_group.shape
    eye = jnp.eye(G, dtype=w_group.dtype)
    return (eye[:, None, :, None] * w_group[:, :, None, :]).reshape(G * C, G * C)


def kernel(x, mem, g_mix, g_mem, g_mlp, w_in_pool, w_pool_group, pool_scale, w_in_moba, moba_q_gain,
           moba_k_gain, w_mem_kv, mem_q_gain, mem_k_gain, w_out, w_ff1, w_ff2):
    depth = g_mix.shape[0]
    S = x.shape[1]
    cos_t, sin_t = _rope_tables_t(S)
    col = lambda v: v.reshape(-1, 1)
    row = lambda v: v.reshape(1, -1)
    for i in range(depth):
        j = i // 2
        mk, mvt = _mem_kv(mem, row(g_mem[i]), w_mem_kv[i].T.astype(BF16), col(mem_k_gain[i]))
        if i % 2 == 0:
            w = w_in_pool[j]
            tok, mo = _pool_in(
                x, row(g_mix[i]), w[:, :TOK_WIDTH].astype(BF16), w[:, TOK_WIDTH:].T.astype(BF16),
                _block_diag(w_pool_group[j]).astype(BF16), row(pool_scale[j]), col(mem_q_gain[i]), mk, mvt)
        else:
            w = w_in_moba[j]
            qt, k, vt, bias, mo = _moba_in(
                x, row(g_mix[i]), w[:, :3 * TOK_WIDTH].T.astype(BF16), w[:, 3 * TOK_WIDTH:].T.astype(BF16),
                cos_t, sin_t, col(moba_q_gain[j]), col(moba_k_gain[j]), col(mem_q_gain[i]), mk, mvt)
            tok = _moba_attn(qt, k, vt, bias)
        x = _post(x, tok, mo, w_out[i].astype(BF16), row(g_mlp[i]), w_ff1[i].astype(BF16), w_ff2[i].astype(BF16))
    return x
```

```python
import functools

import jax
import jax.numpy as jnp
from jax import lax
from jax.experimental import pallas as pl
from jax.experimental.pallas import tpu as pltpu

D_MODEL = 1024
HEAD_DIM = 64
HALF_DIM = HEAD_DIM // 2
MEM_HEADS = 4
MEM_WIDTH = MEM_HEADS * HEAD_DIM
TOK_WIDTH = D_MODEL - MEM_WIDTH
MOBA_HEADS = TOK_WIDTH // HEAD_DIM
MOBA_BLOCK = 256
MOBA_TOPK = 3
POOL_WINDOWS = (2, 4, 8, 16)
POOL_GROUP_WIDTH = TOK_WIDTH // len(POOL_WINDOWS)
POOL_HALO = 16
D_FF = 4 * D_MODEL
ROPE_THETA = 10000.0
EPS = 1e-6
NEG = -1e30
SM_SCALE = HEAD_DIM ** -0.5
LOG2_E = 1.4426950408889634

LANES = 128
F32_SUBLANES = 8
BIAS_ROWS = 16
SUM_ROWS = 16
MXU_DEPTH = 256
VMEM_LIMIT_BYTES = 56 * 1024 * 1024

TOKEN_TILE = 512
KV_GROUP = 4

F32 = jnp.float32
BF16 = jnp.bfloat16
NT_DIMS = (((1,), (1,)), ((), ()))


def _params():
    return pltpu.CompilerParams(
        dimension_semantics=("arbitrary", "arbitrary"), vmem_limit_bytes=VMEM_LIMIT_BYTES)


def _rms_rows(x, g):
    ms = jnp.mean(x * x, axis=-1, keepdims=True)
    return x * lax.rsqrt(ms + EPS) * g


def _head_rms_t(h, gain_col):
    ms = jnp.mean(h * h, axis=0, keepdims=True)
    return h * lax.rsqrt(ms + EPS) * gain_col


def _rope_t(h, cos, sin):
    h1, h2 = h[:HALF_DIM], h[HALF_DIM:]
    return jnp.concatenate([h1 * cos - h2 * sin, h2 * cos + h1 * sin], axis=0)


def _pair_operand(q_bf, slot):
    z = jnp.zeros_like(q_bf)
    return jnp.concatenate([q_bf, z] if slot == 0 else [z, q_bf], axis=0)


def _mem_attention_t(qm_t, mq_gain, mk_ref, mvt_ref):
    outs = []
    for h in range(MEM_HEADS):
        pair, slot = divmod(h, 2)
        q = _head_rms_t(qm_t[h * HEAD_DIM:(h + 1) * HEAD_DIM], mq_gain)
        q2 = _pair_operand((q * SM_SCALE).astype(BF16), slot)
        k_pair = mk_ref[0, :, pair * LANES:(pair + 1) * LANES]
        s = jnp.dot(k_pair, q2, preferred_element_type=F32)
        m = jnp.max(s, axis=0, keepdims=True)
        e = jnp.exp(s - m)
        l = jnp.sum(e, axis=0, keepdims=True)
        v_t = mvt_ref[0, h * HEAD_DIM:(h + 1) * HEAD_DIM, :]
        o = jnp.dot(v_t, e.astype(BF16), preferred_element_type=F32)
        outs.append(o / l)
    return jnp.concatenate(outs, axis=0).T


def _mem_kv_kernel(mem_ref, g_ref, wkv_t_ref, kg_ref, mk_ref, mvt_ref):
    mem_n = _rms_rows(mem_ref[0], g_ref[...]).astype(BF16)
    kv_t = lax.dot_general(wkv_t_ref[...], mem_n, NT_DIMS, preferred_element_type=F32)
    k_heads = [_head_rms_t(kv_t[h * HEAD_DIM:(h + 1) * HEAD_DIM], kg_ref[...]) for h in range(MEM_HEADS)]
    mk_ref[0] = jnp.concatenate(k_heads, axis=0).T.astype(BF16)
    mvt_ref[0] = kv_t[MEM_WIDTH:].astype(BF16)


def _mem_kv(mem, g, wkv_t, k_gain):
    B, M, D = mem.shape
    return pl.pallas_call(
        _mem_kv_kernel,
        grid=(B, 1),
        in_specs=[
            pl.BlockSpec((1, M, D), lambda b, _: (b, 0, 0)),
            pl.BlockSpec((1, D), lambda b, _: (0, 0)),
            pl.BlockSpec((2 * MEM_WIDTH, D), lambda b, _: (0, 0)),
            pl.BlockSpec((HEAD_DIM, 1), lambda b, _: (0, 0)),
        ],
        out_specs=[
            pl.BlockSpec((1, M, MEM_WIDTH), lambda b, _: (b, 0, 0)),
            pl.BlockSpec((1, MEM_WIDTH, M), lambda b, _: (b, 0, 0)),
        ],
        out_shape=[
            jax.ShapeDtypeStruct((B, M, MEM_WIDTH), BF16),
            jax.ShapeDtypeStruct((B, MEM_WIDTH, M), BF16),
        ],
        compiler_params=_params(),
        name="mem_kv",
    )(mem, g, wkv_t, k_gain)


def _pool_in_kernel(x_ref, g_ref, w_tok_ref, wm_t_ref, wg_ref, scale_ref, mqg_ref, mk_ref, mvt_ref,
                    tok_ref, mo_ref, halo_ref):
    t = pl.program_id(1)
    T = x_ref.shape[1]

    @pl.when(t == 0)
    def _():
        halo_ref[...] = jnp.zeros_like(halo_ref)

    u = _rms_rows(x_ref[0], g_ref[...]).astype(BF16)
    h = jnp.dot(u, w_tok_ref[...], preferred_element_type=F32)

    ext = jnp.concatenate([halo_ref[...], h], axis=0)
    halo_ref[...] = h[T - POOL_HALO:]
    s2 = ext + pltpu.roll(ext, 1, 0)
    s4 = s2 + pltpu.roll(s2, 2, 0)
    s8 = s4 + pltpu.roll(s4, 4, 0)
    s16 = s8 + pltpu.roll(s8, 8, 0)
    lane = lax.broadcasted_iota(jnp.int32, (T, TOK_WIDTH), 1)
    pos = t * T + lax.broadcasted_iota(jnp.int32, (T, TOK_WIDTH), 0)
    sums = (s2, s4, s8, s16)
    win_sum = sums[-1][POOL_HALO:]
    window = jnp.full((T, TOK_WIDTH), POOL_WINDOWS[-1], jnp.int32)
    for gi in range(len(POOL_WINDOWS) - 2, -1, -1):
        in_group = lane < (gi + 1) * POOL_GROUP_WIDTH
        win_sum = jnp.where(in_group, sums[gi][POOL_HALO:], win_sum)
        window = jnp.where(in_group, POOL_WINDOWS[gi], window)
    count = jnp.minimum(pos + 1, window).astype(F32)
    d = (win_sum / count - h).astype(BF16)
    y = jnp.dot(d, wg_ref[...], preferred_element_type=F32) * scale_ref[...]
    tok_ref[0] = y.astype(BF16)

    qm_t = lax.dot_general(wm_t_ref[...], u, NT_DIMS, preferred_element_type=F32)
    mo_ref[0] = _mem_attention_t(qm_t, mqg_ref[...], mk_ref, mvt_ref).astype(BF16)


def _pool_in(x, g, w_tok, wm_t, wg, scale, mq_gain, mk, mvt):
    B, S, D = x.shape
    M = mk.shape[1]
    T = min(TOKEN_TILE, S)
    const = lambda b, t: (0, 0)
    return pl.pallas_call(
        _pool_in_kernel,
        grid=(B, S // T),
        in_specs=[
            pl.BlockSpec((1, T, D), lambda b, t: (b, t, 0)),
            pl.BlockSpec((1, D), const),
            pl.BlockSpec((D, TOK_WIDTH), const),
            pl.BlockSpec((MEM_WIDTH, D), const),
            pl.BlockSpec((TOK_WIDTH, TOK_WIDTH), const),
            pl.BlockSpec((1, TOK_WIDTH), const),
            pl.BlockSpec((HEAD_DIM, 1), const),
            pl.BlockSpec((1, M, MEM_WIDTH), lambda b, t: (b, 0, 0)),
            pl.BlockSpec((1, MEM_WIDTH, M), lambda b, t: (b, 0, 0)),
        ],
        out_specs=[
            pl.BlockSpec((1, T, TOK_WIDTH), lambda b, t: (b, t, 0)),
            pl.BlockSpec((1, T, MEM_WIDTH), lambda b, t: (b, t, 0)),
        ],
        out_shape=[
            jax.ShapeDtypeStruct((B, S, TOK_WIDTH), BF16),
            jax.ShapeDtypeStruct((B, S, MEM_WIDTH), BF16),
        ],
        scratch_shapes=[pltpu.VMEM((POOL_HALO, TOK_WIDTH), F32)],
        compiler_params=_params(),
        name="pool_in",
    )(x, g, w_tok, wm_t, wg, scale, mq_gain, mk, mvt)


def _selection_bias(gate, t, T):
    nb = gate.shape[0]
    blk = lax.broadcasted_iota(jnp.int32, (nb, T), 0)
    pos = t * T + lax.broadcasted_iota(jnp.int32, (nb, T), 1)
    own = lax.shift_right_logical(pos, MOBA_BLOCK.bit_length() - 1)
    g = jnp.where(blk < own, gate, -jnp.inf)
    bias = jnp.full((nb, T), NEG, F32)
    for _ in range(MOBA_TOPK):
        mx = jnp.max(g, axis=0, keepdims=True)
        idx = jnp.min(jnp.where(g == mx, blk, nb), axis=0, keepdims=True)
        idx = jnp.where(mx > -jnp.inf, idx, -1)
        pick = blk == idx
        bias = jnp.where(pick, 0.0, bias)
        g = jnp.where(pick, -jnp.inf, g)
    return bias


def _moba_in_kernel(x_ref, g_ref, wqkv_t_ref, wm_t_ref, cos_ref, sin_ref, qg_ref, kg_ref, mqg_ref,
                    mk_ref, mvt_ref, qt_ref, k_ref, vt_ref, bias_ref, mo_ref, kmean_ref):
    t = pl.program_id(1)
    T = x_ref.shape[1]
    blocks_per_tile = T // MOBA_BLOCK

    @pl.when(t == 0)
    def _():
        kmean_ref[...] = jnp.zeros_like(kmean_ref)

    u = _rms_rows(x_ref[0], g_ref[...]).astype(BF16)
    cos, sin = cos_ref[...], sin_ref[...]

    def proj_t(lo, hi):
        return lax.dot_general(wqkv_t_ref[lo:hi, :], u, NT_DIMS, preferred_element_type=F32)

    k_t = proj_t(TOK_WIDTH, 2 * TOK_WIDTH)
    k_heads = [_rope_t(_head_rms_t(k_t[h * HEAD_DIM:(h + 1) * HEAD_DIM], kg_ref[...]), cos, sin)
               for h in range(MOBA_HEADS)]
    k_nat = jnp.concatenate(k_heads, axis=0).T
    k_ref[0] = k_nat.astype(BF16)
    for c in range(blocks_per_tile):
        kmean_ref[pl.ds(t * blocks_per_tile + c, 1), :] = jnp.mean(
            k_nat[c * MOBA_BLOCK:(c + 1) * MOBA_BLOCK], axis=0, keepdims=True)

    v_t = proj_t(2 * TOK_WIDTH, 3 * TOK_WIDTH)
    for c in range(blocks_per_tile):
        vt_ref[0, c] = v_t[:, c * MOBA_BLOCK:(c + 1) * MOBA_BLOCK].astype(BF16)

    q_t = proj_t(0, TOK_WIDTH)
    for h in range(MOBA_HEADS):
        rows = slice(h * HEAD_DIM, (h + 1) * HEAD_DIM)
        q = _rope_t(_head_rms_t(q_t[rows], qg_ref[...]), cos, sin)
        qt_ref[0, rows, :] = (q * (SM_SCALE * LOG2_E)).astype(BF16)
        gate = jnp.dot(kmean_ref[:, rows], q, preferred_element_type=F32,
                       precision=lax.Precision.HIGHEST)
        bias_ref[0, h] = _selection_bias(gate, t, T)

    qm_t = lax.dot_general(wm_t_ref[...], u, NT_DIMS, preferred_element_type=F32)
    mo_ref[0] = _mem_attention_t(qm_t, mqg_ref[...], mk_ref, mvt_ref).astype(BF16)


def _moba_in(x, g, wqkv_t, wm_t, cos_t, sin_t, q_gain, k_gain, mq_gain, mk, mvt):
    B, S, D = x.shape
    M = mk.shape[1]
    T = min(TOKEN_TILE, S)
    nb = S // MOBA_BLOCK
    const = lambda b, t: (0, 0)
    return pl.pallas_call(
        _moba_in_kernel,
        grid=(B, S // T),
        in_specs=[
            pl.BlockSpec((1, T, D), lambda b, t: (b, t, 0)),
            pl.BlockSpec((1, D), const),
            pl.BlockSpec((3 * TOK_WIDTH, D), const),
            pl.BlockSpec((MEM_WIDTH, D), const),
            pl.BlockSpec((HALF_DIM, T), lambda b, t: (0, t)),
            pl.BlockSpec((HALF_DIM, T), lambda b, t: (0, t)),
            pl.BlockSpec((HEAD_DIM, 1), const),
            pl.BlockSpec((HEAD_DIM, 1), const),
            pl.BlockSpec((HEAD_DIM, 1), const),
            pl.BlockSpec((1, M, MEM_WIDTH), lambda b, t: (b, 0, 0)),
            pl.BlockSpec((1, MEM_WIDTH, M), lambda b, t: (b, 0, 0)),
        ],
        out_specs=[
            pl.BlockSpec((1, TOK_WIDTH, T), lambda b, t: (b, 0, t)),
            pl.BlockSpec((1, T, TOK_WIDTH), lambda b, t: (b, t, 0)),
            pl.BlockSpec((1, T // MOBA_BLOCK, TOK_WIDTH, MOBA_BLOCK), lambda b, t: (b, t, 0, 0)),
            pl.BlockSpec((1, MOBA_HEADS, nb, T), lambda b, t: (b, 0, 0, t)),
            pl.BlockSpec((1, T, MEM_WIDTH), lambda b, t: (b, t, 0)),
        ],
        out_shape=[
            jax.ShapeDtypeStruct((B, TOK_WIDTH, S), BF16),
            jax.ShapeDtypeStruct((B, S, TOK_WIDTH), BF16),
            jax.ShapeDtypeStruct((B, nb, TOK_WIDTH, MOBA_BLOCK), BF16),
            jax.ShapeDtypeStruct((B, MOBA_HEADS, nb, S), F32),
            jax.ShapeDtypeStruct((B, S, MEM_WIDTH), BF16),
        ],
        scratch_shapes=[pltpu.VMEM((nb, TOK_WIDTH), F32)],
        compiler_params=_params(),
        name="moba_in",
    )(x, g, wqkv_t, wm_t, cos_t, sin_t, q_gain, k_gain, mq_gain, mk, mvt)


def _moba_attn_kernel(qt_ref, k_ref, vt_ref, bias_ref, onehot_ref, o_ref, *scratch):
    i = pl.program_id(2)
    TQ = qt_ref.shape[2]
    s_refs = (scratch[0:2], scratch[2:4])
    bm_refs = (scratch[4:6], scratch[6:8])
    group_keys = KV_GROUP * MOBA_BLOCK
    total_groups = k_ref.shape[1] // group_keys
    live_groups = (i + KV_GROUP - 1) // KV_GROUP

    q_pair = qt_ref[0]
    row = lax.broadcasted_iota(jnp.int32, q_pair.shape, 0)
    zero = jnp.zeros_like(q_pair)
    q_heads = (jnp.where(row < HEAD_DIM, q_pair, zero), jnp.where(row >= HEAD_DIM, q_pair, zero))

    zeros_tail = jnp.zeros((MXU_DEPTH - LANES - BIAS_ROWS, TQ), BF16)
    ones_rows = jnp.ones((SUM_ROWS, 1), BF16)

    def produce(g, parity, a, s_ref, bm_ref):
        gc = jnp.minimum(g, total_groups - 1)
        k_g = k_ref[0, pl.ds(pl.multiple_of(gc * group_keys, group_keys), group_keys), :]
        lhs = jnp.concatenate([k_g, onehot_ref[parity]], axis=1)
        penalty = jnp.where(g >= live_groups, NEG, 0.0)
        slab_row = pl.multiple_of((gc // 2) * 2 * KV_GROUP, 2 * KV_GROUP)
        slab = bias_ref[0, a, pl.ds(slab_row, 2 * KV_GROUP), :] + penalty
        slab = jnp.concatenate([slab, jnp.zeros_like(slab)], axis=0).astype(BF16)
        rhs = jnp.concatenate([q_heads[a], slab, zeros_tail], axis=0)
        s = jnp.dot(lhs, rhs, preferred_element_type=F32)
        s_ref[...] = s
        bm_ref[...] = jnp.max(s, axis=0, keepdims=True)

    def pv_operand(v_t):
        return jnp.concatenate([v_t, jnp.broadcast_to(ones_rows, (SUM_ROWS, v_t.shape[1]))], axis=0)

    def consume(g, a, s_ref, bm_ref, state):
        gc = jnp.minimum(g, total_groups - 1)
        j0 = gc * KV_GROUP
        rows = slice(a * HEAD_DIM, (a + 1) * HEAD_DIM)
        v_g = jnp.concatenate([vt_ref[0, j0 + c, rows, :] for c in range(KV_GROUP)], axis=1)
        m, acc = state
        m_new = jnp.maximum(m, bm_ref[...])
        alpha = jnp.exp2(m - m_new)
        p = jnp.exp2(s_ref[...] - m_new).astype(BF16)
        pv = jnp.dot(pv_operand(v_g), p, preferred_element_type=F32)
        return m_new, alpha * acc + pv

    produce(0, 0, 0, s_refs[0][0], bm_refs[0][0])
    produce(0, 0, 1, s_refs[0][1], bm_refs[0][1])

    k_own = k_ref[0, pl.ds(pl.multiple_of(i * MOBA_BLOCK, MOBA_BLOCK), MOBA_BLOCK), :]
    v_own = vt_ref[0, i]
    key_idx = lax.broadcasted_iota(jnp.int32, (MOBA_BLOCK, TQ), 0)
    qry_idx = lax.broadcasted_iota(jnp.int32, (MOBA_BLOCK, TQ), 1)
    causal = key_idx <= qry_idx
    init = []
    for a in range(2):
        s = jnp.dot(k_own, q_heads[a], preferred_element_type=F32)
        s = jnp.where(causal, s, NEG)
        m = jnp.max(s, axis=0, keepdims=True)
        p = jnp.exp2(s - m).astype(BF16)
        acc = jnp.dot(pv_operand(v_own[a * HEAD_DIM:(a + 1) * HEAD_DIM]), p, preferred_element_type=F32)
        init.append((m, acc))

    def body(step, carry):
        g = 2 * step
        state = list(carry)
        for half in range(2):
            nxt, cur = 1 - half, half
            for a in range(2):
                produce(g + half + 1, nxt, a, s_refs[nxt][a], bm_refs[nxt][a])
                state[a] = consume(g + half, a, s_refs[cur][a], bm_refs[cur][a], state[a])
        return tuple(state)

    final = lax.fori_loop(0, (live_groups + 1) // 2, body, tuple(init))
    o_t = jnp.concatenate([acc[:HEAD_DIM] / acc[HEAD_DIM:HEAD_DIM + 1] for (_, acc) in final], axis=0)
    o_ref[0] = o_t.T.astype(BF16)


def _block_onehot():
    key_block = jnp.arange(KV_GROUP * MOBA_BLOCK, dtype=jnp.int32) // MOBA_BLOCK
    col = jnp.arange(LANES, dtype=jnp.int32)
    parity = jnp.arange(2, dtype=jnp.int32)
    hit = col[None, None, :] == (parity[:, None, None] * KV_GROUP + key_block[None, :, None])
    return hit.astype(BF16)


def _moba_attn(qt, k, vt, bias):
    B, _, S = qt.shape
    nb = S // MOBA_BLOCK
    assert 2 * KV_GROUP == F32_SUBLANES and nb % F32_SUBLANES == 0, "bias slabs are whole f32 sublane tiles"
    pairs = MOBA_HEADS // 2
    group_keys = KV_GROUP * MOBA_BLOCK
    return pl.pallas_call(
        _moba_attn_kernel,
        grid=(B, pairs, nb),
        in_specs=[
            pl.BlockSpec((1, LANES, MOBA_BLOCK), lambda b, p, i: (b, p, i)),
            pl.BlockSpec((1, S, LANES), lambda b, p, i: (b, 0, p)),
            pl.BlockSpec((1, nb, LANES, MOBA_BLOCK), lambda b, p, i: (b, 0, p, 0)),
            pl.BlockSpec((1, 2, nb, MOBA_BLOCK), lambda b, p, i: (b, p, 0, i)),
            pl.BlockSpec((2, group_keys, LANES), lambda b, p, i: (0, 0, 0)),
        ],
        out_specs=pl.BlockSpec((1, MOBA_BLOCK, LANES), lambda b, p, i: (b, i, p)),
        out_shape=jax.ShapeDtypeStruct((B, S, TOK_WIDTH), BF16),
        scratch_shapes=[pltpu.VMEM((group_keys, MOBA_BLOCK), F32)] * 4 + [pltpu.VMEM((1, MOBA_BLOCK), F32)] * 4,
        compiler_params=pltpu.CompilerParams(
            dimension_semantics=("arbitrary", "arbitrary", "arbitrary"), vmem_limit_bytes=VMEM_LIMIT_BYTES),
        name="moba_attn",
    )(qt, k, vt, bias, _block_onehot())


def _post_kernel(x_ref, tok_ref, mo_ref, wo_ref, g_ref, w1_ref, w2_ref, o_ref):
    mixed = jnp.concatenate([tok_ref[0], mo_ref[0]], axis=-1)
    x1 = x_ref[0] + jnp.dot(mixed, wo_ref[...], preferred_element_type=F32)
    u = _rms_rows(x1, g_ref[...]).astype(BF16)
    h = jnp.dot(u, w1_ref[...], preferred_element_type=F32)
    h = jnp.square(jnp.maximum(h, 0.0)).astype(BF16)
    o_ref[0] = x1 + jnp.dot(h, w2_ref[...], preferred_element_type=F32)


def _post(x, tok, mo, wo, g, w1, w2):
    B, S, D = x.shape
    T = min(TOKEN_TILE, S)
    const = lambda b, t: (0, 0)
    resident = functools.partial(pl.BlockSpec, index_map=const, pipeline_mode=pl.Buffered(1))
    return pl.pallas_call(
        _post_kernel,
        grid=(B, S // T),
        in_specs=[
            pl.BlockSpec((1, T, D), lambda b, t: (b, t, 0)),
            pl.BlockSpec((1, T, TOK_WIDTH), lambda b, t: (b, t, 0)),
            pl.BlockSpec((1, T, MEM_WIDTH), lambda b, t: (b, t, 0)),
            resident((D, D)),
            pl.BlockSpec((1, D), const),
            resident((D, D_FF)),
            resident((D_FF, D)),
        ],
        out_specs=pl.BlockSpec((1, T, D), lambda b, t: (b, t, 0)),
        out_shape=jax.ShapeDtypeStruct((B, S, D), F32),
        compiler_params=_params(),
        name="post",
    )(x, tok, mo, wo, g, w1, w2)


def _rope_tables_t(seq_len):
    pos = jnp.arange(seq_len, dtype=F32)
    inv = ROPE_THETA ** (-jnp.arange(0, HEAD_DIM, 2, dtype=F32) / HEAD_DIM)
    ang = pos[:, None] * inv[None, :]
    return jnp.cos(ang).T, jnp.sin(ang).T


def _block_diag(w_group):
    G, C, _ = w_group.shape
    eye = jnp.eye(G, dtype=w_group.dtype)
    return (eye[:, None, :, None] * w_group[:, :, None, :]).reshape(G * C, G * C)


def kernel(x, mem, g_mix, g_mem, g_mlp, w_in_pool, w_pool_group, pool_scale, w_in_moba, moba_q_gain,
           moba_k_gain, w_mem_kv, mem_q_gain, mem_k_gain, w_out, w_ff1, w_ff2):
    depth = g_mix.shape[0]
    S = x.shape[1]
    cos_t, sin_t = _rope_tables_t(S)
    col = lambda v: v.reshape(-1, 1)
    row = lambda v: v.reshape(1, -1)
    for i in range(depth):
        j = i // 2
        mk, mvt = _mem_kv(mem, row(g_mem[i]), w_mem_kv[i].T.astype(BF16), col(mem_k_gain[i]))
        if i % 2 == 0:
            w = w_in_pool[j]
            tok, mo = _pool_in(
                x, row(g_mix[i]), w[:, :TOK_WIDTH].astype(BF16), w[:, TOK_WIDTH:].T.astype(BF16),
                _block_diag(w_pool_group[j]).astype(BF16), row(pool_scale[j]), col(mem_q_gain[i]), mk, mvt)
        else:
            w = w_in_moba[j]
            qt, k, vt, bias, mo = _moba_in(
                x, row(g_mix[i]), w[:, :3 * TOK_WIDTH].T.astype(BF16), w[:, 3 * TOK_WIDTH:].T.astype(BF16),
                cos_t, sin_t, col(moba_q_gain[j]), col(moba_k_gain[j]), col(mem_q_gain[i]), mk, mvt)
            tok = _moba_attn(qt, k, vt, bias)
        x = _post(x, tok, mo, w_out[i].astype(BF16), row(g_mlp[i]), w_ff1[i].astype(BF16), w_ff2[i].astype(BF16))
    return x
```

```python
import functools

import jax
import jax.numpy as jnp
from jax import lax
from jax.experimental import pallas as pl
from jax.experimental.pallas import tpu as pltpu

D_MODEL = 1024
HEAD_DIM = 64
HALF_DIM = HEAD_DIM // 2
MEM_HEADS = 4
MEM_WIDTH = MEM_HEADS * HEAD_DIM
TOK_WIDTH = D_MODEL - MEM_WIDTH
MOBA_HEADS = TOK_WIDTH // HEAD_DIM
MOBA_BLOCK = 256
MOBA_TOPK = 3
POOL_WINDOWS = (2, 4, 8, 16)
POOL_GROUP_WIDTH = TOK_WIDTH // len(POOL_WINDOWS)
POOL_HALO = 16
D_FF = 4 * D_MODEL
ROPE_THETA = 10000.0
EPS = 1e-6
NEG = -1e30
SM_SCALE = HEAD_DIM ** -0.5
LOG2_E = 1.4426950408889634

LANES = 128
F32_SUBLANES = 8
BIAS_ROWS = 16
SUM_ROWS = 16
MXU_DEPTH = 256
VMEM_LIMIT_BYTES = 56 * 1024 * 1024

TOKEN_TILE = 512
KV_GROUP = 4
GROUPS_PER_STEP = 2

F32 = jnp.float32
BF16 = jnp.bfloat16
NT_DIMS = (((1,), (1,)), ((), ()))


def _params():
    return pltpu.CompilerParams(
        dimension_semantics=("arbitrary", "arbitrary"), vmem_limit_bytes=VMEM_LIMIT_BYTES)


def _rms_rows(x, g):
    ms = jnp.mean(x * x, axis=-1, keepdims=True)
    return x * lax.rsqrt(ms + EPS) * g


def _head_rms_t(h, gain_col):
    ms = jnp.mean(h * h, axis=0, keepdims=True)
    return h * lax.rsqrt(ms + EPS) * gain_col


def _rope_t(h, cos, sin):
    h1, h2 = h[:HALF_DIM], h[HALF_DIM:]
    return jnp.concatenate([h1 * cos - h2 * sin, h2 * cos + h1 * sin], axis=0)


def _pair_operand(q_bf, slot):
    z = jnp.zeros_like(q_bf)
    return jnp.concatenate([q_bf, z] if slot == 0 else [z, q_bf], axis=0)


def _mem_attention_t(qm_t, mq_gain, mk_ref, mvt_ref):
    outs = []
    for h in range(MEM_HEADS):
        pair, slot = divmod(h, 2)
        q = _head_rms_t(qm_t[h * HEAD_DIM:(h + 1) * HEAD_DIM], mq_gain)
        q2 = _pair_operand((q * SM_SCALE).astype(BF16), slot)
        k_pair = mk_ref[0, :, pair * LANES:(pair + 1) * LANES]
        s = jnp.dot(k_pair, q2, preferred_element_type=F32)
        m = jnp.max(s, axis=0, keepdims=True)
        e = jnp.exp(s - m)
        l = jnp.sum(e, axis=0, keepdims=True)
        v_t = mvt_ref[0, h * HEAD_DIM:(h + 1) * HEAD_DIM, :]
        o = jnp.dot(v_t, e.astype(BF16), preferred_element_type=F32)
        outs.append(o / l)
    return jnp.concatenate(outs, axis=0).T


def _mem_kv_kernel(mem_ref, g_ref, wkv_t_ref, kg_ref, mk_ref, mvt_ref):
    mem_n = _rms_rows(mem_ref[0], g_ref[...]).astype(BF16)
    kv_t = lax.dot_general(wkv_t_ref[...], mem_n, NT_DIMS, preferred_element_type=F32)
    k_heads = [_head_rms_t(kv_t[h * HEAD_DIM:(h + 1) * HEAD_DIM], kg_ref[...]) for h in range(MEM_HEADS)]
    mk_ref[0] = jnp.concatenate(k_heads, axis=0).T.astype(BF16)
    mvt_ref[0] = kv_t[MEM_WIDTH:].astype(BF16)


def _mem_kv(mem, g, wkv_t, k_gain):
    B, M, D = mem.shape
    return pl.pallas_call(
        _mem_kv_kernel,
        grid=(B, 1),
        in_specs=[
            pl.BlockSpec((1, M, D), lambda b, _: (b, 0, 0)),
            pl.BlockSpec((1, D), lambda b, _: (0, 0)),
            pl.BlockSpec((2 * MEM_WIDTH, D), lambda b, _: (0, 0)),
            pl.BlockSpec((HEAD_DIM, 1), lambda b, _: (0, 0)),
        ],
        out_specs=[
            pl.BlockSpec((1, M, MEM_WIDTH), lambda b, _: (b, 0, 0)),
            pl.BlockSpec((1, MEM_WIDTH, M), lambda b, _: (b, 0, 0)),
        ],
        out_shape=[
            jax.ShapeDtypeStruct((B, M, MEM_WIDTH), BF16),
            jax.ShapeDtypeStruct((B, MEM_WIDTH, M), BF16),
        ],
        compiler_params=_params(),
        name="mem_kv",
    )(mem, g, wkv_t, k_gain)


def _pool_in_kernel(x_ref, g_ref, w_tok_ref, wm_t_ref, wg_ref, scale_ref, mqg_ref, mk_ref, mvt_ref,
                    tok_ref, mo_ref, halo_ref):
    t = pl.program_id(1)
    T = x_ref.shape[1]

    @pl.when(t == 0)
    def _():
        halo_ref[...] = jnp.zeros_like(halo_ref)

    u = _rms_rows(x_ref[0], g_ref[...]).astype(BF16)
    h = jnp.dot(u, w_tok_ref[...], preferred_element_type=F32)

    ext = jnp.concatenate([halo_ref[...], h], axis=0)
    halo_ref[...] = h[T - POOL_HALO:]
    s2 = ext + pltpu.roll(ext, 1, 0)
    s4 = s2 + pltpu.roll(s2, 2, 0)
    s8 = s4 + pltpu.roll(s4, 4, 0)
    s16 = s8 + pltpu.roll(s8, 8, 0)
    lane = lax.broadcasted_iota(jnp.int32, (T, TOK_WIDTH), 1)
    pos = t * T + lax.broadcasted_iota(jnp.int32, (T, TOK_WIDTH), 0)
    sums = (s2, s4, s8, s16)
    win_sum = sums[-1][POOL_HALO:]
    window = jnp.full((T, TOK_WIDTH), POOL_WINDOWS[-1], jnp.int32)
    for gi in range(len(POOL_WINDOWS) - 2, -1, -1):
        in_group = lane < (gi + 1) * POOL_GROUP_WIDTH
        win_sum = jnp.where(in_group, sums[gi][POOL_HALO:], win_sum)
        window = jnp.where(in_group, POOL_WINDOWS[gi], window)
    count = jnp.minimum(pos + 1, window).astype(F32)
    d = (win_sum / count - h).astype(BF16)
    y = jnp.dot(d, wg_ref[...], preferred_element_type=F32) * scale_ref[...]
    tok_ref[0] = y.astype(BF16)

    qm_t = lax.dot_general(wm_t_ref[...], u, NT_DIMS, preferred_element_type=F32)
    mo_ref[0] = _mem_attention_t(qm_t, mqg_ref[...], mk_ref, mvt_ref).astype(BF16)


def _pool_in(x, g, w_tok, wm_t, wg, scale, mq_gain, mk, mvt):
    B, S, D = x.shape
    M = mk.shape[1]
    T = min(TOKEN_TILE, S)
    const = lambda b, t: (0, 0)
    return pl.pallas_call(
        _pool_in_kernel,
        grid=(B, S // T),
        in_specs=[
            pl.BlockSpec((1, T, D), lambda b, t: (b, t, 0)),
            pl.BlockSpec((1, D), const),
            pl.BlockSpec((D, TOK_WIDTH), const),
            pl.BlockSpec((MEM_WIDTH, D), const),
            pl.BlockSpec((TOK_WIDTH, TOK_WIDTH), const),
            pl.BlockSpec((1, TOK_WIDTH), const),
            pl.BlockSpec((HEAD_DIM, 1), const),
            pl.BlockSpec((1, M, MEM_WIDTH), lambda b, t: (b, 0, 0)),
            pl.BlockSpec((1, MEM_WIDTH, M), lambda b, t: (b, 0, 0)),
        ],
        out_specs=[
            pl.BlockSpec((1, T, TOK_WIDTH), lambda b, t: (b, t, 0)),
            pl.BlockSpec((1, T, MEM_WIDTH), lambda b, t: (b, t, 0)),
        ],
        out_shape=[
            jax.ShapeDtypeStruct((B, S, TOK_WIDTH), BF16),
            jax.ShapeDtypeStruct((B, S, MEM_WIDTH), BF16),
        ],
        scratch_shapes=[pltpu.VMEM((POOL_HALO, TOK_WIDTH), F32)],
        compiler_params=_params(),
        name="pool_in",
    )(x, g, w_tok, wm_t, wg, scale, mq_gain, mk, mvt)


def _selection_bias(gate, t, T):
    nb = gate.shape[0]
    blk = lax.broadcasted_iota(jnp.int32, (nb, T), 0)
    pos = t * T + lax.broadcasted_iota(jnp.int32, (nb, T), 1)
    own = lax.shift_right_logical(pos, MOBA_BLOCK.bit_length() - 1)
    g = jnp.where(blk < own, gate, -jnp.inf)
    bias = jnp.full((nb, T), NEG, F32)
    for _ in range(MOBA_TOPK):
        mx = jnp.max(g, axis=0, keepdims=True)
        idx = jnp.min(jnp.where(g == mx, blk, nb), axis=0, keepdims=True)
        idx = jnp.where(mx > -jnp.inf, idx, -1)
        pick = blk == idx
        bias = jnp.where(pick, 0.0, bias)
        g = jnp.where(pick, -jnp.inf, g)
    return bias


def _moba_in_kernel(x_ref, g_ref, wqkv_t_ref, wm_t_ref, cos_ref, sin_ref, qg_ref, kg_ref, mqg_ref,
                    mk_ref, mvt_ref, qt_ref, k_ref, vt_ref, bias_ref, mo_ref, kmean_ref):
    t = pl.program_id(1)
    T = x_ref.shape[1]
    blocks_per_tile = T // MOBA_BLOCK

    @pl.when(t == 0)
    def _():
        kmean_ref[...] = jnp.zeros_like(kmean_ref)

    u = _rms_rows(x_ref[0], g_ref[...]).astype(BF16)
    cos, sin = cos_ref[...], sin_ref[...]

    def proj_t(lo, hi):
        return lax.dot_general(wqkv_t_ref[lo:hi, :], u, NT_DIMS, preferred_element_type=F32)

    k_t = proj_t(TOK_WIDTH, 2 * TOK_WIDTH)
    k_heads = [_rope_t(_head_rms_t(k_t[h * HEAD_DIM:(h + 1) * HEAD_DIM], kg_ref[...]), cos, sin)
               for h in range(MOBA_HEADS)]
    k_nat = jnp.concatenate(k_heads, axis=0).T
    k_ref[0] = k_nat.astype(BF16)
    for c in range(blocks_per_tile):
        kmean_ref[pl.ds(t * blocks_per_tile + c, 1), :] = jnp.mean(
            k_nat[c * MOBA_BLOCK:(c + 1) * MOBA_BLOCK], axis=0, keepdims=True)

    v_t = proj_t(2 * TOK_WIDTH, 3 * TOK_WIDTH)
    for c in range(blocks_per_tile):
        vt_ref[0, c] = v_t[:, c * MOBA_BLOCK:(c + 1) * MOBA_BLOCK].astype(BF16)

    q_t = proj_t(0, TOK_WIDTH)
    for h in range(MOBA_HEADS):
        rows = slice(h * HEAD_DIM, (h + 1) * HEAD_DIM)
        q = _rope_t(_head_rms_t(q_t[rows], qg_ref[...]), cos, sin)
        qt_ref[0, rows, :] = (q * (SM_SCALE * LOG2_E)).astype(BF16)
        gate = jnp.dot(kmean_ref[:, rows], q, preferred_element_type=F32,
                       precision=lax.Precision.HIGHEST)
        bias_ref[0, h] = _selection_bias(gate, t, T)

    qm_t = lax.dot_general(wm_t_ref[...], u, NT_DIMS, preferred_element_type=F32)
    mo_ref[0] = _mem_attention_t(qm_t, mqg_ref[...], mk_ref, mvt_ref).astype(BF16)


def _moba_in(x, g, wqkv_t, wm_t, cos_t, sin_t, q_gain, k_gain, mq_gain, mk, mvt):
    B, S, D = x.shape
    M = mk.shape[1]
    T = min(TOKEN_TILE, S)
    nb = S // MOBA_BLOCK
    const = lambda b, t: (0, 0)
    return pl.pallas_call(
        _moba_in_kernel,
        grid=(B, S // T),
        in_specs=[
            pl.BlockSpec((1, T, D), lambda b, t: (b, t, 0)),
            pl.BlockSpec((1, D), const),
            pl.BlockSpec((3 * TOK_WIDTH, D), const),
            pl.BlockSpec((MEM_WIDTH, D), const),
            pl.BlockSpec((HALF_DIM, T), lambda b, t: (0, t)),
            pl.BlockSpec((HALF_DIM, T), lambda b, t: (0, t)),
            pl.BlockSpec((HEAD_DIM, 1), const),
            pl.BlockSpec((HEAD_DIM, 1), const),
            pl.BlockSpec((HEAD_DIM, 1), const),
            pl.BlockSpec((1, M, MEM_WIDTH), lambda b, t: (b, 0, 0)),
            pl.BlockSpec((1, MEM_WIDTH, M), lambda b, t: (b, 0, 0)),
        ],
        out_specs=[
            pl.BlockSpec((1, TOK_WIDTH, T), lambda b, t: (b, 0, t)),
            pl.BlockSpec((1, T, TOK_WIDTH), lambda b, t: (b, t, 0)),
            pl.BlockSpec((1, T // MOBA_BLOCK, TOK_WIDTH, MOBA_BLOCK), lambda b, t: (b, t, 0, 0)),
            pl.BlockSpec((1, MOBA_HEADS, nb, T), lambda b, t: (b, 0, 0, t)),
            pl.BlockSpec((1, T, MEM_WIDTH), lambda b, t: (b, t, 0)),
        ],
        out_shape=[
            jax.ShapeDtypeStruct((B, TOK_WIDTH, S), BF16),
            jax.ShapeDtypeStruct((B, S, TOK_WIDTH), BF16),
            jax.ShapeDtypeStruct((B, nb, TOK_WIDTH, MOBA_BLOCK), BF16),
            jax.ShapeDtypeStruct((B, MOBA_HEADS, nb, S), F32),
            jax.ShapeDtypeStruct((B, S, MEM_WIDTH), BF16),
        ],
        scratch_shapes=[pltpu.VMEM((nb, TOK_WIDTH), F32)],
        compiler_params=_params(),
        name="moba_in",
    )(x, g, wqkv_t, wm_t, cos_t, sin_t, q_gain, k_gain, mq_gain, mk, mvt)


def _moba_attn_kernel(qt_ref, k_ref, vt_ref, bias_ref, onehot_ref, o_ref, *scratch):
    i = pl.program_id(2)
    TQ = qt_ref.shape[2]
    s_refs = (scratch[0:2], scratch[2:4])
    bm_refs = (scratch[4:6], scratch[6:8])
    group_keys = KV_GROUP * MOBA_BLOCK
    total_groups = k_ref.shape[1] // group_keys
    live_groups = (i + KV_GROUP - 1) // KV_GROUP

    q_pair = qt_ref[0]
    row = lax.broadcasted_iota(jnp.int32, q_pair.shape, 0)
    zero = jnp.zeros_like(q_pair)
    q_heads = (jnp.where(row < HEAD_DIM, q_pair, zero), jnp.where(row >= HEAD_DIM, q_pair, zero))

    zeros_tail = jnp.zeros((MXU_DEPTH - LANES - BIAS_ROWS, TQ), BF16)
    ones_rows = jnp.ones((SUM_ROWS, 1), BF16)

    def produce(g, parity, a, s_ref, bm_ref):
        gc = jnp.minimum(g, total_groups - 1)
        k_g = k_ref[0, pl.ds(pl.multiple_of(gc * group_keys, group_keys), group_keys), :]
        lhs = jnp.concatenate([k_g, onehot_ref[parity]], axis=1)
        penalty = jnp.where(g >= live_groups, NEG, 0.0)
        slab_row = pl.multiple_of((gc // 2) * 2 * KV_GROUP, 2 * KV_GROUP)
        slab = bias_ref[0, a, pl.ds(slab_row, 2 * KV_GROUP), :] + penalty
        slab = jnp.concatenate([slab, jnp.zeros_like(slab)], axis=0).astype(BF16)
        rhs = jnp.concatenate([q_heads[a], slab, zeros_tail], axis=0)
        s = jnp.dot(lhs, rhs, preferred_element_type=F32)
        s_ref[...] = s
        bm_ref[...] = jnp.max(s, axis=0, keepdims=True)

    def pv_operand(v_t):
        return jnp.concatenate([v_t, jnp.broadcast_to(ones_rows, (SUM_ROWS, v_t.shape[1]))], axis=0)

    def consume(g, a, s_ref, bm_ref, state):
        gc = jnp.minimum(g, total_groups - 1)
        j0 = gc * KV_GROUP
        rows = slice(a * HEAD_DIM, (a + 1) * HEAD_DIM)
        v_g = jnp.concatenate([vt_ref[0, j0 + c, rows, :] for c in range(KV_GROUP)], axis=1)
        m, acc = state
        m_new = jnp.maximum(m, bm_ref[...])
        alpha = jnp.exp2(m - m_new)
        p = jnp.exp2(s_ref[...] - m_new).astype(BF16)
        pv = jnp.dot(pv_operand(v_g), p, preferred_element_type=F32)
        return m_new, alpha * acc + pv

    k_own = k_ref[0, pl.ds(pl.multiple_of(i * MOBA_BLOCK, MOBA_BLOCK), MOBA_BLOCK), :]
    v_own = vt_ref[0, i]
    key_idx = lax.broadcasted_iota(jnp.int32, (MOBA_BLOCK, TQ), 0)
    qry_idx = lax.broadcasted_iota(jnp.int32, (MOBA_BLOCK, TQ), 1)
    causal = key_idx <= qry_idx
    s_own = [jnp.where(causal, jnp.dot(k_own, q_heads[a], preferred_element_type=F32), NEG)
             for a in range(2)]
    produce(0, 0, 0, s_refs[0][0], bm_refs[0][0])
    produce(0, 0, 1, s_refs[0][1], bm_refs[0][1])
    init = []
    for a in range(2):
        m = jnp.max(s_own[a], axis=0, keepdims=True)
        p = jnp.exp2(s_own[a] - m).astype(BF16)
        acc = jnp.dot(pv_operand(v_own[a * HEAD_DIM:(a + 1) * HEAD_DIM]), p, preferred_element_type=F32)
        init.append((m, acc))

    def body(step, carry):
        g = GROUPS_PER_STEP * step
        state = list(carry)
        for sub in range(GROUPS_PER_STEP):
            cur = sub % 2
            nxt = 1 - cur
            for a in range(2):
                produce(g + sub + 1, nxt, a, s_refs[nxt][a], bm_refs[nxt][a])
                state[a] = consume(g + sub, a, s_refs[cur][a], bm_refs[cur][a], state[a])
        return tuple(state)

    final = lax.fori_loop(0, (live_groups + GROUPS_PER_STEP - 1) // GROUPS_PER_STEP, body, tuple(init))
    o_t = jnp.concatenate([acc[:HEAD_DIM] / acc[HEAD_DIM:HEAD_DIM + 1] for (_, acc) in final], axis=0)
    o_ref[0] = o_t.T.astype(BF16)


def _block_onehot():
    key_block = jnp.arange(KV_GROUP * MOBA_BLOCK, dtype=jnp.int32) // MOBA_BLOCK
    col = jnp.arange(LANES, dtype=jnp.int32)
    parity = jnp.arange(2, dtype=jnp.int32)
    hit = col[None, None, :] == (parity[:, None, None] * KV_GROUP + key_block[None, :, None])
    return hit.astype(BF16)


def _moba_attn(qt, k, vt, bias):
    B, _, S = qt.shape
    nb = S // MOBA_BLOCK
    assert 2 * KV_GROUP == F32_SUBLANES and nb % F32_SUBLANES == 0, "bias slabs are whole f32 sublane tiles"
    pairs = MOBA_HEADS // 2
    group_keys = KV_GROUP * MOBA_BLOCK
    return pl.pallas_call(
        _moba_attn_kernel,
        grid=(B, pairs, nb),
        in_specs=[
            pl.BlockSpec((1, LANES, MOBA_BLOCK), lambda b, p, i: (b, p, i)),
            pl.BlockSpec((1, S, LANES), lambda b, p, i: (b, 0, p)),
            pl.BlockSpec((1, nb, LANES, MOBA_BLOCK), lambda b, p, i: (b, 0, p, 0)),
            pl.BlockSpec((1, 2, nb, MOBA_BLOCK), lambda b, p, i: (b, p, 0, i)),
            pl.BlockSpec((2, group_keys, LANES), lambda b, p, i: (0, 0, 0)),
        ],
        out_specs=pl.BlockSpec((1, MOBA_BLOCK, LANES), lambda b, p, i: (b, i, p)),
        out_shape=jax.ShapeDtypeStruct((B, S, TOK_WIDTH), BF16),
        scratch_shapes=[pltpu.VMEM((group_keys, MOBA_BLOCK), F32)] * 4 + [pltpu.VMEM((1, MOBA_BLOCK), F32)] * 4,
        compiler_params=pltpu.CompilerParams(
            dimension_semantics=("arbitrary", "arbitrary", "arbitrary"), vmem_limit_bytes=VMEM_LIMIT_BYTES),
        name="moba_attn",
    )(qt, k, vt, bias, _block_onehot())


def _post_kernel(x_ref, tok_ref, mo_ref, wo_ref, g_ref, w1_ref, w2_ref, o_ref):
    mixed = jnp.concatenate([tok_ref[0], mo_ref[0]], axis=-1)
    x1 = x_ref[0] + jnp.dot(mixed, wo_ref[...], preferred_element_type=F32)
    u = _rms_rows(x1, g_ref[...]).astype(BF16)
    h = jnp.dot(u, w1_ref[...], preferred_element_type=F32)
    h = jnp.square(jnp.maximum(h, 0.0)).astype(BF16)
    o_ref[0] = x1 + jnp.dot(h, w2_ref[...], preferred_element_type=F32)


def _post(x, tok, mo, wo, g, w1, w2):
    B, S, D = x.shape
    T = min(TOKEN_TILE, S)
    const = lambda b, t: (0, 0)
    resident = functools.partial(pl.BlockSpec, index_map=const, pipeline_mode=pl.Buffered(1))
    return pl.pallas_call(
        _post_kernel,
        grid=(B, S // T),
        in_specs=[
            pl.BlockSpec((1, T, D), lambda b, t: (b, t, 0)),
            pl.BlockSpec((1, T, TOK_WIDTH), lambda b, t: (b, t, 0)),
            pl.BlockSpec((1, T, MEM_WIDTH), lambda b, t: (b, t, 0)),
            resident((D, D)),
            pl.BlockSpec((1, D), const),
            resident((D, D_FF)),
            resident((D_FF, D)),
        ],
        out_specs=pl.BlockSpec((1, T, D), lambda b, t: (b, t, 0)),
        out_shape=jax.ShapeDtypeStruct((B, S, D), F32),
        compiler_params=_params(),
        name="post",
    )(x, tok, mo, wo, g, w1, w2)


def _rope_tables_t(seq_len):
    pos = jnp.arange(seq_len, dtype=F32)
    inv = ROPE_THETA ** (-jnp.arange(0, HEAD_DIM, 2, dtype=F32) / HEAD_DIM)
    ang = pos[:, None] * inv[None, :]
    return jnp.cos(ang).T, jnp.sin(ang).T


def _block_diag(w_group):
    G, C, _ = w_group.shape
    eye = jnp.eye(G, dtype=w_group.dtype)
    return (eye[:, None, :, None] * w_group[:, :, None, :]).reshape(G * C, G * C)


def kernel(x, mem, g_mix, g_mem, g_mlp, w_in_pool, w_pool_group, pool_scale, w_in_moba, moba_q_gain,
           moba_k_gain, w_mem_kv, mem_q_gain, mem_k_gain, w_out, w_ff1, w_ff2):
    depth = g_mix.shape[0]
    S = x.shape[1]
    cos_t, sin_t = _rope_tables_t(S)
    col = lambda v: v.reshape(-1, 1)
    row = lambda v: v.reshape(1, -1)
    for i in range(depth):
        j = i // 2
        mk, mvt = _mem_kv(mem, row(g_mem[i]), w_mem_kv[i].T.astype(BF16), col(mem_k_gain[i]))
        if i % 2 == 0:
            w = w_in_pool[j]
            tok, mo = _pool_in(
                x, row(g_mix[i]), w[:, :TOK_WIDTH].astype(BF16), w[:, TOK_WIDTH:].T.astype(BF16),
                _block_diag(w_pool_group[j]).astype(BF16), row(pool_scale[j]), col(mem_q_gain[i]), mk, mvt)
        else:
            w = w_in_moba[j]
            qt, k, vt, bias, mo = _moba_in(
                x, row(g_mix[i]), w[:, :3 * TOK_WIDTH].T.astype(BF16), w[:, 3 * TOK_WIDTH:].T.astype(BF16),
                cos_t, sin_t, col(moba_q_gain[j]), col(moba_k_gain[j]), col(mem_q_gain[i]), mk, mvt)
            tok = _moba_attn(qt, k, vt, bias)
        x = _post(x, tok, mo, w_out[i].astype(BF16), row(g_mlp[i]), w_ff1[i].astype(BF16), w_ff2[i].astype(BF16))
    return x
```

```python
import functools

import jax
import jax.numpy as jnp
from jax import lax
from jax.experimental import pallas as pl
from jax.experimental.pallas import tpu as pltpu

D_MODEL = 1024
HEAD_DIM = 64
HALF_DIM = HEAD_DIM // 2
MEM_HEADS = 4
MEM_WIDTH = MEM_HEADS * HEAD_DIM
TOK_WIDTH = D_MODEL - MEM_WIDTH
MOBA_HEADS = TOK_WIDTH // HEAD_DIM
MOBA_BLOCK = 256
MOBA_TOPK = 3
POOL_WINDOWS = (2, 4, 8, 16)
POOL_GROUP_WIDTH = TOK_WIDTH // len(POOL_WINDOWS)
POOL_HALO = 16
D_FF = 4 * D_MODEL
ROPE_THETA = 10000.0
EPS = 1e-6
NEG = -1e30
SM_SCALE = HEAD_DIM ** -0.5
LOG2_E = 1.4426950408889634

LANES = 128
F32_SUBLANES = 8
BIAS_ROWS = 16
SUM_ROWS = 16
MXU_DEPTH = 256
VMEM_LIMIT_BYTES = 56 * 1024 * 1024

TOKEN_TILE = 512
KV_GROUP = 4
LONG_STEP_GROUPS = 4
SHORT_STEP_GROUPS = 2

F32 = jnp.float32
BF16 = jnp.bfloat16
NT_DIMS = (((1,), (1,)), ((), ()))


def _params():
    return pltpu.CompilerParams(
        dimension_semantics=("arbitrary", "arbitrary"), vmem_limit_bytes=VMEM_LIMIT_BYTES)


def _rms_rows(x, g):
    ms = jnp.mean(x * x, axis=-1, keepdims=True)
    return x * lax.rsqrt(ms + EPS) * g


def _head_rms_t(h, gain_col):
    ms = jnp.mean(h * h, axis=0, keepdims=True)
    return h * lax.rsqrt(ms + EPS) * gain_col


def _rope_t(h, cos, sin):
    h1, h2 = h[:HALF_DIM], h[HALF_DIM:]
    return jnp.concatenate([h1 * cos - h2 * sin, h2 * cos + h1 * sin], axis=0)


def _pair_operand(q_bf, slot):
    z = jnp.zeros_like(q_bf)
    return jnp.concatenate([q_bf, z] if slot == 0 else [z, q_bf], axis=0)


def _mem_attention_t(qm_t, mq_gain, mk_ref, mvt_ref):
    outs = []
    for h in range(MEM_HEADS):
        pair, slot = divmod(h, 2)
        q = _head_rms_t(qm_t[h * HEAD_DIM:(h + 1) * HEAD_DIM], mq_gain)
        q2 = _pair_operand((q * SM_SCALE).astype(BF16), slot)
        k_pair = mk_ref[0, :, pair * LANES:(pair + 1) * LANES]
        s = jnp.dot(k_pair, q2, preferred_element_type=F32)
        m = jnp.max(s, axis=0, keepdims=True)
        e = jnp.exp(s - m)
        l = jnp.sum(e, axis=0, keepdims=True)
        v_t = mvt_ref[0, h * HEAD_DIM:(h + 1) * HEAD_DIM, :]
        o = jnp.dot(v_t, e.astype(BF16), preferred_element_type=F32)
        outs.append(o / l)
    return jnp.concatenate(outs, axis=0).T


def _mem_kv_kernel(mem_ref, g_ref, wkv_t_ref, kg_ref, mk_ref, mvt_ref):
    mem_n = _rms_rows(mem_ref[0], g_ref[...]).astype(BF16)
    kv_t = lax.dot_general(wkv_t_ref[...], mem_n, NT_DIMS, preferred_element_type=F32)
    k_heads = [_head_rms_t(kv_t[h * HEAD_DIM:(h + 1) * HEAD_DIM], kg_ref[...]) for h in range(MEM_HEADS)]
    mk_ref[0] = jnp.concatenate(k_heads, axis=0).T.astype(BF16)
    mvt_ref[0] = kv_t[MEM_WIDTH:].astype(BF16)


def _mem_kv(mem, g, wkv_t, k_gain):
    B, M, D = mem.shape
    return pl.pallas_call(
        _mem_kv_kernel,
        grid=(B, 1),
        in_specs=[
            pl.BlockSpec((1, M, D), lambda b, _: (b, 0, 0)),
            pl.BlockSpec((1, D), lambda b, _: (0, 0)),
            pl.BlockSpec((2 * MEM_WIDTH, D), lambda b, _: (0, 0)),
            pl.BlockSpec((HEAD_DIM, 1), lambda b, _: (0, 0)),
        ],
        out_specs=[
            pl.BlockSpec((1, M, MEM_WIDTH), lambda b, _: (b, 0, 0)),
            pl.BlockSpec((1, MEM_WIDTH, M), lambda b, _: (b, 0, 0)),
        ],
        out_shape=[
            jax.ShapeDtypeStruct((B, M, MEM_WIDTH), BF16),
            jax.ShapeDtypeStruct((B, MEM_WIDTH, M), BF16),
        ],
        compiler_params=_params(),
        name="mem_kv",
    )(mem, g, wkv_t, k_gain)


def _pool_in_kernel(x_ref, g_ref, w_tok_ref, wm_t_ref, wg_ref, scale_ref, mqg_ref, mk_ref, mvt_ref,
                    tok_ref, mo_ref, halo_ref):
    t = pl.program_id(1)
    T = x_ref.shape[1]

    @pl.when(t == 0)
    def _():
        halo_ref[...] = jnp.zeros_like(halo_ref)

    u = _rms_rows(x_ref[0], g_ref[...]).astype(BF16)
    h = jnp.dot(u, w_tok_ref[...], preferred_element_type=F32)

    ext = jnp.concatenate([halo_ref[...], h], axis=0)
    halo_ref[...] = h[T - POOL_HALO:]
    s2 = ext + pltpu.roll(ext, 1, 0)
    s4 = s2 + pltpu.roll(s2, 2, 0)
    s8 = s4 + pltpu.roll(s4, 4, 0)
    s16 = s8 + pltpu.roll(s8, 8, 0)
    lane = lax.broadcasted_iota(jnp.int32, (T, TOK_WIDTH), 1)
    pos = t * T + lax.broadcasted_iota(jnp.int32, (T, TOK_WIDTH), 0)
    sums = (s2, s4, s8, s16)
    win_sum = sums[-1][POOL_HALO:]
    window = jnp.full((T, TOK_WIDTH), POOL_WINDOWS[-1], jnp.int32)
    for gi in range(len(POOL_WINDOWS) - 2, -1, -1):
        in_group = lane < (gi + 1) * POOL_GROUP_WIDTH
        win_sum = jnp.where(in_group, sums[gi][POOL_HALO:], win_sum)
        window = jnp.where(in_group, POOL_WINDOWS[gi], window)
    count = jnp.minimum(pos + 1, window).astype(F32)
    d = (win_sum / count - h).astype(BF16)
    y = jnp.dot(d, wg_ref[...], preferred_element_type=F32) * scale_ref[...]
    tok_ref[0] = y.astype(BF16)

    qm_t = lax.dot_general(wm_t_ref[...], u, NT_DIMS, preferred_element_type=F32)
    mo_ref[0] = _mem_attention_t(qm_t, mqg_ref[...], mk_ref, mvt_ref).astype(BF16)


def _pool_in(x, g, w_tok, wm_t, wg, scale, mq_gain, mk, mvt):
    B, S, D = x.shape
    M = mk.shape[1]
    T = min(TOKEN_TILE, S)
    const = lambda b, t: (0, 0)
    return pl.pallas_call(
        _pool_in_kernel,
        grid=(B, S // T),
        in_specs=[
            pl.BlockSpec((1, T, D), lambda b, t: (b, t, 0)),
            pl.BlockSpec((1, D), const),
            pl.BlockSpec((D, TOK_WIDTH), const),
            pl.BlockSpec((MEM_WIDTH, D), const),
            pl.BlockSpec((TOK_WIDTH, TOK_WIDTH), const),
            pl.BlockSpec((1, TOK_WIDTH), const),
            pl.BlockSpec((HEAD_DIM, 1), const),
            pl.BlockSpec((1, M, MEM_WIDTH), lambda b, t: (b, 0, 0)),
            pl.BlockSpec((1, MEM_WIDTH, M), lambda b, t: (b, 0, 0)),
        ],
        out_specs=[
            pl.BlockSpec((1, T, TOK_WIDTH), lambda b, t: (b, t, 0)),
            pl.BlockSpec((1, T, MEM_WIDTH), lambda b, t: (b, t, 0)),
        ],
        out_shape=[
            jax.ShapeDtypeStruct((B, S, TOK_WIDTH), BF16),
            jax.ShapeDtypeStruct((B, S, MEM_WIDTH), BF16),
        ],
        scratch_shapes=[pltpu.VMEM((POOL_HALO, TOK_WIDTH), F32)],
        compiler_params=_params(),
        name="pool_in",
    )(x, g, w_tok, wm_t, wg, scale, mq_gain, mk, mvt)


def _selection_bias(gate, t, T):
    nb = gate.shape[0]
    blk = lax.broadcasted_iota(jnp.int32, (nb, T), 0)
    pos = t * T + lax.broadcasted_iota(jnp.int32, (nb, T), 1)
    own = lax.shift_right_logical(pos, MOBA_BLOCK.bit_length() - 1)
    g = jnp.where(blk < own, gate, -jnp.inf)
    bias = jnp.full((nb, T), NEG, F32)
    for _ in range(MOBA_TOPK):
        mx = jnp.max(g, axis=0, keepdims=True)
        idx = jnp.min(jnp.where(g == mx, blk, nb), axis=0, keepdims=True)
        idx = jnp.where(mx > -jnp.inf, idx, -1)
        pick = blk == idx
        bias = jnp.where(pick, 0.0, bias)
        g = jnp.where(pick, -jnp.inf, g)
    return bias


def _moba_in_kernel(x_ref, g_ref, wqkv_t_ref, wm_t_ref, cos_ref, sin_ref, qg_ref, kg_ref, mqg_ref,
                    mk_ref, mvt_ref, qt_ref, k_ref, vt_ref, bias_ref, mo_ref, kmean_ref):
    t = pl.program_id(1)
    T = x_ref.shape[1]
    blocks_per_tile = T // MOBA_BLOCK

    @pl.when(t == 0)
    def _():
        kmean_ref[...] = jnp.zeros_like(kmean_ref)

    u = _rms_rows(x_ref[0], g_ref[...]).astype(BF16)
    cos, sin = cos_ref[...], sin_ref[...]

    def proj_t(lo, hi):
        return lax.dot_general(wqkv_t_ref[lo:hi, :], u, NT_DIMS, preferred_element_type=F32)

    k_t = proj_t(TOK_WIDTH, 2 * TOK_WIDTH)
    k_heads = [_rope_t(_head_rms_t(k_t[h * HEAD_DIM:(h + 1) * HEAD_DIM], kg_ref[...]), cos, sin)
               for h in range(MOBA_HEADS)]
    k_nat = jnp.concatenate(k_heads, axis=0).T
    k_ref[0] = k_nat.astype(BF16)
    for c in range(blocks_per_tile):
        kmean_ref[pl.ds(t * blocks_per_tile + c, 1), :] = jnp.mean(
            k_nat[c * MOBA_BLOCK:(c + 1) * MOBA_BLOCK], axis=0, keepdims=True)

    v_t = proj_t(2 * TOK_WIDTH, 3 * TOK_WIDTH)
    for c in range(blocks_per_tile):
        vt_ref[0, c] = v_t[:, c * MOBA_BLOCK:(c + 1) * MOBA_BLOCK].astype(BF16)

    q_t = proj_t(0, TOK_WIDTH)
    for h in range(MOBA_HEADS):
        rows = slice(h * HEAD_DIM, (h + 1) * HEAD_DIM)
        q = _rope_t(_head_rms_t(q_t[rows], qg_ref[...]), cos, sin)
        qt_ref[0, rows, :] = (q * (SM_SCALE * LOG2_E)).astype(BF16)
        gate = jnp.dot(kmean_ref[:, rows], q, preferred_element_type=F32,
                       precision=lax.Precision.HIGHEST)
        bias_ref[0, h] = _selection_bias(gate, t, T)

    qm_t = lax.dot_general(wm_t_ref[...], u, NT_DIMS, preferred_element_type=F32)
    mo_ref[0] = _mem_attention_t(qm_t, mqg_ref[...], mk_ref, mvt_ref).astype(BF16)


def _moba_in(x, g, wqkv_t, wm_t, cos_t, sin_t, q_gain, k_gain, mq_gain, mk, mvt):
    B, S, D = x.shape
    M = mk.shape[1]
    T = min(TOKEN_TILE, S)
    nb = S // MOBA_BLOCK
    const = lambda b, t: (0, 0)
    return pl.pallas_call(
        _moba_in_kernel,
        grid=(B, S // T),
        in_specs=[
            pl.BlockSpec((1, T, D), lambda b, t: (b, t, 0)),
            pl.BlockSpec((1, D), const),
            pl.BlockSpec((3 * TOK_WIDTH, D), const),
            pl.BlockSpec((MEM_WIDTH, D), const),
            pl.BlockSpec((HALF_DIM, T), lambda b, t: (0, t)),
            pl.BlockSpec((HALF_DIM, T), lambda b, t: (0, t)),
            pl.BlockSpec((HEAD_DIM, 1), const),
            pl.BlockSpec((HEAD_DIM, 1), const),
            pl.BlockSpec((HEAD_DIM, 1), const),
            pl.BlockSpec((1, M, MEM_WIDTH), lambda b, t: (b, 0, 0)),
            pl.BlockSpec((1, MEM_WIDTH, M), lambda b, t: (b, 0, 0)),
        ],
        out_specs=[
            pl.BlockSpec((1, TOK_WIDTH, T), lambda b, t: (b, 0, t)),
            pl.BlockSpec((1, T, TOK_WIDTH), lambda b, t: (b, t, 0)),
            pl.BlockSpec((1, T // MOBA_BLOCK, TOK_WIDTH, MOBA_BLOCK), lambda b, t: (b, t, 0, 0)),
            pl.BlockSpec((1, MOBA_HEADS, nb, T), lambda b, t: (b, 0, 0, t)),
            pl.BlockSpec((1, T, MEM_WIDTH), lambda b, t: (b, t, 0)),
        ],
        out_shape=[
            jax.ShapeDtypeStruct((B, TOK_WIDTH, S), BF16),
            jax.ShapeDtypeStruct((B, S, TOK_WIDTH), BF16),
            jax.ShapeDtypeStruct((B, nb, TOK_WIDTH, MOBA_BLOCK), BF16),
            jax.ShapeDtypeStruct((B, MOBA_HEADS, nb, S), F32),
            jax.ShapeDtypeStruct((B, S, MEM_WIDTH), BF16),
        ],
        scratch_shapes=[pltpu.VMEM((nb, TOK_WIDTH), F32)],
        compiler_params=_params(),
        name="moba_in",
    )(x, g, wqkv_t, wm_t, cos_t, sin_t, q_gain, k_gain, mq_gain, mk, mvt)


def _moba_attn_kernel(qt_ref, k_ref, vt_ref, bias_ref, onehot_ref, o_ref, *scratch):
    i = pl.program_id(2)
    TQ = qt_ref.shape[2]
    s_refs = (scratch[0:2], scratch[2:4])
    bm_refs = (scratch[4:6], scratch[6:8])
    group_keys = KV_GROUP * MOBA_BLOCK
    total_groups = k_ref.shape[1] // group_keys
    live_groups = (i + KV_GROUP - 1) // KV_GROUP

    q_pair = qt_ref[0]
    row = lax.broadcasted_iota(jnp.int32, q_pair.shape, 0)
    zero = jnp.zeros_like(q_pair)
    q_heads = (jnp.where(row < HEAD_DIM, q_pair, zero), jnp.where(row >= HEAD_DIM, q_pair, zero))

    zeros_tail = jnp.zeros((MXU_DEPTH - LANES - BIAS_ROWS, TQ), BF16)
    ones_rows = jnp.ones((SUM_ROWS, 1), BF16)

    def produce(g, parity, a, s_ref, bm_ref):
        gc = jnp.minimum(g, total_groups - 1)
        k_g = k_ref[0, pl.ds(pl.multiple_of(gc * group_keys, group_keys), group_keys), :]
        lhs = jnp.concatenate([k_g, onehot_ref[parity]], axis=1)
        penalty = jnp.where(g >= live_groups, NEG, 0.0)
        slab_row = pl.multiple_of((gc // 2) * 2 * KV_GROUP, 2 * KV_GROUP)
        slab = bias_ref[0, a, pl.ds(slab_row, 2 * KV_GROUP), :] + penalty
        slab = jnp.concatenate([slab, jnp.zeros_like(slab)], axis=0).astype(BF16)
        rhs = jnp.concatenate([q_heads[a], slab, zeros_tail], axis=0)
        s = jnp.dot(lhs, rhs, preferred_element_type=F32)
        s_ref[...] = s
        bm_ref[...] = jnp.max(s, axis=0, keepdims=True)

    def pv_operand(v_t):
        return jnp.concatenate([v_t, jnp.broadcast_to(ones_rows, (SUM_ROWS, v_t.shape[1]))], axis=0)

    def consume(g, a, s_ref, bm_ref, state):
        gc = jnp.minimum(g, total_groups - 1)
        j0 = gc * KV_GROUP
        rows = slice(a * HEAD_DIM, (a + 1) * HEAD_DIM)
        v_g = jnp.concatenate([vt_ref[0, j0 + c, rows, :] for c in range(KV_GROUP)], axis=1)
        m, acc = state
        m_new = jnp.maximum(m, bm_ref[...])
        alpha = jnp.exp2(m - m_new)
        p = jnp.exp2(s_ref[...] - m_new).astype(BF16)
        pv = jnp.dot(pv_operand(v_g), p, preferred_element_type=F32)
        return m_new, alpha * acc + pv

    k_own = k_ref[0, pl.ds(pl.multiple_of(i * MOBA_BLOCK, MOBA_BLOCK), MOBA_BLOCK), :]
    v_own = vt_ref[0, i]
    key_idx = lax.broadcasted_iota(jnp.int32, (MOBA_BLOCK, TQ), 0)
    qry_idx = lax.broadcasted_iota(jnp.int32, (MOBA_BLOCK, TQ), 1)
    causal = key_idx <= qry_idx
    s_own = [jnp.where(causal, jnp.dot(k_own, q_heads[a], preferred_element_type=F32), NEG)
             for a in range(2)]
    produce(0, 0, 0, s_refs[0][0], bm_refs[0][0])
    produce(0, 0, 1, s_refs[0][1], bm_refs[0][1])
    init = []
    for a in range(2):
        m = jnp.max(s_own[a], axis=0, keepdims=True)
        p = jnp.exp2(s_own[a] - m).astype(BF16)
        acc = jnp.dot(pv_operand(v_own[a * HEAD_DIM:(a + 1) * HEAD_DIM]), p, preferred_element_type=F32)
        init.append((m, acc))

    def run_steps(groups_per_step, first_group, n_steps, carry):
        def body(step, carry):
            g = first_group + groups_per_step * step
            state = list(carry)
            for sub in range(groups_per_step):
                cur = sub % 2
                nxt = 1 - cur
                for a in range(2):
                    produce(g + sub + 1, nxt, a, s_refs[nxt][a], bm_refs[nxt][a])
                    state[a] = consume(g + sub, a, s_refs[cur][a], bm_refs[cur][a], state[a])
            return tuple(state)
        return lax.fori_loop(0, n_steps, body, carry)

    long_steps = live_groups // LONG_STEP_GROUPS
    done = long_steps * LONG_STEP_GROUPS
    carry = run_steps(LONG_STEP_GROUPS, 0, long_steps, tuple(init))
    final = run_steps(SHORT_STEP_GROUPS, done, (live_groups - done + SHORT_STEP_GROUPS - 1) // SHORT_STEP_GROUPS, carry)
    o_t = jnp.concatenate([acc[:HEAD_DIM] / acc[HEAD_DIM:HEAD_DIM + 1] for (_, acc) in final], axis=0)
    o_ref[0] = o_t.T.astype(BF16)


def _block_onehot():
    key_block = jnp.arange(KV_GROUP * MOBA_BLOCK, dtype=jnp.int32) // MOBA_BLOCK
    col = jnp.arange(LANES, dtype=jnp.int32)
    parity = jnp.arange(2, dtype=jnp.int32)
    hit = col[None, None, :] == (parity[:, None, None] * KV_GROUP + key_block[None, :, None])
    return hit.astype(BF16)


def _moba_attn(qt, k, vt, bias):
    B, _, S = qt.shape
    nb = S // MOBA_BLOCK
    assert 2 * KV_GROUP == F32_SUBLANES and nb % F32_SUBLANES == 0, "bias slabs are whole f32 sublane tiles"
    pairs = MOBA_HEADS // 2
    group_keys = KV_GROUP * MOBA_BLOCK
    return pl.pallas_call(
        _moba_attn_kernel,
        grid=(B, pairs, nb),
        in_specs=[
            pl.BlockSpec((1, LANES, MOBA_BLOCK), lambda b, p, i: (b, p, i)),
            pl.BlockSpec((1, S, LANES), lambda b, p, i: (b, 0, p)),
            pl.BlockSpec((1, nb, LANES, MOBA_BLOCK), lambda b, p, i: (b, 0, p, 0)),
            pl.BlockSpec((1, 2, nb, MOBA_BLOCK), lambda b, p, i: (b, p, 0, i)),
            pl.BlockSpec((2, group_keys, LANES), lambda b, p, i: (0, 0, 0)),
        ],
        out_specs=pl.BlockSpec((1, MOBA_BLOCK, LANES), lambda b, p, i: (b, i, p)),
        out_shape=jax.ShapeDtypeStruct((B, S, TOK_WIDTH), BF16),
        scratch_shapes=[pltpu.VMEM((group_keys, MOBA_BLOCK), F32)] * 4 + [pltpu.VMEM((1, MOBA_BLOCK), F32)] * 4,
        compiler_params=pltpu.CompilerParams(
            dimension_semantics=("arbitrary", "arbitrary", "arbitrary"), vmem_limit_bytes=VMEM_LIMIT_BYTES),
        name="moba_attn",
    )(qt, k, vt, bias, _block_onehot())


def _post_kernel(x_ref, tok_ref, mo_ref, wo_ref, g_ref, w1_ref, w2_ref, o_ref):
    mixed = jnp.concatenate([tok_ref[0], mo_ref[0]], axis=-1)
    x1 = x_ref[0] + jnp.dot(mixed, wo_ref[...], preferred_element_type=F32)
    u = _rms_rows(x1, g_ref[...]).astype(BF16)
    h = jnp.dot(u, w1_ref[...], preferred_element_type=F32)
    h = jnp.square(jnp.maximum(h, 0.0)).astype(BF16)
    o_ref[0] = x1 + jnp.dot(h, w2_ref[...], preferred_element_type=F32)


def _post(x, tok, mo, wo, g, w1, w2):
    B, S, D = x.shape
    T = min(TOKEN_TILE, S)
    const = lambda b, t: (0, 0)
    resident = functools.partial(pl.BlockSpec, index_map=const, pipeline_mode=pl.Buffered(1))
    return pl.pallas_call(
        _post_kernel,
        grid=(B, S // T),
        in_specs=[
            pl.BlockSpec((1, T, D), lambda b, t: (b, t, 0)),
            pl.BlockSpec((1, T, TOK_WIDTH), lambda b, t: (b, t, 0)),
            pl.BlockSpec((1, T, MEM_WIDTH), lambda b, t: (b, t, 0)),
            resident((D, D)),
            pl.BlockSpec((1, D), const),
            resident((D, D_FF)),
            resident((D_FF, D)),
        ],
        out_specs=pl.BlockSpec((1, T, D), lambda b, t: (b, t, 0)),
        out_shape=jax.ShapeDtypeStruct((B, S, D), F32),
        compiler_params=_params(),
        name="post",
    )(x, tok, mo, wo, g, w1, w2)


def _rope_tables_t(seq_len):
    pos = jnp.arange(seq_len, dtype=F32)
    inv = ROPE_THETA ** (-jnp.arange(0, HEAD_DIM, 2, dtype=F32) / HEAD_DIM)
    ang = pos[:, None] * inv[None, :]
    return jnp.cos(ang).T, jnp.sin(ang).T


def _block_diag(w_group):
    G, C, _ = w_group.shape
    eye = jnp.eye(G, dtype=w_group.dtype)
    return (eye[:, None, :, None] * w_group[:, :, None, :]).reshape(G * C, G * C)


def kernel(x, mem, g_mix, g_mem, g_mlp, w_in_pool, w_pool_group, pool_scale, w_in_moba, moba_q_gain,
           moba_k_gain, w_mem_kv, mem_q_gain, mem_k_gain, w_out, w_ff1, w_ff2):
    depth = g_mix.shape[0]
    S = x.shape[1]
    cos_t, sin_t = _rope_tables_t(S)
    col = lambda v: v.reshape(-1, 1)
    row = lambda v: v.reshape(1, -1)
    for i in range(depth):
        j = i // 2
        mk, mvt = _mem_kv(mem, row(g_mem[i]), w_mem_kv[i].T.astype(BF16), col(mem_k_gain[i]))
        if i % 2 == 0:
            w = w_in_pool[j]
            tok, mo = _pool_in(
                x, row(g_mix[i]), w[:, :TOK_WIDTH].astype(BF16), w[:, TOK_WIDTH:].T.astype(BF16),
                _block_diag(w_pool_group[j]).astype(BF16), row(pool_scale[j]), col(mem_q_gain[i]), mk, mvt)
        else:
            w = w_in_moba[j]
            qt, k, vt, bias, mo = _moba_in(
                x, row(g_mix[i]), w[:, :3 * TOK_WIDTH].T.astype(BF16), w[:, 3 * TOK_WIDTH:].T.astype(BF16),
                cos_t, sin_t, col(moba_q_gain[j]), col(moba_k_gain[j]), col(mem_q_gain[i]), mk, mvt)
            tok = _moba_attn(qt, k, vt, bias)
        x = _post(x, tok, mo, w_out[i].astype(BF16), row(g_mlp[i]), w_ff1[i].astype(BF16), w_ff2[i].astype(BF16))
    return x
```

```python
import functools

import jax
import jax.numpy as jnp
from jax import lax
from jax.experimental import pallas as pl
from jax.experimental.pallas import tpu as pltpu

D_MODEL = 1024
HEAD_DIM = 64
HALF_DIM = HEAD_DIM // 2
MEM_HEADS = 4
MEM_WIDTH = MEM_HEADS * HEAD_DIM
TOK_WIDTH = D_MODEL - MEM_WIDTH
MOBA_HEADS = TOK_WIDTH // HEAD_DIM
MOBA_BLOCK = 256
MOBA_TOPK = 3
POOL_WINDOWS = (2, 4, 8, 16)
POOL_GROUP_WIDTH = TOK_WIDTH // len(POOL_WINDOWS)
POOL_HALO = 16
D_FF = 4 * D_MODEL
ROPE_THETA = 10000.0
EPS = 1e-6
NEG = -1e30
SM_SCALE = HEAD_DIM ** -0.5
LOG2_E = 1.4426950408889634

LANES = 128
F32_SUBLANES = 8
BIAS_ROWS = 16
SUM_ROWS = 16
MXU_DEPTH = 256
VMEM_LIMIT_BYTES = 56 * 1024 * 1024

TOKEN_TILE = 512
VALUE_CHUNKS = 3
KV_GROUP = 4
STEP_GROUPS = (8, 4, 2)

F32 = jnp.float32
BF16 = jnp.bfloat16
NT_DIMS = (((1,), (1,)), ((), ()))


def _params():
    return pltpu.CompilerParams(
        dimension_semantics=("arbitrary", "arbitrary"), vmem_limit_bytes=VMEM_LIMIT_BYTES)


def _rms_rows(x, g):
    ms = jnp.mean(x * x, axis=-1, keepdims=True)
    return x * lax.rsqrt(ms + EPS) * g


def _head_rms_t(h, gain_col):
    ms = jnp.mean(h * h, axis=0, keepdims=True)
    return h * lax.rsqrt(ms + EPS) * gain_col


def _rope_t(h, cos, sin):
    h1, h2 = h[:HALF_DIM], h[HALF_DIM:]
    return jnp.concatenate([h1 * cos - h2 * sin, h2 * cos + h1 * sin], axis=0)


def _split_bf16(x):
    hi = x.astype(BF16)
    return hi, (x - hi.astype(F32)).astype(BF16)


def _pair_operand(q_bf, slot):
    z = jnp.zeros_like(q_bf)
    return jnp.concatenate([q_bf, z] if slot == 0 else [z, q_bf], axis=0)


def _mem_attention_t(qm_t, mq_gain, mk_ref, mvt_ref):
    scores = []
    for h in range(MEM_HEADS):
        pair, slot = divmod(h, 2)
        q = _head_rms_t(qm_t[h * HEAD_DIM:(h + 1) * HEAD_DIM], mq_gain)
        q2 = _pair_operand((q * (SM_SCALE * LOG2_E)).astype(BF16), slot)
        k_pair = mk_ref[0, :, pair * LANES:(pair + 1) * LANES]
        scores.append(jnp.dot(k_pair, q2, preferred_element_type=F32))
    probs = [jnp.exp2(s - jnp.max(s, axis=0, keepdims=True)).astype(BF16) for s in scores]
    outs = []
    for h in range(MEM_HEADS):
        v_t = mvt_ref[0, h * HEAD_DIM:(h + 1) * HEAD_DIM, :]
        ones = jnp.ones((SUM_ROWS, v_t.shape[1]), BF16)
        r = jnp.dot(jnp.concatenate([v_t, ones], axis=0), probs[h], preferred_element_type=F32)
        outs.append(r[:HEAD_DIM] / r[HEAD_DIM:HEAD_DIM + 1])
    return jnp.concatenate(outs, axis=0).T


def _mem_kv_kernel(mem_ref, g_ref, wkv_t_ref, kg_ref, mk_ref, mvt_ref):
    mem_n = _rms_rows(mem_ref[0], g_ref[...]).astype(BF16)
    kv_t = lax.dot_general(wkv_t_ref[...], mem_n, NT_DIMS, preferred_element_type=F32)
    k_heads = [_head_rms_t(kv_t[h * HEAD_DIM:(h + 1) * HEAD_DIM], kg_ref[...]) for h in range(MEM_HEADS)]
    mk_ref[0] = jnp.concatenate(k_heads, axis=0).T.astype(BF16)
    mvt_ref[0] = kv_t[MEM_WIDTH:].astype(BF16)


def _mem_kv(mem, g, wkv_t, k_gain):
    B, M, D = mem.shape
    return pl.pallas_call(
        _mem_kv_kernel,
        grid=(B, 1),
        in_specs=[
            pl.BlockSpec((1, M, D), lambda b, _: (b, 0, 0)),
            pl.BlockSpec((1, D), lambda b, _: (0, 0)),
            pl.BlockSpec((2 * MEM_WIDTH, D), lambda b, _: (0, 0)),
            pl.BlockSpec((HEAD_DIM, 1), lambda b, _: (0, 0)),
        ],
        out_specs=[
            pl.BlockSpec((1, M, MEM_WIDTH), lambda b, _: (b, 0, 0)),
            pl.BlockSpec((1, MEM_WIDTH, M), lambda b, _: (b, 0, 0)),
        ],
        out_shape=[
            jax.ShapeDtypeStruct((B, M, MEM_WIDTH), BF16),
            jax.ShapeDtypeStruct((B, MEM_WIDTH, M), BF16),
        ],
        compiler_params=_params(),
        name="mem_kv",
    )(mem, g, wkv_t, k_gain)


def _pool_in_kernel(x_ref, g_ref, w_tok_ref, wm_t_ref, wg_ref, scale_ref, mqg_ref, mk_ref, mvt_ref,
                    tok_ref, mo_ref, halo_ref):
    t = pl.program_id(1)
    T = x_ref.shape[1]

    @pl.when(t == 0)
    def _():
        halo_ref[...] = jnp.zeros_like(halo_ref)

    u = _rms_rows(x_ref[0], g_ref[...]).astype(BF16)
    h = jnp.dot(u, w_tok_ref[...], preferred_element_type=F32)

    ext = jnp.concatenate([halo_ref[...], h], axis=0)
    halo_ref[...] = h[T - POOL_HALO:]
    s2 = ext + pltpu.roll(ext, 1, 0)
    s4 = s2 + pltpu.roll(s2, 2, 0)
    s8 = s4 + pltpu.roll(s4, 4, 0)
    s16 = s8 + pltpu.roll(s8, 8, 0)
    lane = lax.broadcasted_iota(jnp.int32, (T, TOK_WIDTH), 1)
    pos = t * T + lax.broadcasted_iota(jnp.int32, (T, TOK_WIDTH), 0)
    sums = (s2, s4, s8, s16)
    win_sum = sums[-1][POOL_HALO:]
    window = jnp.full((T, TOK_WIDTH), POOL_WINDOWS[-1], jnp.int32)
    for gi in range(len(POOL_WINDOWS) - 2, -1, -1):
        in_group = lane < (gi + 1) * POOL_GROUP_WIDTH
        win_sum = jnp.where(in_group, sums[gi][POOL_HALO:], win_sum)
        window = jnp.where(in_group, POOL_WINDOWS[gi], window)
    count = jnp.minimum(pos + 1, window).astype(F32)
    d = (win_sum / count - h).astype(BF16)
    y = jnp.dot(d, wg_ref[...], preferred_element_type=F32) * scale_ref[...]
    tok_ref[0] = y.astype(BF16)

    qm_t = lax.dot_general(wm_t_ref[...], u, NT_DIMS, preferred_element_type=F32)
    mo_ref[0] = _mem_attention_t(qm_t, mqg_ref[...], mk_ref, mvt_ref).astype(BF16)


def _pool_in(x, g, w_tok, wm_t, wg, scale, mq_gain, mk, mvt):
    B, S, D = x.shape
    M = mk.shape[1]
    T = min(TOKEN_TILE, S)
    const = lambda b, t: (0, 0)
    return pl.pallas_call(
        _pool_in_kernel,
        grid=(B, S // T),
        in_specs=[
            pl.BlockSpec((1, T, D), lambda b, t: (b, t, 0)),
            pl.BlockSpec((1, D), const),
            pl.BlockSpec((D, TOK_WIDTH), const),
            pl.BlockSpec((MEM_WIDTH, D), const),
            pl.BlockSpec((TOK_WIDTH, TOK_WIDTH), const),
            pl.BlockSpec((1, TOK_WIDTH), const),
            pl.BlockSpec((HEAD_DIM, 1), const),
            pl.BlockSpec((1, M, MEM_WIDTH), lambda b, t: (b, 0, 0)),
            pl.BlockSpec((1, MEM_WIDTH, M), lambda b, t: (b, 0, 0)),
        ],
        out_specs=[
            pl.BlockSpec((1, T, TOK_WIDTH), lambda b, t: (b, t, 0)),
            pl.BlockSpec((1, T, MEM_WIDTH), lambda b, t: (b, t, 0)),
        ],
        out_shape=[
            jax.ShapeDtypeStruct((B, S, TOK_WIDTH), BF16),
            jax.ShapeDtypeStruct((B, S, MEM_WIDTH), BF16),
        ],
        scratch_shapes=[pltpu.VMEM((POOL_HALO, TOK_WIDTH), F32)],
        compiler_params=_params(),
        name="pool_in",
    )(x, g, w_tok, wm_t, wg, scale, mq_gain, mk, mvt)


def _selection_bias(gate, t, T):
    nb, width = gate.shape
    blk = lax.broadcasted_iota(jnp.int32, (nb, width), 0)
    pos = t * T + lax.rem(lax.broadcasted_iota(jnp.int32, (nb, width), 1), T)
    own = lax.shift_right_logical(pos, MOBA_BLOCK.bit_length() - 1)
    g = jnp.where(blk < own, gate, -jnp.inf)
    bias = jnp.full((nb, width), NEG, F32)
    for _ in range(MOBA_TOPK):
        mx = jnp.max(g, axis=0, keepdims=True)
        idx = jnp.min(jnp.where(g == mx, blk, nb), axis=0, keepdims=True)
        idx = jnp.where(mx > -jnp.inf, idx, -1)
        pick = blk == idx
        bias = jnp.where(pick, 0.0, bias)
        g = jnp.where(pick, -jnp.inf, g)
    return bias


def _moba_in_kernel(x_ref, g_ref, wqkv_t_ref, wm_t_ref, cos_ref, sin_ref, qg_ref, kg_ref, mqg_ref,
                    mk_ref, mvt_ref, qt_ref, k_ref, vt_ref, bias_ref, mo_ref, kmean_ref):
    t = pl.program_id(1)
    T = x_ref.shape[1]
    blocks_per_tile = T // MOBA_BLOCK

    @pl.when(t == 0)
    def _():
        kmean_ref[...] = jnp.zeros_like(kmean_ref)

    u = _rms_rows(x_ref[0], g_ref[...]).astype(BF16)
    cos, sin = cos_ref[...], sin_ref[...]

    def proj_t(lo, hi):
        return lax.dot_general(wqkv_t_ref[lo:hi, :], u, NT_DIMS, preferred_element_type=F32)

    k_t = proj_t(TOK_WIDTH, 2 * TOK_WIDTH)
    q_t = proj_t(0, TOK_WIDTH)

    k_heads =[_rope_t(_head_rms_t(k_t[h * HEAD_DIM:(h + 1) * HEAD_DIM], kg_ref[...]), cos, sin)
               for h in range(MOBA_HEADS)]
    k_nat = jnp.concatenate(k_heads, axis=0).T
    k_ref[0] = k_nat.astype(BF16)
    for c in range(blocks_per_tile):
        kmean_ref[pl.ds(t * blocks_per_tile + c, 1), :] = jnp.mean(
            k_nat[c * MOBA_BLOCK:(c + 1) * MOBA_BLOCK], axis=0, keepdims=True)

    km_hi, km_lo = _split_bf16(kmean_ref[...])
    heads_per_chunk = MOBA_HEADS // VALUE_CHUNKS
    chunk_rows = TOK_WIDTH // VALUE_CHUNKS
    for chunk in range(VALUE_CHUNKS):
        lo = 2 * TOK_WIDTH + chunk * chunk_rows
        v_t = proj_t(lo, lo + chunk_rows)
        for c in range(blocks_per_tile):
            vt_ref[0, c, chunk * chunk_rows:(chunk + 1) * chunk_rows, :] = (
                v_t[:, c * MOBA_BLOCK:(c + 1) * MOBA_BLOCK].astype(BF16))
        heads = range(chunk * heads_per_chunk, (chunk + 1) * heads_per_chunk)
        gates = []
        for h in heads:
            rows = slice(h * HEAD_DIM, (h + 1) * HEAD_DIM)
            pair, slot = divmod(h, 2)
            lanes = slice(pair * LANES, (pair + 1) * LANES)
            q = _rope_t(_head_rms_t(q_t[rows], qg_ref[...]), cos, sin)
            qt_ref[0, rows, :] = (q * (SM_SCALE * LOG2_E)).astype(BF16)
            q_hi, q_lo = _split_bf16(q)
            q_hi2, q_lo2 = _pair_operand(q_hi, slot), _pair_operand(q_lo, slot)
            gates.append(jnp.dot(jnp.concatenate([km_hi[:, lanes], km_lo[:, lanes]], axis=1),
                                 jnp.concatenate([q_hi2, q_hi2], axis=0), preferred_element_type=F32)
                         + jnp.dot(km_hi[:, lanes], q_lo2, preferred_element_type=F32))
        bias = _selection_bias(jnp.concatenate(gates, axis=1), t, T)
        for n, h in enumerate(heads):
            bias_ref[0, h] = bias[:, n * T:(n + 1) * T]

    qm_t = lax.dot_general(wm_t_ref[...], u, NT_DIMS, preferred_element_type=F32)
    mo_ref[0] = _mem_attention_t(qm_t, mqg_ref[...], mk_ref, mvt_ref).astype(BF16)


def _moba_in(x, g, wqkv_t, wm_t, cos_t, sin_t, q_gain, k_gain, mq_gain, mk, mvt):
    B, S, D = x.shape
    M = mk.shape[1]
    T = min(TOKEN_TILE, S)
    nb = S // MOBA_BLOCK
    const = lambda b, t: (0, 0)
    return pl.pallas_call(
        _moba_in_kernel,
        grid=(B, S // T),
        in_specs=[
            pl.BlockSpec((1, T, D), lambda b, t: (b, t, 0)),
            pl.BlockSpec((1, D), const),
            pl.BlockSpec((3 * TOK_WIDTH, D), const),
            pl.BlockSpec((MEM_WIDTH, D), const),
            pl.BlockSpec((HALF_DIM, T), lambda b, t: (0, t)),
            pl.BlockSpec((HALF_DIM, T), lambda b, t: (0, t)),
            pl.BlockSpec((HEAD_DIM, 1), const),
            pl.BlockSpec((HEAD_DIM, 1), const),
            pl.BlockSpec((HEAD_DIM, 1), const),
            pl.BlockSpec((1, M, MEM_WIDTH), lambda b, t: (b, 0, 0)),
            pl.BlockSpec((1, MEM_WIDTH, M), lambda b, t: (b, 0, 0)),
        ],
        out_specs=[
            pl.BlockSpec((1, TOK_WIDTH, T), lambda b, t: (b, 0, t)),
            pl.BlockSpec((1, T, TOK_WIDTH), lambda b, t: (b, t, 0)),
            pl.BlockSpec((1, T // MOBA_BLOCK, TOK_WIDTH, MOBA_BLOCK), lambda b, t: (b, t, 0, 0)),
            pl.BlockSpec((1, MOBA_HEADS, nb, T), lambda b, t: (b, 0, 0, t)),
            pl.BlockSpec((1, T, MEM_WIDTH), lambda b, t: (b, t, 0)),
        ],
        out_shape=[
            jax.ShapeDtypeStruct((B, TOK_WIDTH, S), BF16),
            jax.ShapeDtypeStruct((B, S, TOK_WIDTH), BF16),
            jax.ShapeDtypeStruct((B, nb, TOK_WIDTH, MOBA_BLOCK), BF16),
            jax.ShapeDtypeStruct((B, MOBA_HEADS, nb, S), F32),
            jax.ShapeDtypeStruct((B, S, MEM_WIDTH), BF16),
        ],
        scratch_shapes=[pltpu.VMEM((nb, TOK_WIDTH), F32)],
        compiler_params=_params(),
        name="moba_in",
    )(x, g, wqkv_t, wm_t, cos_t, sin_t, q_gain, k_gain, mq_gain, mk, mvt)


def _moba_attn_kernel(qt_ref, k_ref, vt_ref, bias_ref, onehot_ref, o_ref, *scratch):
    i = pl.program_id(2)
    TQ = qt_ref.shape[2]
    s_refs = (scratch[0:2], scratch[2:4])
    bm_refs = (scratch[4:6], scratch[6:8])
    group_keys = KV_GROUP * MOBA_BLOCK
    total_groups = k_ref.shape[1] // group_keys
    live_groups = (i + KV_GROUP - 1) // KV_GROUP

    q_pair = qt_ref[0]
    row = lax.broadcasted_iota(jnp.int32, q_pair.shape, 0)
    zero = jnp.zeros_like(q_pair)
    q_heads = (jnp.where(row < HEAD_DIM, q_pair, zero), jnp.where(row >= HEAD_DIM, q_pair, zero))

    zeros_tail = jnp.zeros((MXU_DEPTH - LANES - BIAS_ROWS, TQ), BF16)
    ones_rows = jnp.ones((SUM_ROWS, 1), BF16)

    def produce(g, parity, a, s_ref, bm_ref):
        gc = jnp.minimum(g, total_groups - 1)
        k_g = k_ref[0, pl.ds(pl.multiple_of(gc * group_keys, group_keys), group_keys), :]
        lhs = jnp.concatenate([k_g, onehot_ref[parity]], axis=1)
        penalty = jnp.where(g >= live_groups, NEG, 0.0)
        slab_row = pl.multiple_of((gc // 2) * 2 * KV_GROUP, 2 * KV_GROUP)
        slab = bias_ref[0, a, pl.ds(slab_row, 2 * KV_GROUP), :] + penalty
        slab = jnp.concatenate([slab, jnp.zeros_like(slab)], axis=0).astype(BF16)
        rhs = jnp.concatenate([q_heads[a], slab, zeros_tail], axis=0)
        s = jnp.dot(lhs, rhs, preferred_element_type=F32)
        s_ref[...] = s
        bm_ref[...] = jnp.max(s, axis=0, keepdims=True)

    def pv_operand(v_t):
        return jnp.concatenate([v_t, jnp.broadcast_to(ones_rows, (SUM_ROWS, v_t.shape[1]))], axis=0)

    def consume(g, a, s_ref, bm_ref, state):
        gc = jnp.minimum(g, total_groups - 1)
        j0 = gc * KV_GROUP
        rows = slice(a * HEAD_DIM, (a + 1) * HEAD_DIM)
        v_g = jnp.concatenate([vt_ref[0, j0 + c, rows, :] for c in range(KV_GROUP)], axis=1)
        m, acc = state
        m_new = jnp.maximum(m, bm_ref[...])
        alpha = jnp.exp2(m - m_new)
        p = jnp.exp2(s_ref[...] - m_new).astype(BF16)
        pv = jnp.dot(pv_operand(v_g), p, preferred_element_type=F32)
        return m_new, alpha * acc + pv

    k_own = k_ref[0, pl.ds(pl.multiple_of(i * MOBA_BLOCK, MOBA_BLOCK), MOBA_BLOCK), :]
    v_own = vt_ref[0, i]
    key_idx = lax.broadcasted_iota(jnp.int32, (MOBA_BLOCK, TQ), 0)
    qry_idx = lax.broadcasted_iota(jnp.int32, (MOBA_BLOCK, TQ), 1)
    causal = key_idx <= qry_idx
    s_own = [jnp.where(causal, jnp.dot(k_own, q_heads[a], preferred_element_type=F32), NEG)
             for a in range(2)]
    produce(0, 0, 0, s_refs[0][0], bm_refs[0][0])
    produce(0, 0, 1, s_refs[0][1], bm_refs[0][1])
    init = []
    for a in range(2):
        m = jnp.max(s_own[a], axis=0, keepdims=True)
        p = jnp.exp2(s_own[a] - m).astype(BF16)
        acc = jnp.dot(pv_operand(v_own[a * HEAD_DIM:(a + 1) * HEAD_DIM]), p, preferred_element_type=F32)
        init.append((m, acc))

    def run_steps(groups_per_step, first_group, n_steps, carry):
        def body(step, carry):
            g = first_group + groups_per_step * step
            state = list(carry)
            for sub in range(groups_per_step):
                cur = sub % 2
                nxt = 1 - cur
                for a in range(2):
                    produce(g + sub + 1, nxt, a, s_refs[nxt][a], bm_refs[nxt][a])
                    state[a] = consume(g + sub, a, s_refs[cur][a], bm_refs[cur][a], state[a])
            return tuple(state)
        return lax.fori_loop(0, n_steps, body, carry)

    carry, done = tuple(init), 0
    for size in STEP_GROUPS[:-1]:
        n_steps = (live_groups - done) // size
        carry = run_steps(size, done, n_steps, carry)
        done = done + n_steps * size
    final = run_steps(STEP_GROUPS[-1], done, (live_groups - done + STEP_GROUPS[-1] - 1) // STEP_GROUPS[-1], carry)
    o_t = jnp.concatenate([acc[:HEAD_DIM] / acc[HEAD_DIM:HEAD_DIM + 1] for (_, acc) in final], axis=0)
    o_ref[0] = o_t.T.astype(BF16)


def _block_onehot():
    key_block = jnp.arange(KV_GROUP * MOBA_BLOCK, dtype=jnp.int32) // MOBA_BLOCK
    col = jnp.arange(LANES, dtype=jnp.int32)
    parity = jnp.arange(2, dtype=jnp.int32)
    hit = col[None, None, :] == (parity[:, None, None] * KV_GROUP + key_block[None, :, None])
    return hit.astype(BF16)


def _moba_attn(qt, k, vt, bias):
    B, _, S = qt.shape
    nb = S // MOBA_BLOCK
    assert 2 * KV_GROUP == F32_SUBLANES and nb % F32_SUBLANES == 0, "bias slabs are whole f32 sublane tiles"
    pairs = MOBA_HEADS // 2
    group_keys = KV_GROUP * MOBA_BLOCK
    return pl.pallas_call(
        _moba_attn_kernel,
        grid=(B, pairs, nb),
        in_specs=[
            pl.BlockSpec((1, LANES, MOBA_BLOCK), lambda b, p, i: (b, p, i)),
            pl.BlockSpec((1, S, LANES), lambda b, p, i: (b, 0, p)),
            pl.BlockSpec((1, nb, LANES, MOBA_BLOCK), lambda b, p, i: (b, 0, p, 0)),
            pl.BlockSpec((1, 2, nb, MOBA_BLOCK), lambda b, p, i: (b, p, 0, i)),
            pl.BlockSpec((2, group_keys, LANES), lambda b, p, i: (0, 0, 0)),
        ],
        out_specs=pl.BlockSpec((1, MOBA_BLOCK, LANES), lambda b, p, i: (b, i, p)),
        out_shape=jax.ShapeDtypeStruct((B, S, TOK_WIDTH), BF16),
        scratch_shapes=[pltpu.VMEM((group_keys, MOBA_BLOCK), F32)] * 4 + [pltpu.VMEM((1, MOBA_BLOCK), F32)] * 4,
        compiler_params=pltpu.CompilerParams(
            dimension_semantics=("arbitrary", "arbitrary", "arbitrary"), vmem_limit_bytes=VMEM_LIMIT_BYTES),
        name="moba_attn",
    )(qt, k, vt, bias, _block_onehot())


def _post_kernel(x_ref, tok_ref, mo_ref, wo_ref, g_ref, w1_ref, w2_ref, o_ref):
    mixed = jnp.concatenate([tok_ref[0], mo_ref[0]], axis=-1)
    x1 = x_ref[0] + jnp.dot(mixed, wo_ref[...], preferred_element_type=F32)
    u = _rms_rows(x1, g_ref[...]).astype(BF16)
    h = jnp.dot(u, w1_ref[...], preferred_element_type=F32)
    h = jnp.square(jnp.maximum(h, 0.0)).astype(BF16)
    o_ref[0] = x1 + jnp.dot(h, w2_ref[...], preferred_element_type=F32)


def _post(x, tok, mo, wo, g, w1, w2):
    B, S, D = x.shape
    T = min(TOKEN_TILE, S)
    const = lambda b, t: (0, 0)
    resident = functools.partial(pl.BlockSpec, index_map=const, pipeline_mode=pl.Buffered(1))
    return pl.pallas_call(
        _post_kernel,
        grid=(B, S // T),
        in_specs=[
            pl.BlockSpec((1, T, D), lambda b, t: (b, t, 0)),
            pl.BlockSpec((1, T, TOK_WIDTH), lambda b, t: (b, t, 0)),
            pl.BlockSpec((1, T, MEM_WIDTH), lambda b, t: (b, t, 0)),
            resident((D, D)),
            pl.BlockSpec((1, D), const),
            resident((D, D_FF)),
            resident((D_FF, D)),
        ],
        out_specs=pl.BlockSpec((1, T, D), lambda b, t: (b, t, 0)),
        out_shape=jax.ShapeDtypeStruct((B, S, D), F32),
        compiler_params=_params(),
        name="post",
    )(x, tok, mo, wo, g, w1, w2)


def _rope_tables_t(seq_len):
    pos = jnp.arange(seq_len, dtype=F32)
    inv = ROPE_THETA ** (-jnp.arange(0, HEAD_DIM, 2, dtype=F32) / HEAD_DIM)
    ang = pos[:, None] * inv[None, :]
    return jnp.cos(ang).T, jnp.sin(ang).T


def _block_diag(w_group):
    G, C, _ = w_group.shape
    eye = jnp.eye(G, dtype=w_group.dtype)
    return (eye[:, None, :, None] * w_group[:, :, None, :]).reshape(G * C, G * C)


def kernel(x, mem, g_mix, g_mem, g_mlp, w_in_pool, w_pool_group, pool_scale, w_in_moba, moba_q_gain,
           moba_k_gain, w_mem_kv, mem_q_gain, mem_k_gain, w_out, w_ff1, w_ff2):
    depth = g_mix.shape[0]
    S = x.shape[1]
    cos_t, sin_t = _rope_tables_t(S)
    col = lambda v: v.reshape(-1, 1)
    row = lambda v: v.reshape(1, -1)
    for i in range(depth):
        j = i // 2
        mk, mvt = _mem_kv(mem, row(g_mem[i]), w_mem_kv[i].T.astype(BF16), col(mem_k_gain[i]))
        if i % 2 == 0:
            w = w_in_pool[j]
            tok, mo = _pool_in(
                x, row(g_mix[i]), w[:, :TOK_WIDTH].astype(BF16), w[:, TOK_WIDTH:].T.astype(BF16),
                _block_diag(w_pool_group[j]).astype(BF16), row(pool_scale[j]), col(mem_q_gain[i]), mk, mvt)
        else:
            w = w_in_moba[j]
            qt, k, vt, bias, mo = _moba_in(
                x, row(g_mix[i]), w[:, :3 * TOK_WIDTH].T.astype(BF16), w[:, 3 * TOK_WIDTH:].T.astype(BF16),
                cos_t, sin_t, col(moba_q_gain[j]), col(moba_k_gain[j]), col(mem_q_gain[i]), mk, mvt)
            tok = _moba_attn(qt, k, vt, bias)
        x = _post(x, tok, mo, w_out[i].astype(BF16), row(g_mlp[i]), w_ff1[i].astype(BF16), w_ff2[i].astype(BF16))
    return x
```

```python
import functools

import jax
import jax.numpy as jnp
from jax import lax
from jax.experimental import pallas as pl
from jax.experimental.pallas import tpu as pltpu

D_MODEL = 1024
HEAD_DIM = 64
HALF_DIM = HEAD_DIM // 2
MEM_HEADS = 4
MEM_WIDTH = MEM_HEADS * HEAD_DIM
TOK_WIDTH = D_MODEL - MEM_WIDTH
MOBA_HEADS = TOK_WIDTH // HEAD_DIM
MOBA_BLOCK = 256
MOBA_TOPK = 3
POOL_WINDOWS = (2, 4, 8, 16)
POOL_GROUP_WIDTH = TOK_WIDTH // len(POOL_WINDOWS)
POOL_HALO = 16
D_FF = 4 * D_MODEL
ROPE_THETA = 10000.0
EPS = 1e-6
NEG = -1e30
SM_SCALE = HEAD_DIM ** -0.5
LOG2_E = 1.4426950408889634

LANES = 128
F32_SUBLANES = 8
BIAS_ROWS = 16
SUM_ROWS = 16
MXU_DEPTH = 256
VMEM_LIMIT_BYTES = 56 * 1024 * 1024

TOKEN_TILE = 512
VALUE_CHUNKS = 3
KV_GROUP = 4
STEP_GROUPS = (8, 4, 2)

F32 = jnp.float32
BF16 = jnp.bfloat16
NT_DIMS = (((1,), (1,)), ((), ()))


def _params():
    return pltpu.CompilerParams(
        dimension_semantics=("arbitrary", "arbitrary"), vmem_limit_bytes=VMEM_LIMIT_BYTES)


def _rms_rows(x, g):
    ms = jnp.mean(x * x, axis=-1, keepdims=True)
    return x * lax.rsqrt(ms + EPS) * g


def _head_rms_t(h, gain_col):
    ms = jnp.mean(h * h, axis=0, keepdims=True)
    return h * lax.rsqrt(ms + EPS) * gain_col


def _rope_t(h, cos, sin):
    h1, h2 = h[:HALF_DIM], h[HALF_DIM:]
    return jnp.concatenate([h1 * cos - h2 * sin, h2 * cos + h1 * sin], axis=0)


def _split_bf16(x):
    hi = x.astype(BF16)
    return hi, (x - hi.astype(F32)).astype(BF16)


def _pair_operand(q_bf, slot):
    z = jnp.zeros_like(q_bf)
    return jnp.concatenate([q_bf, z] if slot == 0 else [z, q_bf], axis=0)


def _mem_attention_t(qm_t, mq_gain, mk_ref, mvt_ref):
    scores = []
    for h in range(MEM_HEADS):
        pair, slot = divmod(h, 2)
        q = _head_rms_t(qm_t[h * HEAD_DIM:(h + 1) * HEAD_DIM], mq_gain)
        q2 = _pair_operand((q * (SM_SCALE * LOG2_E)).astype(BF16), slot)
        k_pair = mk_ref[0, :, pair * LANES:(pair + 1) * LANES]
        scores.append(jnp.dot(k_pair, q2, preferred_element_type=F32))
    probs = [jnp.exp2(s - jnp.max(s, axis=0, keepdims=True)).astype(BF16) for s in scores]
    outs = []
    for h in range(MEM_HEADS):
        v_t = mvt_ref[0, h * HEAD_DIM:(h + 1) * HEAD_DIM, :]
        ones = jnp.ones((SUM_ROWS, v_t.shape[1]), BF16)
        r = jnp.dot(jnp.concatenate([v_t, ones], axis=0), probs[h], preferred_element_type=F32)
        outs.append(r[:HEAD_DIM] / r[HEAD_DIM:HEAD_DIM + 1])
    return jnp.concatenate(outs, axis=0).T


def _mem_kv_kernel(mem_ref, g_ref, wkv_t_ref, kg_ref, mk_ref, mvt_ref):
    mem_n = _rms_rows(mem_ref[0], g_ref[...]).astype(BF16)
    kv_t = lax.dot_general(wkv_t_ref[...], mem_n, NT_DIMS, preferred_element_type=F32)
    k_heads = [_head_rms_t(kv_t[h * HEAD_DIM:(h + 1) * HEAD_DIM], kg_ref[...]) for h in range(MEM_HEADS)]
    mk_ref[0] = jnp.concatenate(k_heads, axis=0).T.astype(BF16)
    mvt_ref[0] = kv_t[MEM_WIDTH:].astype(BF16)


def _mem_kv(mem, g, wkv_t, k_gain):
    B, M, D = mem.shape
    return pl.pallas_call(
        _mem_kv_kernel,
        grid=(B, 1),
        in_specs=[
            pl.BlockSpec((1, M, D), lambda b, _: (b, 0, 0)),
            pl.BlockSpec((1, D), lambda b, _: (0, 0)),
            pl.BlockSpec((2 * MEM_WIDTH, D), lambda b, _: (0, 0)),
            pl.BlockSpec((HEAD_DIM, 1), lambda b, _: (0, 0)),
        ],
        out_specs=[
            pl.BlockSpec((1, M, MEM_WIDTH), lambda b, _: (b, 0, 0)),
            pl.BlockSpec((1, MEM_WIDTH, M), lambda b, _: (b, 0, 0)),
        ],
        out_shape=[
            jax.ShapeDtypeStruct((B, M, MEM_WIDTH), BF16),
            jax.ShapeDtypeStruct((B, MEM_WIDTH, M), BF16),
        ],
        compiler_params=_params(),
        name="mem_kv",
    )(mem, g, wkv_t, k_gain)


def _pool_in_kernel(x_ref, g_ref, w_tok_ref, wm_t_ref, wg_ref, scale_ref, mqg_ref, mk_ref, mvt_ref,
                    tok_ref, mo_ref, halo_ref):
    t = pl.program_id(1)
    T = x_ref.shape[1]

    @pl.when(t == 0)
    def _():
        halo_ref[...] = jnp.zeros_like(halo_ref)

    u = _rms_rows(x_ref[0], g_ref[...]).astype(BF16)
    h = jnp.dot(u, w_tok_ref[...], preferred_element_type=F32)

    ext = jnp.concatenate([halo_ref[...], h], axis=0)
    halo_ref[...] = h[T - POOL_HALO:]
    s2 = ext + pltpu.roll(ext, 1, 0)
    s4 = s2 + pltpu.roll(s2, 2, 0)
    s8 = s4 + pltpu.roll(s4, 4, 0)
    s16 = s8 + pltpu.roll(s8, 8, 0)
    lane = lax.broadcasted_iota(jnp.int32, (T, TOK_WIDTH), 1)
    pos = t * T + lax.broadcasted_iota(jnp.int32, (T, TOK_WIDTH), 0)
    sums = (s2, s4, s8, s16)
    win_sum = sums[-1][POOL_HALO:]
    window = jnp.full((T, TOK_WIDTH), POOL_WINDOWS[-1], jnp.int32)
    for gi in range(len(POOL_WINDOWS) - 2, -1, -1):
        in_group = lane < (gi + 1) * POOL_GROUP_WIDTH
        win_sum = jnp.where(in_group, sums[gi][POOL_HALO:], win_sum)
        window = jnp.where(in_group, POOL_WINDOWS[gi], window)
    count = jnp.minimum(pos + 1, window).astype(F32)
    d = (win_sum / count - h).astype(BF16)
    y = jnp.dot(d, wg_ref[...], preferred_element_type=F32) * scale_ref[...]
    tok_ref[0] = y.astype(BF16)

    qm_t = lax.dot_general(wm_t_ref[...], u, NT_DIMS, preferred_element_type=F32)
    mo_ref[0] = _mem_attention_t(qm_t, mqg_ref[...], mk_ref, mvt_ref).astype(BF16)


def _pool_in(x, g, w_tok, wm_t, wg, scale, mq_gain, mk, mvt):
    B, S, D = x.shape
    M = mk.shape[1]
    T = min(TOKEN_TILE, S)
    const = lambda b, t: (0, 0)
    return pl.pallas_call(
        _pool_in_kernel,
        grid=(B, S // T),
        in_specs=[
            pl.BlockSpec((1, T, D), lambda b, t: (b, t, 0)),
            pl.BlockSpec((1, D), const),
            pl.BlockSpec((D, TOK_WIDTH), const),
            pl.BlockSpec((MEM_WIDTH, D), const),
            pl.BlockSpec((TOK_WIDTH, TOK_WIDTH), const),
            pl.BlockSpec((1, TOK_WIDTH), const),
            pl.BlockSpec((HEAD_DIM, 1), const),
            pl.BlockSpec((1, M, MEM_WIDTH), lambda b, t: (b, 0, 0)),
            pl.BlockSpec((1, MEM_WIDTH, M), lambda b, t: (b, 0, 0)),
        ],
        out_specs=[
            pl.BlockSpec((1, T, TOK_WIDTH), lambda b, t: (b, t, 0)),
            pl.BlockSpec((1, T, MEM_WIDTH), lambda b, t: (b, t, 0)),
        ],
        out_shape=[
            jax.ShapeDtypeStruct((B, S, TOK_WIDTH), BF16),
            jax.ShapeDtypeStruct((B, S, MEM_WIDTH), BF16),
        ],
        scratch_shapes=[pltpu.VMEM((POOL_HALO, TOK_WIDTH), F32)],
        compiler_params=_params(),
        name="pool_in",
    )(x, g, w_tok, wm_t, wg, scale, mq_gain, mk, mvt)


def _selection_bias(gate, t, T):
    nb, width = gate.shape
    blk = lax.broadcasted_iota(jnp.int32, (nb, width), 0)
    pos = t * T + lax.rem(lax.broadcasted_iota(jnp.int32, (nb, width), 1), T)
    own = lax.shift_right_logical(pos, MOBA_BLOCK.bit_length() - 1)
    g = jnp.where(blk < own, gate, -jnp.inf)
    bias = jnp.full((nb, width), NEG, F32)
    for _ in range(MOBA_TOPK):
        mx = jnp.max(g, axis=0, keepdims=True)
        idx = jnp.min(jnp.where(g == mx, blk, nb), axis=0, keepdims=True)
        idx = jnp.where(mx > -jnp.inf, idx, -1)
        pick = blk == idx
        bias = jnp.where(pick, 0.0, bias)
        g = jnp.where(pick, -jnp.inf, g)
    return bias


def _moba_in_kernel(x_ref, g_ref, wqkv_t_ref, wm_t_ref, cos_ref, sin_ref, qg_ref, kg_ref, mqg_ref,
                    mk_ref, mvt_ref, qt_ref, k_ref, vt_ref, bias_ref, mo_ref, kmean_ref):
    t = pl.program_id(1)
    T = x_ref.shape[1]
    blocks_per_tile = T // MOBA_BLOCK

    @pl.when(t == 0)
    def _():
        kmean_ref[...] = jnp.zeros_like(kmean_ref)

    u = _rms_rows(x_ref[0], g_ref[...]).astype(BF16)
    cos, sin = cos_ref[...], sin_ref[...]

    def proj_t(lo, hi):
        return lax.dot_general(wqkv_t_ref[lo:hi, :], u, NT_DIMS, preferred_element_type=F32)

    k_t = proj_t(TOK_WIDTH, 2 * TOK_WIDTH)
    q_t = proj_t(0, TOK_WIDTH)

    k_heads =[_rope_t(_head_rms_t(k_t[h * HEAD_DIM:(h + 1) * HEAD_DIM], kg_ref[...]), cos, sin)
               for h in range(MOBA_HEADS)]
    k_nat = jnp.concatenate(k_heads, axis=0).T
    k_ref[0] = k_nat.astype(BF16)
    for c in range(blocks_per_tile):
        kmean_ref[pl.ds(t * blocks_per_tile + c, 1), :] = jnp.mean(
            k_nat[c * MOBA_BLOCK:(c + 1) * MOBA_BLOCK], axis=0, keepdims=True)

    km_hi, km_lo = _split_bf16(kmean_ref[...])
    heads_per_chunk = MOBA_HEADS // VALUE_CHUNKS
    chunk_rows = TOK_WIDTH // VALUE_CHUNKS
    for chunk in range(VALUE_CHUNKS):
        lo = 2 * TOK_WIDTH + chunk * chunk_rows
        v_t = proj_t(lo, lo + chunk_rows)
        for c in range(blocks_per_tile):
            vt_ref[0, c, chunk * chunk_rows:(chunk + 1) * chunk_rows, :] = (
                v_t[:, c * MOBA_BLOCK:(c + 1) * MOBA_BLOCK].astype(BF16))
        heads = range(chunk * heads_per_chunk, (chunk + 1) * heads_per_chunk)
        gates = []
        for h in heads:
            rows = slice(h * HEAD_DIM, (h + 1) * HEAD_DIM)
            pair, slot = divmod(h, 2)
            lanes = slice(pair * LANES, (pair + 1) * LANES)
            q = _rope_t(_head_rms_t(q_t[rows], qg_ref[...]), cos, sin)
            qt_ref[0, rows, :] = (q * (SM_SCALE * LOG2_E)).astype(BF16)
            q_hi, q_lo = _split_bf16(q)
            q_hi2, q_lo2 = _pair_operand(q_hi, slot), _pair_operand(q_lo, slot)
            gates.append(jnp.dot(jnp.concatenate([km_hi[:, lanes], km_lo[:, lanes]], axis=1),
                                 jnp.concatenate([q_hi2, q_hi2], axis=0), preferred_element_type=F32)
                         + jnp.dot(km_hi[:, lanes], q_lo2, preferred_element_type=F32))
        bias = _selection_bias(jnp.concatenate(gates, axis=1), t, T)
        for n, h in enumerate(heads):
            bias_ref[0, h] = bias[:, n * T:(n + 1) * T]

    qm_t = lax.dot_general(wm_t_ref[...], u, NT_DIMS, preferred_element_type=F32)
    mo_ref[0] = _mem_attention_t(qm_t, mqg_ref[...], mk_ref, mvt_ref).astype(BF16)


def _moba_in(x, g, wqkv_t, wm_t, cos_t, sin_t, q_gain, k_gain, mq_gain, mk, mvt):
    B, S, D = x.shape
    M = mk.shape[1]
    T = min(TOKEN_TILE, S)
    nb = S // MOBA_BLOCK
    const = lambda b, t: (0, 0)
    return pl.pallas_call(
        _moba_in_kernel,
        grid=(B, S // T),
        in_specs=[
            pl.BlockSpec((1, T, D), lambda b, t: (b, t, 0)),
            pl.BlockSpec((1, D), const),
            pl.BlockSpec((3 * TOK_WIDTH, D), const),
            pl.BlockSpec((MEM_WIDTH, D), const),
            pl.BlockSpec((HALF_DIM, T), lambda b, t: (0, t)),
            pl.BlockSpec((HALF_DIM, T), lambda b, t: (0, t)),
            pl.BlockSpec((HEAD_DIM, 1), const),
            pl.BlockSpec((HEAD_DIM, 1), const),
            pl.BlockSpec((HEAD_DIM, 1), const),
            pl.BlockSpec((1, M, MEM_WIDTH), lambda b, t: (b, 0, 0)),
            pl.BlockSpec((1, MEM_WIDTH, M), lambda b, t: (b, 0, 0)),
        ],
        out_specs=[
            pl.BlockSpec((1, TOK_WIDTH, T), lambda b, t: (b, 0, t)),
            pl.BlockSpec((1, T, TOK_WIDTH), lambda b, t: (b, t, 0)),
            pl.BlockSpec((1, T // MOBA_BLOCK, TOK_WIDTH, MOBA_BLOCK), lambda b, t: (b, t, 0, 0)),
            pl.BlockSpec((1, MOBA_HEADS, nb, T), lambda b, t: (b, 0, 0, t)),
            pl.BlockSpec((1, T, MEM_WIDTH), lambda b, t: (b, t, 0)),
        ],
        out_shape=[
            jax.ShapeDtypeStruct((B, TOK_WIDTH, S), BF16),
            jax.ShapeDtypeStruct((B, S, TOK_WIDTH), BF16),
            jax.ShapeDtypeStruct((B, nb, TOK_WIDTH, MOBA_BLOCK), BF16),
            jax.ShapeDtypeStruct((B, MOBA_HEADS, nb, S), F32),
            jax.ShapeDtypeStruct((B, S, MEM_WIDTH), BF16),
        ],
        scratch_shapes=[pltpu.VMEM((nb, TOK_WIDTH), F32)],
        compiler_params=_params(),
        name="moba_in",
    )(x, g, wqkv_t, wm_t, cos_t, sin_t, q_gain, k_gain, mq_gain, mk, mvt)


def _moba_attn_kernel(qt_ref, k_ref, vt_ref, bias_ref, onehot_ref, o_ref, *scratch):
    i = pl.program_id(2)
    TQ = qt_ref.shape[2]
    s_refs = (scratch[0:2], scratch[2:4])
    bm_refs = (scratch[4:6], scratch[6:8])
    group_keys = KV_GROUP * MOBA_BLOCK
    total_groups = k_ref.shape[1] // group_keys
    live_groups = (i + KV_GROUP - 1) // KV_GROUP

    q_pair = qt_ref[0]
    row = lax.broadcasted_iota(jnp.int32, q_pair.shape, 0)
    zero = jnp.zeros_like(q_pair)
    q_heads = (jnp.where(row < HEAD_DIM, q_pair, zero), jnp.where(row >= HEAD_DIM, q_pair, zero))

    zeros_tail = jnp.zeros((MXU_DEPTH - LANES - BIAS_ROWS, TQ), BF16)
    ones_rows = jnp.ones((SUM_ROWS, 1), BF16)

    def produce(g, parity, a, s_ref, bm_ref):
        gc = jnp.minimum(g, total_groups - 1)
        k_g = k_ref[0, pl.ds(pl.multiple_of(gc * group_keys, group_keys), group_keys), :]
        lhs = jnp.concatenate([k_g, onehot_ref[parity]], axis=1)
        penalty = jnp.where(g >= live_groups, NEG, 0.0)
        slab_row = pl.multiple_of((gc // 2) * 2 * KV_GROUP, 2 * KV_GROUP)
        slab = bias_ref[0, a, pl.ds(slab_row, 2 * KV_GROUP), :] + penalty
        slab = jnp.concatenate([slab, jnp.zeros_like(slab)], axis=0).astype(BF16)
        rhs = jnp.concatenate([q_heads[a], slab, zeros_tail], axis=0)
        s = jnp.dot(lhs, rhs, preferred_element_type=F32)
        s_ref[...] = s
        bm_ref[...] = jnp.max(s, axis=0, keepdims=True)

    def pv_operand(v_t):
        return jnp.concatenate([v_t, jnp.broadcast_to(ones_rows, (SUM_ROWS, v_t.shape[1]))], axis=0)

    def consume(g, a, s_ref, bm_ref, state):
        gc = jnp.minimum(g, total_groups - 1)
        j0 = gc * KV_GROUP
        rows = slice(a * HEAD_DIM, (a + 1) * HEAD_DIM)
        v_g = jnp.concatenate([vt_ref[0, j0 + c, rows, :] for c in range(KV_GROUP)], axis=1)
        m, acc = state
        m_new = jnp.maximum(m, bm_ref[...])
        alpha = jnp.exp2(m - m_new)
        p = jnp.exp2(s_ref[...] - m_new).astype(BF16)
        pv = jnp.dot(pv_operand(v_g), p, preferred_element_type=F32)
        return m_new, alpha * acc + pv

    k_own = k_ref[0, pl.ds(pl.multiple_of(i * MOBA_BLOCK, MOBA_BLOCK), MOBA_BLOCK), :]
    v_own = vt_ref[0, i]
    key_idx = lax.broadcasted_iota(jnp.int32, (MOBA_BLOCK, TQ), 0)
    qry_idx = lax.broadcasted_iota(jnp.int32, (MOBA_BLOCK, TQ), 1)
    causal = key_idx <= qry_idx
    s_own = [jnp.where(causal, jnp.dot(k_own, q_heads[a], preferred_element_type=F32), NEG)
             for a in range(2)]
    produce(0, 0, 0, s_refs[0][0], bm_refs[0][0])
    produce(0, 0, 1, s_refs[0][1], bm_refs[0][1])
    init = []
    for a in range(2):
        m = jnp.max(s_own[a], axis=0, keepdims=True)
        p = jnp.exp2(s_own[a] - m).astype(BF16)
        acc = jnp.dot(pv_operand(v_own[a * HEAD_DIM:(a + 1) * HEAD_DIM]), p, preferred_element_type=F32)
        init.append((m, acc))

    def run_steps(groups_per_step, first_group, n_steps, carry):
        def body(step, carry):
            g = first_group + groups_per_step * step
            state = list(carry)
            for sub in range(groups_per_step):
                cur = sub % 2
                nxt = 1 - cur
                for a in range(2):
                    produce(g + sub + 1, nxt, a, s_refs[nxt][a], bm_refs[nxt][a])
                    state[a] = consume(g + sub, a, s_refs[cur][a], bm_refs[cur][a], state[a])
            return tuple(state)
        return lax.fori_loop(0, n_steps, body, carry)

    carry, done = tuple(init), 0
    for size in STEP_GROUPS:
        n_steps = (live_groups - done) // size
        carry = run_steps(size, done, n_steps, carry)
        done = done + n_steps * size

    def last_group(_, carry):
        return tuple(consume(done, a, s_refs[0][a], bm_refs[0][a], carry[a]) for a in range(2))

    final = lax.fori_loop(0, live_groups - done, last_group, carry)
    o_t = jnp.concatenate([acc[:HEAD_DIM] / acc[HEAD_DIM:HEAD_DIM + 1] for (_, acc) in final], axis=0)
    o_ref[0] = o_t.T.astype(BF16)


def _block_onehot():
    key_block = jnp.arange(KV_GROUP * MOBA_BLOCK, dtype=jnp.int32) // MOBA_BLOCK
    col = jnp.arange(LANES, dtype=jnp.int32)
    parity = jnp.arange(2, dtype=jnp.int32)
    hit = col[None, None, :] == (parity[:, None, None] * KV_GROUP + key_block[None, :, None])
    return hit.astype(BF16)


def _moba_attn(qt, k, vt, bias):
    B, _, S = qt.shape
    nb = S // MOBA_BLOCK
    assert 2 * KV_GROUP == F32_SUBLANES and nb % F32_SUBLANES == 0, "bias slabs are whole f32 sublane tiles"
    assert STEP_GROUPS[-1] == 2 and all(size % 2 == 0 for size in STEP_GROUPS), "score buffers alternate by group parity"
    pairs = MOBA_HEADS // 2
    group_keys = KV_GROUP * MOBA_BLOCK
    return pl.pallas_call(
        _moba_attn_kernel,
        grid=(B, pairs, nb),
        in_specs=[
            pl.BlockSpec((1, LANES, MOBA_BLOCK), lambda b, p, i: (b, p, i)),
            pl.BlockSpec((1, S, LANES), lambda b, p, i: (b, 0, p)),
            pl.BlockSpec((1, nb, LANES, MOBA_BLOCK), lambda b, p, i: (b, 0, p, 0)),
            pl.BlockSpec((1, 2, nb, MOBA_BLOCK), lambda b, p, i: (b, p, 0, i)),
            pl.BlockSpec((2, group_keys, LANES), lambda b, p, i: (0, 0, 0)),
        ],
        out_specs=pl.BlockSpec((1, MOBA_BLOCK, LANES), lambda b, p, i: (b, i, p)),
        out_shape=jax.ShapeDtypeStruct((B, S, TOK_WIDTH), BF16),
        scratch_shapes=[pltpu.VMEM((group_keys, MOBA_BLOCK), F32)] * 4 + [pltpu.VMEM((1, MOBA_BLOCK), F32)] * 4,
        compiler_params=pltpu.CompilerParams(
            dimension_semantics=("arbitrary", "arbitrary", "arbitrary"), vmem_limit_bytes=VMEM_LIMIT_BYTES),
        name="moba_attn",
    )(qt, k, vt, bias, _block_onehot())


def _post_kernel(x_ref, tok_ref, mo_ref, wo_ref, g_ref, w1_ref, w2_ref, o_ref):
    mixed = jnp.concatenate([tok_ref[0], mo_ref[0]], axis=-1)
    x1 = x_ref[0] + jnp.dot(mixed, wo_ref[...], preferred_element_type=F32)
    u = _rms_rows(x1, g_ref[...]).astype(BF16)
    h = jnp.dot(u, w1_ref[...], preferred_element_type=F32)
    h = jnp.square(jnp.maximum(h, 0.0)).astype(BF16)
    o_ref[0] = x1 + jnp.dot(h, w2_ref[...], preferred_element_type=F32)


def _post(x, tok, mo, wo, g, w1, w2):
    B, S, D = x.shape
    T = min(TOKEN_TILE, S)
    const = lambda b, t: (0, 0)
    resident = functools.partial(pl.BlockSpec, index_map=const, pipeline_mode=pl.Buffered(1))
    return pl.pallas_call(
        _post_kernel,
        grid=(B, S // T),
        in_specs=[
            pl.BlockSpec((1, T, D), lambda b, t: (b, t, 0)),
            pl.BlockSpec((1, T, TOK_WIDTH), lambda b, t: (b, t, 0)),
            pl.BlockSpec((1, T, MEM_WIDTH), lambda b, t: (b, t, 0)),
            resident((D, D)),
            pl.BlockSpec((1, D), const),
            resident((D, D_FF)),
            resident((D_FF, D)),
        ],
        out_specs=pl.BlockSpec((1, T, D), lambda b, t: (b, t, 0)),
        out_shape=jax.ShapeDtypeStruct((B, S, D), F32),
        compiler_params=_params(),
        name="post",
    )(x, tok, mo, wo, g, w1, w2)


def _rope_tables_t(seq_len):
    pos = jnp.arange(seq_len, dtype=F32)
    inv = ROPE_THETA ** (-jnp.arange(0, HEAD_DIM, 2, dtype=F32) / HEAD_DIM)
    ang = pos[:, None] * inv[None, :]
    return jnp.cos(ang).T, jnp.sin(ang).T


def _block_diag(w_group):
    G, C, _ = w_group.shape
    eye = jnp.eye(G, dtype=w_group.dtype)
    return (eye[:, None, :, None] * w_group[:, :, None, :]).reshape(G * C, G * C)


def kernel(x, mem, g_mix, g_mem, g_mlp, w_in_pool, w_pool_group, pool_scale, w_in_moba, moba_q_gain,
           moba_k_gain, w_mem_kv, mem_q_gain, mem_k_gain, w_out, w_ff1, w_ff2):
    depth = g_mix.shape[0]
    S = x.shape[1]
    cos_t, sin_t = _rope_tables_t(S)
    col = lambda v: v.reshape(-1, 1)
    row = lambda v: v.reshape(1, -1)
    for i in range(depth):
        j = i // 2
        mk, mvt = _mem_kv(mem, row(g_mem[i]), w_mem_kv[i].T.astype(BF16), col(mem_k_gain[i]))
        if i % 2 == 0:
            w = w_in_pool[j]
            tok, mo = _pool_in(
                x, row(g_mix[i]), w[:, :TOK_WIDTH].astype(BF16), w[:, TOK_WIDTH:].T.astype(BF16),
                _block_diag(w_pool_group[j]).astype(BF16), row(pool_scale[j]), col(mem_q_gain[i]), mk, mvt)
        else:
            w = w_in_moba[j]
            qt, k, vt, bias, mo = _moba_in(
                x, row(g_mix[i]), w[:, :3 * TOK_WIDTH].T.astype(BF16), w[:, 3 * TOK_WIDTH:].T.astype(BF16),
                cos_t, sin_t, col(moba_q_gain[j]), col(moba_k_gain[j]), col(mem_q_gain[i]), mk, mvt)
            tok = _moba_attn(qt, k, vt, bias)
        x = _post(x, tok, mo, w_out[i].astype(BF16), row(g_mlp[i]), w_ff1[i].astype(BF16), w_ff2[i].astype(BF16))
    return x
```

```python
import functools

import jax
import jax.numpy as jnp
from jax import lax
from jax.experimental import pallas as pl
from jax.experimental.pallas import tpu as pltpu

D_MODEL = 1024
HEAD_DIM = 64
HALF_DIM = HEAD_DIM // 2
MEM_HEADS = 4
MEM_WIDTH = MEM_HEADS * HEAD_DIM
TOK_WIDTH = D_MODEL - MEM_WIDTH
MOBA_HEADS = TOK_WIDTH // HEAD_DIM
MOBA_BLOCK = 256
MOBA_TOPK = 3
POOL_WINDOWS = (2, 4, 8, 16)
POOL_GROUP_WIDTH = TOK_WIDTH // len(POOL_WINDOWS)
POOL_HALO = 16
D_FF = 4 * D_MODEL
ROPE_THETA = 10000.0
EPS = 1e-6
NEG = -1e30
SM_SCALE = HEAD_DIM ** -0.5
LOG2_E = 1.4426950408889634

LANES = 128
F32_SUBLANES = 8
BIAS_ROWS = 16
SUM_ROWS = 16
MXU_DEPTH = 256
VMEM_LIMIT_BYTES = 56 * 1024 * 1024

TOKEN_TILE = 512
VALUE_CHUNKS = 3
KV_GROUP = 4
STEP_GROUPS = (8, 4, 2)

F32 = jnp.float32
BF16 = jnp.bfloat16
NT_DIMS = (((1,), (1,)), ((), ()))


def _params():
    return pltpu.CompilerParams(
        dimension_semantics=("arbitrary", "arbitrary"), vmem_limit_bytes=VMEM_LIMIT_BYTES)


def _rms_rows(x, g):
    ms = jnp.mean(x * x, axis=-1, keepdims=True)
    return x * lax.rsqrt(ms + EPS) * g


def _head_rms_t(h, gain_col):
    ms = jnp.mean(h * h, axis=0, keepdims=True)
    return h * lax.rsqrt(ms + EPS) * gain_col


def _rope_t(h, cos, sin):
    h1, h2 = h[:HALF_DIM], h[HALF_DIM:]
    return jnp.concatenate([h1 * cos - h2 * sin, h2 * cos + h1 * sin], axis=0)


def _split_bf16(x):
    hi = x.astype(BF16)
    return hi, (x - hi.astype(F32)).astype(BF16)


def _pair_operand(q_bf, slot):
    z = jnp.zeros_like(q_bf)
    return jnp.concatenate([q_bf, z] if slot == 0 else [z, q_bf], axis=0)


def _mem_attention_t(qm_t, mq_gain, mk_ref, mvt_ref):
    scores = []
    for h in range(MEM_HEADS):
        pair, slot = divmod(h, 2)
        q = _head_rms_t(qm_t[h * HEAD_DIM:(h + 1) * HEAD_DIM], mq_gain)
        q2 = _pair_operand((q * (SM_SCALE * LOG2_E)).astype(BF16), slot)
        k_pair = mk_ref[0, :, pair * LANES:(pair + 1) * LANES]
        scores.append(jnp.dot(k_pair, q2, preferred_element_type=F32))
    probs = [jnp.exp2(s - jnp.max(s, axis=0, keepdims=True)).astype(BF16) for s in scores]
    outs = []
    for h in range(MEM_HEADS):
        v_t = mvt_ref[0, h * HEAD_DIM:(h + 1) * HEAD_DIM, :]
        ones = jnp.ones((SUM_ROWS, v_t.shape[1]), BF16)
        r = jnp.dot(jnp.concatenate([v_t, ones], axis=0), probs[h], preferred_element_type=F32)
        outs.append(r[:HEAD_DIM] / r[HEAD_DIM:HEAD_DIM + 1])
    return jnp.concatenate(outs, axis=0).T


def _mem_kv_kernel(mem_ref, g_ref, wkv_t_ref, kg_ref, mk_ref, mvt_ref):
    mem_n = _rms_rows(mem_ref[0], g_ref[...]).astype(BF16)
    kv_t = lax.dot_general(wkv_t_ref[...], mem_n, NT_DIMS, preferred_element_type=F32)
    k_heads = [_head_rms_t(kv_t[h * HEAD_DIM:(h + 1) * HEAD_DIM], kg_ref[...]) for h in range(MEM_HEADS)]
    mk_ref[0] = jnp.concatenate(k_heads, axis=0).T.astype(BF16)
    mvt_ref[0] = kv_t[MEM_WIDTH:].astype(BF16)


def _mem_kv(mem, g, wkv_t, k_gain):
    B, M, D = mem.shape
    return pl.pallas_call(
        _mem_kv_kernel,
        grid=(B, 1),
        in_specs=[
            pl.BlockSpec((1, M, D), lambda b, _: (b, 0, 0)),
            pl.BlockSpec((1, D), lambda b, _: (0, 0)),
            pl.BlockSpec((2 * MEM_WIDTH, D), lambda b, _: (0, 0)),
            pl.BlockSpec((HEAD_DIM, 1), lambda b, _: (0, 0)),
        ],
        out_specs=[
            pl.BlockSpec((1, M, MEM_WIDTH), lambda b, _: (b, 0, 0)),
            pl.BlockSpec((1, MEM_WIDTH, M), lambda b, _: (b, 0, 0)),
        ],
        out_shape=[
            jax.ShapeDtypeStruct((B, M, MEM_WIDTH), BF16),
            jax.ShapeDtypeStruct((B, MEM_WIDTH, M), BF16),
        ],
        compiler_params=_params(),
        name="mem_kv",
    )(mem, g, wkv_t, k_gain)


def _pool_in_kernel(x_ref, g_ref, w_tok_ref, wm_t_ref, wg_ref, scale_ref, mqg_ref, mk_ref, mvt_ref,
                    tok_ref, mo_ref, halo_ref):
    t = pl.program_id(1)
    T = x_ref.shape[1]

    @pl.when(t == 0)
    def _():
        halo_ref[...] = jnp.zeros_like(halo_ref)

    u = _rms_rows(x_ref[0], g_ref[...]).astype(BF16)
    h = jnp.dot(u, w_tok_ref[...], preferred_element_type=F32)

    ext = jnp.concatenate([halo_ref[...], h], axis=0)
    halo_ref[...] = h[T - POOL_HALO:]
    s2 = ext + pltpu.roll(ext, 1, 0)
    s4 = s2 + pltpu.roll(s2, 2, 0)
    s8 = s4 + pltpu.roll(s4, 4, 0)
    s16 = s8 + pltpu.roll(s8, 8, 0)
    lane = lax.broadcasted_iota(jnp.int32, (T, TOK_WIDTH), 1)
    pos = t * T + lax.broadcasted_iota(jnp.int32, (T, TOK_WIDTH), 0)
    sums = (s2, s4, s8, s16)
    win_sum = sums[-1][POOL_HALO:]
    window = jnp.full((T, TOK_WIDTH), POOL_WINDOWS[-1], jnp.int32)
    for gi in range(len(POOL_WINDOWS) - 2, -1, -1):
        in_group = lane < (gi + 1) * POOL_GROUP_WIDTH
        win_sum = jnp.where(in_group, sums[gi][POOL_HALO:], win_sum)
        window = jnp.where(in_group, POOL_WINDOWS[gi], window)
    count = jnp.minimum(pos + 1, window).astype(F32)
    d = (win_sum / count - h).astype(BF16)
    y = jnp.dot(d, wg_ref[...], preferred_element_type=F32) * scale_ref[...]
    tok_ref[0] = y.astype(BF16)

    qm_t = lax.dot_general(wm_t_ref[...], u, NT_DIMS, preferred_element_type=F32)
    mo_ref[0] = _mem_attention_t(qm_t, mqg_ref[...], mk_ref, mvt_ref).astype(BF16)


def _pool_in(x, g, w_tok, wm_t, wg, scale, mq_gain, mk, mvt):
    B, S, D = x.shape
    M = mk.shape[1]
    T = min(TOKEN_TILE, S)
    const = lambda b, t: (0, 0)
    return pl.pallas_call(
        _pool_in_kernel,
        grid=(B, S // T),
        in_specs=[
            pl.BlockSpec((1, T, D), lambda b, t: (b, t, 0)),
            pl.BlockSpec((1, D), const),
            pl.BlockSpec((D, TOK_WIDTH), const),
            pl.BlockSpec((MEM_WIDTH, D), const),
            pl.BlockSpec((TOK_WIDTH, TOK_WIDTH), const),
            pl.BlockSpec((1, TOK_WIDTH), const),
            pl.BlockSpec((HEAD_DIM, 1), const),
            pl.BlockSpec((1, M, MEM_WIDTH), lambda b, t: (b, 0, 0)),
            pl.BlockSpec((1, MEM_WIDTH, M), lambda b, t: (b, 0, 0)),
        ],
        out_specs=[
            pl.BlockSpec((1, T, TOK_WIDTH), lambda b, t: (b, t, 0)),
            pl.BlockSpec((1, T, MEM_WIDTH), lambda b, t: (b, t, 0)),
        ],
        out_shape=[
            jax.ShapeDtypeStruct((B, S, TOK_WIDTH), BF16),
            jax.ShapeDtypeStruct((B, S, MEM_WIDTH), BF16),
        ],
        scratch_shapes=[pltpu.VMEM((POOL_HALO, TOK_WIDTH), F32)],
        compiler_params=_params(),
        name="pool_in",
    )(x, g, w_tok, wm_t, wg, scale, mq_gain, mk, mvt)


def _selection_bias(gate, t, T):
    nb, width = gate.shape
    blk = lax.broadcasted_iota(jnp.int32, (nb, width), 0)
    pos = t * T + lax.rem(lax.broadcasted_iota(jnp.int32, (nb, width), 1), T)
    own = lax.shift_right_logical(pos, MOBA_BLOCK.bit_length() - 1)
    g = jnp.where(blk < own, gate, -jnp.inf)
    bias = jnp.full((nb, width), NEG, F32)
    for _ in range(MOBA_TOPK):
        mx = jnp.max(g, axis=0, keepdims=True)
        idx = jnp.min(jnp.where(g == mx, blk, nb), axis=0, keepdims=True)
        idx = jnp.where(mx > -jnp.inf, idx, -1)
        pick = blk == idx
        bias = jnp.where(pick, 0.0, bias)
        g = jnp.where(pick, -jnp.inf, g)
    return bias


def _moba_in_kernel(x_ref, g_ref, wqkv_t_ref, wm_t_ref, cos_ref, sin_ref, qg_ref, kg_ref, mqg_ref,
                    mk_ref, mvt_ref, qt_ref, k_ref, vt_ref, bias_ref, mo_ref, kmean_ref):
    t = pl.program_id(1)
    T = x_ref.shape[1]
    blocks_per_tile = T // MOBA_BLOCK

    @pl.when(t == 0)
    def _():
        kmean_ref[...] = jnp.zeros_like(kmean_ref)

    u = _rms_rows(x_ref[0], g_ref[...]).astype(BF16)
    cos, sin = cos_ref[...], sin_ref[...]

    def proj_t(lo, hi):
        return lax.dot_general(wqkv_t_ref[lo:hi, :], u, NT_DIMS, preferred_element_type=F32)

    k_t = proj_t(TOK_WIDTH, 2 * TOK_WIDTH)
    q_t = proj_t(0, TOK_WIDTH)

    k_heads =[_rope_t(_head_rms_t(k_t[h * HEAD_DIM:(h + 1) * HEAD_DIM], kg_ref[...]), cos, sin)
               for h in range(MOBA_HEADS)]
    k_nat = jnp.concatenate(k_heads, axis=0).T
    k_ref[0] = k_nat.astype(BF16)
    for c in range(blocks_per_tile):
        kmean_ref[pl.ds(t * blocks_per_tile + c, 1), :] = jnp.mean(
            k_nat[c * MOBA_BLOCK:(c + 1) * MOBA_BLOCK], axis=0, keepdims=True)

    km_hi, km_lo = _split_bf16(kmean_ref[...])
    heads_per_chunk = MOBA_HEADS // VALUE_CHUNKS
    chunk_rows = TOK_WIDTH // VALUE_CHUNKS
    for chunk in range(VALUE_CHUNKS):
        lo = 2 * TOK_WIDTH + chunk * chunk_rows
        v_t = proj_t(lo, lo + chunk_rows)
        for c in range(blocks_per_tile):
            vt_ref[0, c, chunk * chunk_rows:(chunk + 1) * chunk_rows, :] = (
                v_t[:, c * MOBA_BLOCK:(c + 1) * MOBA_BLOCK].astype(BF16))
        heads = range(chunk * heads_per_chunk, (chunk + 1) * heads_per_chunk)
        gates = []
        for h in heads:
            rows = slice(h * HEAD_DIM, (h + 1) * HEAD_DIM)
            pair, slot = divmod(h, 2)
            lanes = slice(pair * LANES, (pair + 1) * LANES)
            q = _rope_t(_head_rms_t(q_t[rows], qg_ref[...]), cos, sin)
            qt_ref[0, rows, :] = (q * (SM_SCALE * LOG2_E)).astype(BF16)
            q_hi, q_lo = _split_bf16(q)
            q_hi2, q_lo2 = _pair_operand(q_hi, slot), _pair_operand(q_lo, slot)
            gates.append(jnp.dot(jnp.concatenate([km_hi[:, lanes], km_lo[:, lanes]], axis=1),
                                 jnp.concatenate([q_hi2, q_hi2], axis=0), preferred_element_type=F32)
                         + jnp.dot(km_hi[:, lanes], q_lo2, preferred_element_type=F32))
        bias = _selection_bias(jnp.concatenate(gates, axis=1), t, T)
        for n, h in enumerate(heads):
            bias_ref[0, h] = bias[:, n * T:(n + 1) * T]

    qm_t = lax.dot_general(wm_t_ref[...], u, NT_DIMS, preferred_element_type=F32)
    mo_ref[0] = _mem_attention_t(qm_t, mqg_ref[...], mk_ref, mvt_ref).astype(BF16)


def _moba_in(x, g, wqkv_t, wm_t, cos_t, sin_t, q_gain, k_gain, mq_gain, mk, mvt):
    B, S, D = x.shape
    M = mk.shape[1]
    T = min(TOKEN_TILE, S)
    nb = S // MOBA_BLOCK
    const = lambda b, t: (0, 0)
    return pl.pallas_call(
        _moba_in_kernel,
        grid=(B, S // T),
        in_specs=[
            pl.BlockSpec((1, T, D), lambda b, t: (b, t, 0)),
            pl.BlockSpec((1, D), const),
            pl.BlockSpec((3 * TOK_WIDTH, D), const),
            pl.BlockSpec((MEM_WIDTH, D), const),
            pl.BlockSpec((HALF_DIM, T), lambda b, t: (0, t)),
            pl.BlockSpec((HALF_DIM, T), lambda b, t: (0, t)),
            pl.BlockSpec((HEAD_DIM, 1), const),
            pl.BlockSpec((HEAD_DIM, 1), const),
            pl.BlockSpec((HEAD_DIM, 1), const),
            pl.BlockSpec((1, M, MEM_WIDTH), lambda b, t: (b, 0, 0)),
            pl.BlockSpec((1, MEM_WIDTH, M), lambda b, t: (b, 0, 0)),
        ],
        out_specs=[
            pl.BlockSpec((1, TOK_WIDTH, T), lambda b, t: (b, 0, t)),
            pl.BlockSpec((1, T, TOK_WIDTH), lambda b, t: (b, t, 0)),
            pl.BlockSpec((1, T // MOBA_BLOCK, TOK_WIDTH, MOBA_BLOCK), lambda b, t: (b, t, 0, 0)),
            pl.BlockSpec((1, MOBA_HEADS, nb, T), lambda b, t: (b, 0, 0, t)),
            pl.BlockSpec((1, T, MEM_WIDTH), lambda b, t: (b, t, 0)),
        ],
        out_shape=[
            jax.ShapeDtypeStruct((B, TOK_WIDTH, S), BF16),
            jax.ShapeDtypeStruct((B, S, TOK_WIDTH), BF16),
            jax.ShapeDtypeStruct((B, nb, TOK_WIDTH, MOBA_BLOCK), BF16),
            jax.ShapeDtypeStruct((B, MOBA_HEADS, nb, S), F32),
            jax.ShapeDtypeStruct((B, S, MEM_WIDTH), BF16),
        ],
        scratch_shapes=[pltpu.VMEM((nb, TOK_WIDTH), F32)],
        compiler_params=_params(),
        name="moba_in",
    )(x, g, wqkv_t, wm_t, cos_t, sin_t, q_gain, k_gain, mq_gain, mk, mvt)


def _moba_attn_kernel(qt_ref, qt_next_ref, k_ref, vt_ref, bias_ref, bias_next_ref, onehot_ref, o_ref, *scratch):
    i = pl.program_id(2)
    TQ = qt_ref.shape[2]
    s_refs = (scratch[0:2], scratch[2:4])
    bm_refs = (scratch[4:6], scratch[6:8])
    group_keys = KV_GROUP * MOBA_BLOCK
    slab_rows = 2 * KV_GROUP
    live_groups = (i + KV_GROUP - 1) // KV_GROUP

    def head_operands(q_pair):
        row = lax.broadcasted_iota(jnp.int32, q_pair.shape, 0)
        zero = jnp.zeros_like(q_pair)
        return jnp.where(row < HEAD_DIM, q_pair, zero), jnp.where(row >= HEAD_DIM, q_pair, zero)

    q_heads = head_operands(qt_ref[0])
    q_heads_next = head_operands(qt_next_ref[0])
    zeros_tail = jnp.zeros((MXU_DEPTH - LANES - BIAS_ROWS, TQ), BF16)
    ones_rows = jnp.ones((SUM_ROWS, 1), BF16)

    def score_unit(group, parity, q_head, slab, s_ref, bm_ref):
        k_g = k_ref[0, pl.ds(pl.multiple_of(group * group_keys, group_keys), group_keys), :]
        lhs = jnp.concatenate([k_g, onehot_ref[parity]], axis=1)
        slab = jnp.concatenate([slab, jnp.zeros_like(slab)], axis=0).astype(BF16)
        rhs = jnp.concatenate([q_head, slab, zeros_tail], axis=0)
        s = jnp.dot(lhs, rhs, preferred_element_type=F32)
        s_ref[...] = s
        bm_ref[...] = jnp.max(s, axis=0, keepdims=True)

    def bias_slab(a, group):
        slab_row = pl.multiple_of((group // 2) * slab_rows, slab_rows)
        return bias_ref[0, a, pl.ds(slab_row, slab_rows), :]

    def produce(g, parity, a):
        score_unit(g, parity, q_heads[a], bias_slab(a, g), s_refs[parity][a], bm_refs[parity][a])

    def produce_or_prefetch(g, a):
        live = g < live_groups
        group = jnp.where(live, g, 0)
        q_head = jnp.where(live, q_heads[a], q_heads_next[a])
        slab = jnp.where(live, bias_slab(a, group), bias_next_ref[0, a])
        score_unit(group, 0, q_head, slab, s_refs[0][a], bm_refs[0][a])

    def pv_operand(v_t):
        return jnp.concatenate([v_t, jnp.broadcast_to(ones_rows, (SUM_ROWS, v_t.shape[1]))], axis=0)

    def consume(g, parity, a, state):
        j0 = g * KV_GROUP
        rows = slice(a * HEAD_DIM, (a + 1) * HEAD_DIM)
        v_g = jnp.concatenate([vt_ref[0, j0 + c, rows, :] for c in range(KV_GROUP)], axis=1)
        m, acc = state
        m_new = jnp.maximum(m, bm_refs[parity][a][...])
        alpha = jnp.exp2(m - m_new)
        p = jnp.exp2(s_refs[parity][a][...] - m_new).astype(BF16)
        pv = jnp.dot(pv_operand(v_g), p, preferred_element_type=F32)
        return m_new, alpha * acc + pv

    @pl.when(i == 0)
    def _():
        for a in range(2):
            produce_or_prefetch(0, a)

    k_own = k_ref[0, pl.ds(pl.multiple_of(i * MOBA_BLOCK, MOBA_BLOCK), MOBA_BLOCK), :]
    v_own = vt_ref[0, i]
    key_idx = lax.broadcasted_iota(jnp.int32, (MOBA_BLOCK, TQ), 0)
    qry_idx = lax.broadcasted_iota(jnp.int32, (MOBA_BLOCK, TQ), 1)
    causal = key_idx <= qry_idx
    s_own = [jnp.where(causal, jnp.dot(k_own, q_heads[a], preferred_element_type=F32), NEG)
             for a in range(2)]
    init = []
    for a in range(2):
        m = jnp.max(s_own[a], axis=0, keepdims=True)
        p = jnp.exp2(s_own[a] - m).astype(BF16)
        acc = jnp.dot(pv_operand(v_own[a * HEAD_DIM:(a + 1) * HEAD_DIM]), p, preferred_element_type=F32)
        init.append((m, acc))

    def run_steps(groups_per_step, first_group, n_steps, carry):
        def body(step, carry):
            g = first_group + groups_per_step * step
            state = list(carry)
            for sub in range(groups_per_step):
                cur = sub % 2
                for a in range(2):
                    if sub == groups_per_step - 1:
                        produce_or_prefetch(g + sub + 1, a)
                    else:
                        produce(g + sub + 1, 1 - cur, a)
                    state[a] = consume(g + sub, cur, a, state[a])
            return tuple(state)
        return lax.fori_loop(0, n_steps, body, carry)

    carry, done = tuple(init), 0
    for size in STEP_GROUPS:
        n_steps = (live_groups - done) // size
        carry = run_steps(size, done, n_steps, carry)
        done = done + n_steps * size

    def last_group(_, carry):
        state = list(carry)
        for a in range(2):
            state[a] = consume(done, 0, a, state[a])
            produce_or_prefetch(done + 1, a)
        return tuple(state)

    final = lax.fori_loop(0, live_groups - done, last_group, carry)
    o_t = jnp.concatenate([acc[:HEAD_DIM] / acc[HEAD_DIM:HEAD_DIM + 1] for (_, acc) in final], axis=0)
    o_ref[0] = o_t.T.astype(BF16)


def _block_onehot():
    key_block = jnp.arange(KV_GROUP * MOBA_BLOCK, dtype=jnp.int32) // MOBA_BLOCK
    col = jnp.arange(LANES, dtype=jnp.int32)
    parity = jnp.arange(2, dtype=jnp.int32)
    hit = col[None, None, :] == (parity[:, None, None] * KV_GROUP + key_block[None, :, None])
    return hit.astype(BF16)


def _moba_attn(qt, k, vt, bias):
    B, _, S = qt.shape
    nb = S // MOBA_BLOCK
    assert 2 * KV_GROUP == F32_SUBLANES and nb % F32_SUBLANES == 0, "bias slabs are whole f32 sublane tiles"
    assert STEP_GROUPS[-1] == 2 and all(size % 2 == 0 for size in STEP_GROUPS), "score buffers alternate by group parity"
    pairs = MOBA_HEADS // 2
    group_keys = KV_GROUP * MOBA_BLOCK
    return pl.pallas_call(
        _moba_attn_kernel,
        grid=(B, pairs, nb),
        in_specs=[
            pl.BlockSpec((1, LANES, MOBA_BLOCK), lambda b, p, i: (b, p, i)),
            pl.BlockSpec((1, LANES, MOBA_BLOCK), lambda b, p, i: (b, p, jnp.minimum(i + 1, nb - 1))),
            pl.BlockSpec((1, S, LANES), lambda b, p, i: (b, 0, p)),
            pl.BlockSpec((1, nb, LANES, MOBA_BLOCK), lambda b, p, i: (b, 0, p, 0)),
            pl.BlockSpec((1, 2, nb, MOBA_BLOCK), lambda b, p, i: (b, p, 0, i)),
            pl.BlockSpec((1, 2, F32_SUBLANES, MOBA_BLOCK), lambda b, p, i: (b, p, 0, jnp.minimum(i + 1, nb - 1))),
            pl.BlockSpec((2, group_keys, LANES), lambda b, p, i: (0, 0, 0)),
        ],
        out_specs=pl.BlockSpec((1, MOBA_BLOCK, LANES), lambda b, p, i: (b, i, p)),
        out_shape=jax.ShapeDtypeStruct((B, S, TOK_WIDTH), BF16),
        scratch_shapes=[pltpu.VMEM((group_keys, MOBA_BLOCK), F32)] * 4 + [pltpu.VMEM((1, MOBA_BLOCK), F32)] * 4,
        compiler_params=pltpu.CompilerParams(
            dimension_semantics=("arbitrary", "arbitrary", "arbitrary"), vmem_limit_bytes=VMEM_LIMIT_BYTES),
        name="moba_attn",
    )(qt, qt, k, vt, bias, bias, _block_onehot())


def _post_kernel(x_ref, tok_ref, mo_ref, wo_ref, g_ref, w1_ref, w2_ref, o_ref):
    mixed = jnp.concatenate([tok_ref[0], mo_ref[0]], axis=-1)
    x1 = x_ref[0] + jnp.dot(mixed, wo_ref[...], preferred_element_type=F32)
    u = _rms_rows(x1, g_ref[...]).astype(BF16)
    h = jnp.dot(u, w1_ref[...], preferred_element_type=F32)
    h = jnp.square(jnp.maximum(h, 0.0)).astype(BF16)
    o_ref[0] = x1 + jnp.dot(h, w2_ref[...], preferred_element_type=F32)


def _post(x, tok, mo, wo, g, w1, w2):
    B, S, D = x.shape
    T = min(TOKEN_TILE, S)
    const = lambda b, t: (0, 0)
    resident = functools.partial(pl.BlockSpec, index_map=const, pipeline_mode=pl.Buffered(1))
    return pl.pallas_call(
        _post_kernel,
        grid=(B, S // T),
        in_specs=[
            pl.BlockSpec((1, T, D), lambda b, t: (b, t, 0)),
            pl.BlockSpec((1, T, TOK_WIDTH), lambda b, t: (b, t, 0)),
            pl.BlockSpec((1, T, MEM_WIDTH), lambda b, t: (b, t, 0)),
            resident((D, D)),
            pl.BlockSpec((1, D), const),
            resident((D, D_FF)),
            resident((D_FF, D)),
        ],
        out_specs=pl.BlockSpec((1, T, D), lambda b, t: (b, t, 0)),
        out_shape=jax.ShapeDtypeStruct((B, S, D), F32),
        compiler_params=_params(),
        name="post",
    )(x, tok, mo, wo, g, w1, w2)


def _rope_tables_t(seq_len):
    pos = jnp.arange(seq_len, dtype=F32)
    inv = ROPE_THETA ** (-jnp.arange(0, HEAD_DIM, 2, dtype=F32) / HEAD_DIM)
    ang = pos[:, None] * inv[None, :]
    return jnp.cos(ang).T, jnp.sin(ang).T


def _block_diag(w_group):
    G, C, _ = w_group.shape
    eye = jnp.eye(G, dtype=w_group.dtype)
    return (eye[:, None, :, None] * w_group[:, :, None, :]).reshape(G * C, G * C)


def kernel(x, mem, g_mix, g_mem, g_mlp, w_in_pool, w_pool_group, pool_scale, w_in_moba, moba_q_gain,
           moba_k_gain, w_mem_kv, mem_q_gain, mem_k_gain, w_out, w_ff1, w_ff2):
    depth = g_mix.shape[0]
    S = x.shape[1]
    cos_t, sin_t = _rope_tables_t(S)
    col = lambda v: v.reshape(-1, 1)
    row = lambda v: v.reshape(1, -1)
    for i in range(depth):
        j = i // 2
        mk, mvt = _mem_kv(mem, row(g_mem[i]), w_mem_kv[i].T.astype(BF16), col(mem_k_gain[i]))
        if i % 2 == 0:
            w = w_in_pool[j]
            tok, mo = _pool_in(
                x, row(g_mix[i]), w[:, :TOK_WIDTH].astype(BF16), w[:, TOK_WIDTH:].T.astype(BF16),
                _block_diag(w_pool_group[j]).astype(BF16), row(pool_scale[j]), col(mem_q_gain[i]), mk, mvt)
        else:
            w = w_in_moba[j]
            qt, k, vt, bias, mo = _moba_in(
                x, row(g_mix[i]), w[:, :3 * TOK_WIDTH].T.astype(BF16), w[:, 3 * TOK_WIDTH:].T.astype(BF16),
                cos_t, sin_t, col(moba_q_gain[j]), col(moba_k_gain[j]), col(mem_q_gain[i]), mk, mvt)
            tok = _moba_attn(qt, k, vt, bias)
        x = _post(x, tok, mo, w_out[i].astype(BF16), row(g_mlp[i]), w_ff1[i].astype(BF16), w_ff2[i].astype(BF16))
    return x
```

```python
import functools

import jax
import jax.numpy as jnp
from jax import lax
from jax.experimental import pallas as pl
from jax.experimental.pallas import tpu as pltpu

D_MODEL = 1024
HEAD_DIM = 64
HALF_DIM = HEAD_DIM // 2
MEM_HEADS = 4
MEM_WIDTH = MEM_HEADS * HEAD_DIM
TOK_WIDTH = D_MODEL - MEM_WIDTH
MOBA_HEADS = TOK_WIDTH // HEAD_DIM
MOBA_BLOCK = 256
MOBA_TOPK = 3
POOL_WINDOWS = (2, 4, 8, 16)
POOL_GROUP_WIDTH = TOK_WIDTH // len(POOL_WINDOWS)
POOL_HALO = 16
D_FF = 4 * D_MODEL
ROPE_THETA = 10000.0
EPS = 1e-6
NEG = -1e30
SM_SCALE = HEAD_DIM ** -0.5
LOG2_E = 1.4426950408889634

LANES = 128
F32_SUBLANES = 8
BIAS_ROWS = 16
SUM_ROWS = 16
MXU_DEPTH = 256
VMEM_LIMIT_BYTES = 56 * 1024 * 1024

TOKEN_TILE = 512
VALUE_CHUNKS = 3
QUERY_BLOCKS = 1
KV_GROUP = 4
STEP_GROUPS = (8, 4, 2)

F32 = jnp.float32
BF16 = jnp.bfloat16
NT_DIMS = (((1,), (1,)), ((), ()))


def _params():
    return pltpu.CompilerParams(
        dimension_semantics=("arbitrary", "arbitrary"), vmem_limit_bytes=VMEM_LIMIT_BYTES)


def _rms_rows(x, g):
    ms = jnp.mean(x * x, axis=-1, keepdims=True)
    return x * lax.rsqrt(ms + EPS) * g


def _head_rms_t(h, gain_col):
    ms = jnp.mean(h * h, axis=0, keepdims=True)
    return h * lax.rsqrt(ms + EPS) * gain_col


def _rope_t(h, cos, sin):
    h1, h2 = h[:HALF_DIM], h[HALF_DIM:]
    return jnp.concatenate([h1 * cos - h2 * sin, h2 * cos + h1 * sin], axis=0)


def _split_bf16(x):
    hi = x.astype(BF16)
    return hi, (x - hi.astype(F32)).astype(BF16)


def _pair_operand(q_bf, slot):
    z = jnp.zeros_like(q_bf)
    return jnp.concatenate([q_bf, z] if slot == 0 else [z, q_bf], axis=0)


def _mem_attention_t(qm_t, mq_gain, mk_ref, mvt_ref):
    scores = []
    for h in range(MEM_HEADS):
        pair, slot = divmod(h, 2)
        q = _head_rms_t(qm_t[h * HEAD_DIM:(h + 1) * HEAD_DIM], mq_gain)
        q2 = _pair_operand((q * (SM_SCALE * LOG2_E)).astype(BF16), slot)
        k_pair = mk_ref[0, :, pair * LANES:(pair + 1) * LANES]
        scores.append(jnp.dot(k_pair, q2, preferred_element_type=F32))
    probs = [jnp.exp2(s - jnp.max(s, axis=0, keepdims=True)).astype(BF16) for s in scores]
    outs = []
    for h in range(MEM_HEADS):
        v_t = mvt_ref[0, h * HEAD_DIM:(h + 1) * HEAD_DIM, :]
        ones = jnp.ones((SUM_ROWS, v_t.shape[1]), BF16)
        r = jnp.dot(jnp.concatenate([v_t, ones], axis=0), probs[h], preferred_element_type=F32)
        outs.append(r[:HEAD_DIM] / r[HEAD_DIM:HEAD_DIM + 1])
    return jnp.concatenate(outs, axis=0).T


def _mem_kv_kernel(mem_ref, g_ref, wkv_t_ref, kg_ref, mk_ref, mvt_ref):
    mem_n = _rms_rows(mem_ref[0], g_ref[...]).astype(BF16)
    kv_t = lax.dot_general(wkv_t_ref[...], mem_n, NT_DIMS, preferred_element_type=F32)
    k_heads = [_head_rms_t(kv_t[h * HEAD_DIM:(h + 1) * HEAD_DIM], kg_ref[...]) for h in range(MEM_HEADS)]
    mk_ref[0] = jnp.concatenate(k_heads, axis=0).T.astype(BF16)
    mvt_ref[0] = kv_t[MEM_WIDTH:].astype(BF16)


def _mem_kv(mem, g, wkv_t, k_gain):
    B, M, D = mem.shape
    return pl.pallas_call(
        _mem_kv_kernel,
        grid=(B, 1),
        in_specs=[
            pl.BlockSpec((1, M, D), lambda b, _: (b, 0, 0)),
            pl.BlockSpec((1, D), lambda b, _: (0, 0)),
            pl.BlockSpec((2 * MEM_WIDTH, D), lambda b, _: (0, 0)),
            pl.BlockSpec((HEAD_DIM, 1), lambda b, _: (0, 0)),
        ],
        out_specs=[
            pl.BlockSpec((1, M, MEM_WIDTH), lambda b, _: (b, 0, 0)),
            pl.BlockSpec((1, MEM_WIDTH, M), lambda b, _: (b, 0, 0)),
        ],
        out_shape=[
            jax.ShapeDtypeStruct((B, M, MEM_WIDTH), BF16),
            jax.ShapeDtypeStruct((B, MEM_WIDTH, M), BF16),
        ],
        compiler_params=_params(),
        name="mem_kv",
    )(mem, g, wkv_t, k_gain)


def _pool_in_kernel(x_ref, g_ref, w_tok_ref, wm_t_ref, wg_ref, scale_ref, mqg_ref, mk_ref, mvt_ref,
                    tok_ref, mo_ref, halo_ref):
    t = pl.program_id(1)
    T = x_ref.shape[1]

    @pl.when(t == 0)
    def _():
        halo_ref[...] = jnp.zeros_like(halo_ref)

    u = _rms_rows(x_ref[0], g_ref[...]).astype(BF16)
    h = jnp.dot(u, w_tok_ref[...], preferred_element_type=F32)
    qm_t = lax.dot_general(wm_t_ref[...], u, NT_DIMS, preferred_element_type=F32)

    ext = jnp.concatenate([halo_ref[...], h], axis=0)
    halo_ref[...] = h[T - POOL_HALO:]
    s2 = ext + pltpu.roll(ext, 1, 0)
    s4 = s2 + pltpu.roll(s2, 2, 0)
    s8 = s4 + pltpu.roll(s4, 4, 0)
    s16 = s8 + pltpu.roll(s8, 8, 0)
    lane = lax.broadcasted_iota(jnp.int32, (T, TOK_WIDTH), 1)
    pos = t * T + lax.broadcasted_iota(jnp.int32, (T, TOK_WIDTH), 0)
    sums = (s2, s4, s8, s16)
    win_sum = sums[-1][POOL_HALO:]
    window = jnp.full((T, TOK_WIDTH), POOL_WINDOWS[-1], jnp.int32)
    for gi in range(len(POOL_WINDOWS) - 2, -1, -1):
        in_group = lane < (gi + 1) * POOL_GROUP_WIDTH
        win_sum = jnp.where(in_group, sums[gi][POOL_HALO:], win_sum)
        window = jnp.where(in_group, POOL_WINDOWS[gi], window)
    count = jnp.minimum(pos + 1, window).astype(F32)
    d = (win_sum / count - h).astype(BF16)
    y = jnp.dot(d, wg_ref[...], preferred_element_type=F32) * scale_ref[...]
    tok_ref[0] = y.astype(BF16)

    mo_ref[0] = _mem_attention_t(qm_t, mqg_ref[...], mk_ref, mvt_ref).astype(BF16)


def _pool_in(x, g, w_tok, wm_t, wg, scale, mq_gain, mk, mvt):
    B, S, D = x.shape
    M = mk.shape[1]
    T = min(TOKEN_TILE, S)
    const = lambda b, t: (0, 0)
    return pl.pallas_call(
        _pool_in_kernel,
        grid=(B, S // T),
        in_specs=[
            pl.BlockSpec((1, T, D), lambda b, t: (b, t, 0)),
            pl.BlockSpec((1, D), const),
            pl.BlockSpec((D, TOK_WIDTH), const),
            pl.BlockSpec((MEM_WIDTH, D), const),
            pl.BlockSpec((TOK_WIDTH, TOK_WIDTH), const),
            pl.BlockSpec((1, TOK_WIDTH), const),
            pl.BlockSpec((HEAD_DIM, 1), const),
            pl.BlockSpec((1, M, MEM_WIDTH), lambda b, t: (b, 0, 0)),
            pl.BlockSpec((1, MEM_WIDTH, M), lambda b, t: (b, 0, 0)),
        ],
        out_specs=[
            pl.BlockSpec((1, T, TOK_WIDTH), lambda b, t: (b, t, 0)),
            pl.BlockSpec((1, T, MEM_WIDTH), lambda b, t: (b, t, 0)),
        ],
        out_shape=[
            jax.ShapeDtypeStruct((B, S, TOK_WIDTH), BF16),
            jax.ShapeDtypeStruct((B, S, MEM_WIDTH), BF16),
        ],
        scratch_shapes=[pltpu.VMEM((POOL_HALO, TOK_WIDTH), F32)],
        compiler_params=_params(),
        name="pool_in",
    )(x, g, w_tok, wm_t, wg, scale, mq_gain, mk, mvt)


def _selection_bias(gate, t, T):
    nb, width = gate.shape
    blk = lax.broadcasted_iota(jnp.int32, (nb, width), 0)
    pos = t * T + lax.rem(lax.broadcasted_iota(jnp.int32, (nb, width), 1), T)
    own = lax.shift_right_logical(pos, MOBA_BLOCK.bit_length() - 1)
    g = jnp.where(blk < own, gate, -jnp.inf)
    bias = jnp.full((nb, width), NEG, F32)
    for _ in range(MOBA_TOPK):
        mx = jnp.max(g, axis=0, keepdims=True)
        idx = jnp.min(jnp.where(g == mx, blk, nb), axis=0, keepdims=True)
        idx = jnp.where(mx > -jnp.inf, idx, -1)
        pick = blk == idx
        bias = jnp.where(pick, 0.0, bias)
        g = jnp.where(pick, -jnp.inf, g)
    return bias


def _moba_in_kernel(x_ref, g_ref, wqkv_t_ref, wm_t_ref, cos_ref, sin_ref, qg_ref, kg_ref, mqg_ref,
                    mk_ref, mvt_ref, qt_ref, k_ref, vt_ref, bias_ref, mo_ref, kmean_ref):
    t = pl.program_id(1)
    T = x_ref.shape[1]
    blocks_per_tile = T // MOBA_BLOCK

    @pl.when(t == 0)
    def _():
        kmean_ref[...] = jnp.zeros_like(kmean_ref)

    u = _rms_rows(x_ref[0], g_ref[...]).astype(BF16)
    cos, sin = cos_ref[...], sin_ref[...]

    def proj_t(lo, hi):
        return lax.dot_general(wqkv_t_ref[lo:hi, :], u, NT_DIMS, preferred_element_type=F32)

    k_t = proj_t(TOK_WIDTH, 2 * TOK_WIDTH)
    q_t = proj_t(0, TOK_WIDTH)

    k_heads = [_rope_t(_head_rms_t(k_t[h * HEAD_DIM:(h + 1) * HEAD_DIM], kg_ref[...]), cos, sin)
               for h in range(MOBA_HEADS)]
    k_nat = jnp.concatenate(k_heads, axis=0).T
    k_ref[0] = k_nat.astype(BF16)
    for c in range(blocks_per_tile):
        kmean_ref[pl.ds(t * blocks_per_tile + c, 1), :] = jnp.mean(
            k_nat[c * MOBA_BLOCK:(c + 1) * MOBA_BLOCK], axis=0, keepdims=True)

    km_hi, km_lo = _split_bf16(kmean_ref[...])
    heads_per_chunk = MOBA_HEADS // VALUE_CHUNKS
    chunk_rows = TOK_WIDTH // VALUE_CHUNKS
    for chunk in range(VALUE_CHUNKS):
        lo = 2 * TOK_WIDTH + chunk * chunk_rows
        v_t = proj_t(lo, lo + chunk_rows)
        for c in range(blocks_per_tile):
            vt_ref[0, c, chunk * chunk_rows:(chunk + 1) * chunk_rows, :] = (
                v_t[:, c * MOBA_BLOCK:(c + 1) * MOBA_BLOCK].astype(BF16))
        heads = range(chunk * heads_per_chunk, (chunk + 1) * heads_per_chunk)
        gates = []
        for h in heads:
            rows = slice(h * HEAD_DIM, (h + 1) * HEAD_DIM)
            pair, slot = divmod(h, 2)
            lanes = slice(pair * LANES, (pair + 1) * LANES)
            q = _rope_t(_head_rms_t(q_t[rows], qg_ref[...]), cos, sin)
            qt_ref[0, rows, :] = (q * (SM_SCALE * LOG2_E)).astype(BF16)
            q_hi, q_lo = _split_bf16(q)
            q_hi2, q_lo2 = _pair_operand(q_hi, slot), _pair_operand(q_lo, slot)
            gates.append(jnp.dot(jnp.concatenate([km_hi[:, lanes], km_lo[:, lanes]], axis=1),
                                 jnp.concatenate([q_hi2, q_hi2], axis=0), preferred_element_type=F32)
                         + jnp.dot(km_hi[:, lanes], q_lo2, preferred_element_type=F32))
        bias = _selection_bias(jnp.concatenate(gates, axis=1), t, T)
        for n, h in enumerate(heads):
            bias_ref[0, h] = bias[:, n * T:(n + 1) * T]

    qm_t = lax.dot_general(wm_t_ref[...], u, NT_DIMS, preferred_element_type=F32)
    mo_ref[0] = _mem_attention_t(qm_t, mqg_ref[...], mk_ref, mvt_ref).astype(BF16)


def _moba_in(x, g, wqkv_t, wm_t, cos_t, sin_t, q_gain, k_gain, mq_gain, mk, mvt):
    B, S, D = x.shape
    M = mk.shape[1]
    T = min(TOKEN_TILE, S)
    nb = S // MOBA_BLOCK
    const = lambda b, t: (0, 0)
    return pl.pallas_call(
        _moba_in_kernel,
        grid=(B, S // T),
        in_specs=[
            pl.BlockSpec((1, T, D), lambda b, t: (b, t, 0)),
            pl.BlockSpec((1, D), const),
            pl.BlockSpec((3 * TOK_WIDTH, D), const),
            pl.BlockSpec((MEM_WIDTH, D), const),
            pl.BlockSpec((HALF_DIM, T), lambda b, t: (0, t)),
            pl.BlockSpec((HALF_DIM, T), lambda b, t: (0, t)),
            pl.BlockSpec((HEAD_DIM, 1), const),
            pl.BlockSpec((HEAD_DIM, 1), const),
            pl.BlockSpec((HEAD_DIM, 1), const),
            pl.BlockSpec((1, M, MEM_WIDTH), lambda b, t: (b, 0, 0)),
            pl.BlockSpec((1, MEM_WIDTH, M), lambda b, t: (b, 0, 0)),
        ],
        out_specs=[
            pl.BlockSpec((1, TOK_WIDTH, T), lambda b, t: (b, 0, t)),
            pl.BlockSpec((1, T, TOK_WIDTH), lambda b, t: (b, t, 0)),
            pl.BlockSpec((1, T // MOBA_BLOCK, TOK_WIDTH, MOBA_BLOCK), lambda b, t: (b, t, 0, 0)),
            pl.BlockSpec((1, MOBA_HEADS, nb, T), lambda b, t: (b, 0, 0, t)),
            pl.BlockSpec((1, T, MEM_WIDTH), lambda b, t: (b, t, 0)),
        ],
        out_shape=[
            jax.ShapeDtypeStruct((B, TOK_WIDTH, S), BF16),
            jax.ShapeDtypeStruct((B, S, TOK_WIDTH), BF16),
            jax.ShapeDtypeStruct((B, nb, TOK_WIDTH, MOBA_BLOCK), BF16),
            jax.ShapeDtypeStruct((B, MOBA_HEADS, nb, S), F32),
            jax.ShapeDtypeStruct((B, S, MEM_WIDTH), BF16),
        ],
        scratch_shapes=[pltpu.VMEM((nb, TOK_WIDTH), F32)],
        compiler_params=_params(),
        name="moba_in",
    )(x, g, wqkv_t, wm_t, cos_t, sin_t, q_gain, k_gain, mq_gain, mk, mvt)


def _moba_attn_kernel(qt_ref, k_ref, vt_ref, bias_ref, onehot_ref, o_ref, *scratch):
    TQ = qt_ref.shape[2]
    first_block = pl.program_id(2) * QUERY_BLOCKS
    s_refs = (scratch[0:2], scratch[2:4])
    bm_refs = (scratch[4:6], scratch[6:8])
    group_keys = KV_GROUP * MOBA_BLOCK
    total_groups = k_ref.shape[1] // group_keys
    past_blocks = first_block + QUERY_BLOCKS - 1
    live_groups = (past_blocks + KV_GROUP - 1) // KV_GROUP

    q_pair = qt_ref[0]
    row = lax.broadcasted_iota(jnp.int32, q_pair.shape, 0)
    zero = jnp.zeros_like(q_pair)
    q_heads = (jnp.where(row < HEAD_DIM, q_pair, zero), jnp.where(row >= HEAD_DIM, q_pair, zero))

    zeros_tail = jnp.zeros((MXU_DEPTH - LANES - BIAS_ROWS, TQ), BF16)
    ones_rows = jnp.ones((SUM_ROWS, 1), BF16)

    def produce(g, parity, a, s_ref, bm_ref):
        gc = jnp.minimum(g, total_groups - 1)
        k_g = k_ref[0, pl.ds(pl.multiple_of(gc * group_keys, group_keys), group_keys), :]
        lhs = jnp.concatenate([k_g, onehot_ref[parity]], axis=1)
        penalty = jnp.where(g >= live_groups, NEG, 0.0)
        slab_row = pl.multiple_of((gc // 2) * 2 * KV_GROUP, 2 * KV_GROUP)
        slab = bias_ref[0, a, pl.ds(slab_row, 2 * KV_GROUP), :] + penalty
        slab = jnp.concatenate([slab, jnp.zeros_like(slab)], axis=0).astype(BF16)
        rhs = jnp.concatenate([q_heads[a], slab, zeros_tail], axis=0)
        s = jnp.dot(lhs, rhs, preferred_element_type=F32)
        s_ref[...] = s
        bm_ref[...] = jnp.max(s, axis=0, keepdims=True)

    def pv_operand(v_t):
        return jnp.concatenate([v_t, jnp.broadcast_to(ones_rows, (SUM_ROWS, v_t.shape[1]))], axis=0)

    def consume(g, a, s_ref, bm_ref, state):
        gc = jnp.minimum(g, total_groups - 1)
        j0 = gc * KV_GROUP
        rows = slice(a * HEAD_DIM, (a + 1) * HEAD_DIM)
        v_g = jnp.concatenate([vt_ref[0, j0 + c, rows, :] for c in range(KV_GROUP)], axis=1)
        m, acc = state
        m_new = jnp.maximum(m, bm_ref[...])
        alpha = jnp.exp2(m - m_new)
        p = jnp.exp2(s_ref[...] - m_new).astype(BF16)
        pv = jnp.dot(pv_operand(v_g), p, preferred_element_type=F32)
        return m_new, alpha * acc + pv

    key_idx = lax.broadcasted_iota(jnp.int32, (MOBA_BLOCK, MOBA_BLOCK), 0)
    qry_idx = lax.broadcasted_iota(jnp.int32, (MOBA_BLOCK, MOBA_BLOCK), 1)
    causal = key_idx <= qry_idx
    own = []
    for qb in range(QUERY_BLOCKS):
        cols = slice(qb * MOBA_BLOCK, (qb + 1) * MOBA_BLOCK)
        k_own = k_ref[0, pl.ds(pl.multiple_of((first_block + qb) * MOBA_BLOCK, MOBA_BLOCK), MOBA_BLOCK), :]
        own.append([jnp.where(causal, jnp.dot(k_own, q_heads[a][:, cols], preferred_element_type=F32), NEG)
                    for a in range(2)])
    produce(0, 0, 0, s_refs[0][0], bm_refs[0][0])
    produce(0, 0, 1, s_refs[0][1], bm_refs[0][1])
    init = []
    for a in range(2):
        rows = slice(a * HEAD_DIM, (a + 1) * HEAD_DIM)
        ms, accs = [], []
        for qb in range(QUERY_BLOCKS):
            m = jnp.max(own[qb][a], axis=0, keepdims=True)
            p = jnp.exp2(own[qb][a] - m).astype(BF16)
            ms.append(m)
            accs.append(jnp.dot(pv_operand(vt_ref[0, first_block + qb, rows, :]), p, preferred_element_type=F32))
        init.append((jnp.concatenate(ms, axis=1), jnp.concatenate(accs, axis=1)))

    def run_steps(groups_per_step, first_group, n_steps, carry):
        def body(step, carry):
            g = first_group + groups_per_step * step
            state = list(carry)
            for sub in range(groups_per_step):
                cur = sub % 2
                nxt = 1 - cur
                for a in range(2):
                    produce(g + sub + 1, nxt, a, s_refs[nxt][a], bm_refs[nxt][a])
                    state[a] = consume(g + sub, a, s_refs[cur][a], bm_refs[cur][a], state[a])
            return tuple(state)
        return lax.fori_loop(0, n_steps, body, carry)

    carry, done = tuple(init), 0
    for size in STEP_GROUPS:
        n_steps = (live_groups - done) // size
        carry = run_steps(size, done, n_steps, carry)
        done = done + n_steps * size

    def last_group(_, carry):
        return tuple(consume(done, a, s_refs[0][a], bm_refs[0][a], carry[a]) for a in range(2))

    final = lax.fori_loop(0, live_groups - done, last_group, carry)
    o_t = jnp.concatenate([acc[:HEAD_DIM] / acc[HEAD_DIM:HEAD_DIM + 1] for (_, acc) in final], axis=0)
    o_ref[0] = o_t.T.astype(BF16)


def _block_onehot():
    key_block = jnp.arange(KV_GROUP * MOBA_BLOCK, dtype=jnp.int32) // MOBA_BLOCK
    col = jnp.arange(LANES, dtype=jnp.int32)
    parity = jnp.arange(2, dtype=jnp.int32)
    hit = col[None, None, :] == (parity[:, None, None] * KV_GROUP + key_block[None, :, None])
    return hit.astype(BF16)


def _moba_attn(qt, k, vt, bias):
    B, _, S = qt.shape
    nb = S // MOBA_BLOCK
    tq = QUERY_BLOCKS * MOBA_BLOCK
    assert nb % QUERY_BLOCKS == 0
    assert 2 * KV_GROUP == F32_SUBLANES and nb % F32_SUBLANES == 0, "bias slabs are whole f32 sublane tiles"
    assert STEP_GROUPS[-1] == 2 and all(size % 2 == 0 for size in STEP_GROUPS), "score buffers alternate by group parity"
    pairs = MOBA_HEADS // 2
    group_keys = KV_GROUP * MOBA_BLOCK
    return pl.pallas_call(
        _moba_attn_kernel,
        grid=(B, pairs, nb // QUERY_BLOCKS),
        in_specs=[
            pl.BlockSpec((1, LANES, tq), lambda b, p, i: (b, p, i)),
            pl.BlockSpec((1, S, LANES), lambda b, p, i: (b, 0, p)),
            pl.BlockSpec((1, nb, LANES, MOBA_BLOCK), lambda b, p, i: (b, 0, p, 0)),
            pl.BlockSpec((1, 2, nb, tq), lambda b, p, i: (b, p, 0, i)),
            pl.BlockSpec((2, group_keys, LANES), lambda b, p, i: (0, 0, 0)),
        ],
        out_specs=pl.BlockSpec((1, tq, LANES), lambda b, p, i: (b, i, p)),
        out_shape=jax.ShapeDtypeStruct((B, S, TOK_WIDTH), BF16),
        scratch_shapes=[pltpu.VMEM((group_keys, tq), F32)] * 4 + [pltpu.VMEM((1, tq), F32)] * 4,
        compiler_params=pltpu.CompilerParams(
            dimension_semantics=("arbitrary", "arbitrary", "arbitrary"), vmem_limit_bytes=VMEM_LIMIT_BYTES),
        name="moba_attn",
    )(qt, k, vt, bias, _block_onehot())


def _post_kernel(x_ref, tok_ref, mo_ref, wo_ref, g_ref, w1_ref, w2_ref, o_ref):
    mixed = jnp.concatenate([tok_ref[0], mo_ref[0]], axis=-1)
    x1 = x_ref[0] + jnp.dot(mixed, wo_ref[...], preferred_element_type=F32)
    u = _rms_rows(x1, g_ref[...]).astype(BF16)
    h = jnp.dot(u, w1_ref[...], preferred_element_type=F32)
    h = jnp.square(jnp.maximum(h, 0.0)).astype(BF16)
    o_ref[0] = x1 + jnp.dot(h, w2_ref[...], preferred_element_type=F32)


def _post(x, tok, mo, wo, g, w1, w2):
    B, S, D = x.shape
    T = min(TOKEN_TILE, S)
    const = lambda b, t: (0, 0)
    resident = functools.partial(pl.BlockSpec, index_map=const, pipeline_mode=pl.Buffered(1))
    return pl.pallas_call(
        _post_kernel,
        grid=(B, S // T),
        in_specs=[
            pl.BlockSpec((1, T, D), lambda b, t: (b, t, 0)),
            pl.BlockSpec((1, T, TOK_WIDTH), lambda b, t: (b, t, 0)),
            pl.BlockSpec((1, T, MEM_WIDTH), lambda b, t: (b, t, 0)),
            resident((D, D)),
            pl.BlockSpec((1, D), const),
            resident((D, D_FF)),
            resident((D_FF, D)),
        ],
        out_specs=pl.BlockSpec((1, T, D), lambda b, t: (b, t, 0)),
        out_shape=jax.ShapeDtypeStruct((B, S, D), F32),
        compiler_params=_params(),
        name="post",
    )(x, tok, mo, wo, g, w1, w2)


def _rope_tables_t(seq_len):
    pos = jnp.arange(seq_len, dtype=F32)
    inv = ROPE_THETA ** (-jnp.arange(0, HEAD_DIM, 2, dtype=F32) / HEAD_DIM)
    ang = pos[:, None] * inv[None, :]
    return jnp.cos(ang).T, jnp.sin(ang).T


def _block_diag(w_group):
    G, C, _ = w_group.shape
    eye = jnp.eye(G, dtype=w_group.dtype)
    return (eye[:, None, :, None] * w_group[:, :, None, :]).reshape(G * C, G * C)


def kernel(x, mem, g_mix, g_mem, g_mlp, w_in_pool, w_pool_group, pool_scale, w_in_moba, moba_q_gain,
           moba_k_gain, w_mem_kv, mem_q_gain, mem_k_gain, w_out, w_ff1, w_ff2):
    depth = g_mix.shape[0]
    S = x.shape[1]
    cos_t, sin_t = _rope_tables_t(S)
    col = lambda v: v.reshape(-1, 1)
    row = lambda v: v.reshape(1, -1)
    for i in range(depth):
        j = i // 2
        mk, mvt = _mem_kv(mem, row(g_mem[i]), w_mem_kv[i].T.astype(BF16), col(mem_k_gain[i]))
        if i % 2 == 0:
            w = w_in_pool[j]
            tok, mo = _pool_in(
                x, row(g_mix[i]), w[:, :TOK_WIDTH].astype(BF16), w[:, TOK_WIDTH:].T.astype(BF16),
                _block_diag(w_pool_group[j]).astype(BF16), row(pool_scale[j]), col(mem_q_gain[i]), mk, mvt)
        else:
            w = w_in_moba[j]
            qt, k, vt, bias, mo = _moba_in(
                x, row(g_mix[i]), w[:, :3 * TOK_WIDTH].T.astype(BF16), w[:, 3 * TOK_WIDTH:].T.astype(BF16),
                cos_t, sin_t, col(moba_q_gain[j]), col(moba_k_gain[j]), col(mem_q_gain[i]), mk, mvt)
            tok = _moba_attn(qt, k, vt, bias)
        x = _post(x, tok, mo, w_out[i].astype(BF16), row(g_mlp[i]), w_ff1[i].astype(BF16), w_ff2[i].astype(BF16))
    return x
```

```python
import functools

import jax
import jax.numpy as jnp
from jax import lax
from jax.experimental import pallas as pl
from jax.experimental.pallas import tpu as pltpu

D_MODEL = 1024
HEAD_DIM = 64
HALF_DIM = HEAD_DIM // 2
MEM_HEADS = 4
MEM_WIDTH = MEM_HEADS * HEAD_DIM
TOK_WIDTH = D_MODEL - MEM_WIDTH
MOBA_HEADS = TOK_WIDTH // HEAD_DIM
MOBA_BLOCK = 256
MOBA_TOPK = 3
POOL_WINDOWS = (2, 4, 8, 16)
POOL_GROUP_WIDTH = TOK_WIDTH // len(POOL_WINDOWS)
POOL_HALO = 16
D_FF = 4 * D_MODEL
ROPE_THETA = 10000.0
EPS = 1e-6
NEG = -1e30
SM_SCALE = HEAD_DIM ** -0.5
LOG2_E = 1.4426950408889634

LANES = 128
F32_SUBLANES = 8
BIAS_ROWS = 16
SUM_ROWS = 16
MXU_DEPTH = 256
VMEM_LIMIT_BYTES = 56 * 1024 * 1024

TOKEN_TILE = 512
VALUE_CHUNKS = 3
KV_GROUP = 4
STEP_GROUPS = (8, 4, 2)

F32 = jnp.float32
BF16 = jnp.bfloat16
NT_DIMS = (((1,), (1,)), ((), ()))


def _params():
    return pltpu.CompilerParams(
        dimension_semantics=("arbitrary", "arbitrary"), vmem_limit_bytes=VMEM_LIMIT_BYTES)


def _rms_rows(x, g):
    ms = jnp.mean(x * x, axis=-1, keepdims=True)
    return x * lax.rsqrt(ms + EPS) * g


def _head_rms_t(h, gain_col):
    ms = jnp.mean(h * h, axis=0, keepdims=True)
    return h * lax.rsqrt(ms + EPS) * gain_col


def _rope_t(h, cos, sin):
    h1, h2 = h[:HALF_DIM], h[HALF_DIM:]
    return jnp.concatenate([h1 * cos - h2 * sin, h2 * cos + h1 * sin], axis=0)


def _split_bf16(x):
    hi = x.astype(BF16)
    return hi, (x - hi.astype(F32)).astype(BF16)


def _pair_operand(q_bf, slot):
    z = jnp.zeros_like(q_bf)
    return jnp.concatenate([q_bf, z] if slot == 0 else [z, q_bf], axis=0)


def _mem_attention_t(qm_t, mq_gain, mk_ref, mvt_ref):
    scores = []
    for h in range(MEM_HEADS):
        pair, slot = divmod(h, 2)
        q = _head_rms_t(qm_t[h * HEAD_DIM:(h + 1) * HEAD_DIM], mq_gain)
        q2 = _pair_operand((q * (SM_SCALE * LOG2_E)).astype(BF16), slot)
        k_pair = mk_ref[0, :, pair * LANES:(pair + 1) * LANES]
        scores.append(jnp.dot(k_pair, q2, preferred_element_type=F32))
    probs = [jnp.exp2(s - jnp.max(s, axis=0, keepdims=True)).astype(BF16) for s in scores]
    outs = []
    for h in range(MEM_HEADS):
        v_t = mvt_ref[0, h * HEAD_DIM:(h + 1) * HEAD_DIM, :]
        ones = jnp.ones((SUM_ROWS, v_t.shape[1]), BF16)
        r = jnp.dot(jnp.concatenate([v_t, ones], axis=0), probs[h], preferred_element_type=F32)
        outs.append(r[:HEAD_DIM] / r[HEAD_DIM:HEAD_DIM + 1])
    return jnp.concatenate(outs, axis=0).T


def _mem_kv_kernel(mem_ref, g_ref, wkv_t_ref, kg_ref, mk_ref, mvt_ref):
    mem_n = _rms_rows(mem_ref[0], g_ref[...]).astype(BF16)
    kv_t = lax.dot_general(wkv_t_ref[...], mem_n, NT_DIMS, preferred_element_type=F32)
    k_heads = [_head_rms_t(kv_t[h * HEAD_DIM:(h + 1) * HEAD_DIM], kg_ref[...]) for h in range(MEM_HEADS)]
    mk_ref[0] = jnp.concatenate(k_heads, axis=0).T.astype(BF16)
    mvt_ref[0] = kv_t[MEM_WIDTH:].astype(BF16)


def _mem_kv(mem, g, wkv_t, k_gain):
    B, M, D = mem.shape
    return pl.pallas_call(
        _mem_kv_kernel,
        grid=(B, 1),
        in_specs=[
            pl.BlockSpec((1, M, D), lambda b, _: (b, 0, 0)),
            pl.BlockSpec((1, D), lambda b, _: (0, 0)),
            pl.BlockSpec((2 * MEM_WIDTH, D), lambda b, _: (0, 0)),
            pl.BlockSpec((HEAD_DIM, 1), lambda b, _: (0, 0)),
        ],
        out_specs=[
            pl.BlockSpec((1, M, MEM_WIDTH), lambda b, _: (b, 0, 0)),
            pl.BlockSpec((1, MEM_WIDTH, M), lambda b, _: (b, 0, 0)),
        ],
        out_shape=[
            jax.ShapeDtypeStruct((B, M, MEM_WIDTH), BF16),
            jax.ShapeDtypeStruct((B, MEM_WIDTH, M), BF16),
        ],
        compiler_params=_params(),
        name="mem_kv",
    )(mem, g, wkv_t, k_gain)


def _pool_in_kernel(x_ref, g_ref, w_tok_ref, wm_t_ref, wg_ref, scale_ref, mqg_ref, mk_ref, mvt_ref,
                    tok_ref, mo_ref, halo_ref):
    t = pl.program_id(1)
    T = x_ref.shape[1]

    @pl.when(t == 0)
    def _():
        halo_ref[...] = jnp.zeros_like(halo_ref)

    u = _rms_rows(x_ref[0], g_ref[...]).astype(BF16)
    h = jnp.dot(u, w_tok_ref[...], preferred_element_type=F32)
    qm_t = lax.dot_general(wm_t_ref[...], u, NT_DIMS, preferred_element_type=F32)

    ext = jnp.concatenate([halo_ref[...], h], axis=0)
    halo_ref[...] = h[T - POOL_HALO:]
    s2 = ext + pltpu.roll(ext, 1, 0)
    s4 = s2 + pltpu.roll(s2, 2, 0)
    s8 = s4 + pltpu.roll(s4, 4, 0)
    s16 = s8 + pltpu.roll(s8, 8, 0)
    lane = lax.broadcasted_iota(jnp.int32, (T, TOK_WIDTH), 1)
    pos = t * T + lax.broadcasted_iota(jnp.int32, (T, TOK_WIDTH), 0)
    sums = (s2, s4, s8, s16)
    win_sum = sums[-1][POOL_HALO:]
    window = jnp.full((T, TOK_WIDTH), POOL_WINDOWS[-1], jnp.int32)
    for gi in range(len(POOL_WINDOWS) - 2, -1, -1):
        in_group = lane < (gi + 1) * POOL_GROUP_WIDTH
        win_sum = jnp.where(in_group, sums[gi][POOL_HALO:], win_sum)
        window = jnp.where(in_group, POOL_WINDOWS[gi], window)
    count = jnp.minimum(pos + 1, window).astype(F32)
    d = (win_sum / count - h).astype(BF16)
    y = jnp.dot(d, wg_ref[...], preferred_element_type=F32) * scale_ref[...]
    tok_ref[0] = y.astype(BF16)

    mo_ref[0] = _mem_attention_t(qm_t, mqg_ref[...], mk_ref, mvt_ref).astype(BF16)


def _pool_in(x, g, w_tok, wm_t, wg, scale, mq_gain, mk, mvt):
    B, S, D = x.shape
    M = mk.shape[1]
    T = min(TOKEN_TILE, S)
    const = lambda b, t: (0, 0)
    return pl.pallas_call(
        _pool_in_kernel,
        grid=(B, S // T),
        in_specs=[
            pl.BlockSpec((1, T, D), lambda b, t: (b, t, 0)),
            pl.BlockSpec((1, D), const),
            pl.BlockSpec((D, TOK_WIDTH), const),
            pl.BlockSpec((MEM_WIDTH, D), const),
            pl.BlockSpec((TOK_WIDTH, TOK_WIDTH), const),
            pl.BlockSpec((1, TOK_WIDTH), const),
            pl.BlockSpec((HEAD_DIM, 1), const),
            pl.BlockSpec((1, M, MEM_WIDTH), lambda b, t: (b, 0, 0)),
            pl.BlockSpec((1, MEM_WIDTH, M), lambda b, t: (b, 0, 0)),
        ],
        out_specs=[
            pl.BlockSpec((1, T, TOK_WIDTH), lambda b, t: (b, t, 0)),
            pl.BlockSpec((1, T, MEM_WIDTH), lambda b, t: (b, t, 0)),
        ],
        out_shape=[
            jax.ShapeDtypeStruct((B, S, TOK_WIDTH), BF16),
            jax.ShapeDtypeStruct((B, S, MEM_WIDTH), BF16),
        ],
        scratch_shapes=[pltpu.VMEM((POOL_HALO, TOK_WIDTH), F32)],
        compiler_params=_params(),
        name="pool_in",
    )(x, g, w_tok, wm_t, wg, scale, mq_gain, mk, mvt)


def _selection_bias(gate, t, T):
    nb, width = gate.shape
    blk = lax.broadcasted_iota(jnp.int32, (nb, width), 0)
    pos = t * T + lax.rem(lax.broadcasted_iota(jnp.int32, (nb, width), 1), T)
    own = lax.shift_right_logical(pos, MOBA_BLOCK.bit_length() - 1)
    g = jnp.where(blk < own, gate, -jnp.inf)
    bias = jnp.full((nb, width), NEG, F32)
    for _ in range(MOBA_TOPK):
        mx = jnp.max(g, axis=0, keepdims=True)
        idx = jnp.min(jnp.where(g == mx, blk, nb), axis=0, keepdims=True)
        idx = jnp.where(mx > -jnp.inf, idx, -1)
        pick = blk == idx
        bias = jnp.where(pick, 0.0, bias)
        g = jnp.where(pick, -jnp.inf, g)
    return bias


def _moba_in_kernel(x_ref, g_ref, wqkv_t_ref, wm_t_ref, cos_ref, sin_ref, qg_ref, kg_ref, mqg_ref,
                    mk_ref, mvt_ref, qt_ref, k_ref, vt_ref, bias_ref, mo_ref, kmean_ref):
    t = pl.program_id(1)
    T = x_ref.shape[1]
    blocks_per_tile = T // MOBA_BLOCK

    @pl.when(t == 0)
    def _():
        kmean_ref[...] = jnp.zeros_like(kmean_ref)

    u = _rms_rows(x_ref[0], g_ref[...]).astype(BF16)
    cos, sin = cos_ref[...], sin_ref[...]

    def proj_t(lo, hi):
        return lax.dot_general(wqkv_t_ref[lo:hi, :], u, NT_DIMS, preferred_element_type=F32)

    k_t = proj_t(TOK_WIDTH, 2 * TOK_WIDTH)
    q_t = proj_t(0, TOK_WIDTH)

    k_heads = [_rope_t(_head_rms_t(k_t[h * HEAD_DIM:(h + 1) * HEAD_DIM], kg_ref[...]), cos, sin)
               for h in range(MOBA_HEADS)]
    k_nat = jnp.concatenate(k_heads, axis=0).T
    k_ref[0] = k_nat.astype(BF16)
    for c in range(blocks_per_tile):
        kmean_ref[pl.ds(t * blocks_per_tile + c, 1), :] = jnp.mean(
            k_nat[c * MOBA_BLOCK:(c + 1) * MOBA_BLOCK], axis=0, keepdims=True)

    km_hi, km_lo = _split_bf16(kmean_ref[...])
    heads_per_chunk = MOBA_HEADS // VALUE_CHUNKS
    chunk_rows = TOK_WIDTH // VALUE_CHUNKS
    for chunk in range(VALUE_CHUNKS):
        lo = 2 * TOK_WIDTH + chunk * chunk_rows
        v_t = proj_t(lo, lo + chunk_rows)
        for c in range(blocks_per_tile):
            vt_ref[0, c, chunk * chunk_rows:(chunk + 1) * chunk_rows, :] = (
                v_t[:, c * MOBA_BLOCK:(c + 1) * MOBA_BLOCK].astype(BF16))
        heads = range(chunk * heads_per_chunk, (chunk + 1) * heads_per_chunk)
        gates = []
        for h in heads:
            rows = slice(h * HEAD_DIM, (h + 1) * HEAD_DIM)
            pair, slot = divmod(h, 2)
            lanes = slice(pair * LANES, (pair + 1) * LANES)
            q = _rope_t(_head_rms_t(q_t[rows], qg_ref[...]), cos, sin)
            qt_ref[0, rows, :] = (q * (SM_SCALE * LOG2_E)).astype(BF16)
            q_hi, q_lo = _split_bf16(q)
            q_hi2, q_lo2 = _pair_operand(q_hi, slot), _pair_operand(q_lo, slot)
            gates.append(jnp.dot(jnp.concatenate([km_hi[:, lanes], km_lo[:, lanes]], axis=1),
                                 jnp.concatenate([q_hi2, q_hi2], axis=0), preferred_element_type=F32)
                         + jnp.dot(km_hi[:, lanes], q_lo2, preferred_element_type=F32))
        bias = _selection_bias(jnp.concatenate(gates, axis=1), t, T)
        for n, h in enumerate(heads):
            bias_ref[0, h] = bias[:, n * T:(n + 1) * T]

    qm_t = lax.dot_general(wm_t_ref[...], u, NT_DIMS, preferred_element_type=F32)
    mo_ref[0] = _mem_attention_t(qm_t, mqg_ref[...], mk_ref, mvt_ref).astype(BF16)


def _moba_in(x, g, wqkv_t, wm_t, cos_t, sin_t, q_gain, k_gain, mq_gain, mk, mvt):
    B, S, D = x.shape
    M = mk.shape[1]
    T = min(TOKEN_TILE, S)
    nb = S // MOBA_BLOCK
    const = lambda b, t: (0, 0)
    return pl.pallas_call(
        _moba_in_kernel,
        grid=(B, S // T),
        in_specs=[
            pl.BlockSpec((1, T, D), lambda b, t: (b, t, 0)),
            pl.BlockSpec((1, D), const),
            pl.BlockSpec((3 * TOK_WIDTH, D), const),
            pl.BlockSpec((MEM_WIDTH, D), const),
            pl.BlockSpec((HALF_DIM, T), lambda b, t: (0, t)),
            pl.BlockSpec((HALF_DIM, T), lambda b, t: (0, t)),
            pl.BlockSpec((HEAD_DIM, 1), const),
            pl.BlockSpec((HEAD_DIM, 1), const),
            pl.BlockSpec((HEAD_DIM, 1), const),
            pl.BlockSpec((1, M, MEM_WIDTH), lambda b, t: (b, 0, 0)),
            pl.BlockSpec((1, MEM_WIDTH, M), lambda b, t: (b, 0, 0)),
        ],
        out_specs=[
            pl.BlockSpec((1, TOK_WIDTH, T), lambda b, t: (b, 0, t)),
            pl.BlockSpec((1, T, TOK_WIDTH), lambda b, t: (b, t, 0)),
            pl.BlockSpec((1, T // MOBA_BLOCK, TOK_WIDTH, MOBA_BLOCK), lambda b, t: (b, t, 0, 0)),
            pl.BlockSpec((1, MOBA_HEADS, nb, T), lambda b, t: (b, 0, 0, t)),
            pl.BlockSpec((1, T, MEM_WIDTH), lambda b, t: (b, t, 0)),
        ],
        out_shape=[
            jax.ShapeDtypeStruct((B, TOK_WIDTH, S), BF16),
            jax.ShapeDtypeStruct((B, S, TOK_WIDTH), BF16),
            jax.ShapeDtypeStruct((B, nb, TOK_WIDTH, MOBA_BLOCK), BF16),
            jax.ShapeDtypeStruct((B, MOBA_HEADS, nb, S), F32),
            jax.ShapeDtypeStruct((B, S, MEM_WIDTH), BF16),
        ],
        scratch_shapes=[pltpu.VMEM((nb, TOK_WIDTH), F32)],
        compiler_params=_params(),
        name="moba_in",
    )(x, g, wqkv_t, wm_t, cos_t, sin_t, q_gain, k_gain, mq_gain, mk, mvt)


def _moba_attn_kernel(qt_ref, qt_next_ref, k_ref, vt_ref, bias_ref, bias_next_ref, onehot_ref, o_ref, *scratch):
    i = pl.program_id(2)
    TQ = qt_ref.shape[2]
    s_refs = (scratch[0:2], scratch[2:4])
    bm_refs = (scratch[4:6], scratch[6:8])
    own_m_refs, own_acc_refs = scratch[8:10], scratch[10:12]
    group_keys = KV_GROUP * MOBA_BLOCK
    slab_rows = 2 * KV_GROUP
    live_groups = (i + KV_GROUP - 1) // KV_GROUP

    def head_operands(q_pair):
        row = lax.broadcasted_iota(jnp.int32, q_pair.shape, 0)
        zero = jnp.zeros_like(q_pair)
        return jnp.where(row < HEAD_DIM, q_pair, zero), jnp.where(row >= HEAD_DIM, q_pair, zero)

    q_heads = head_operands(qt_ref[0])
    q_heads_next = head_operands(qt_next_ref[0])
    zeros_tail = jnp.zeros((MXU_DEPTH - LANES - BIAS_ROWS, TQ), BF16)
    ones_rows = jnp.ones((SUM_ROWS, 1), BF16)

    def score_unit(group, parity, q_head, slab, s_ref, bm_ref):
        k_g = k_ref[0, pl.ds(pl.multiple_of(group * group_keys, group_keys), group_keys), :]
        lhs = jnp.concatenate([k_g, onehot_ref[parity]], axis=1)
        slab = jnp.concatenate([slab, jnp.zeros_like(slab)], axis=0).astype(BF16)
        rhs = jnp.concatenate([q_head, slab, zeros_tail], axis=0)
        s = jnp.dot(lhs, rhs, preferred_element_type=F32)
        s_ref[...] = s
        bm_ref[...] = jnp.max(s, axis=0, keepdims=True)

    def bias_slab(a, group):
        slab_row = pl.multiple_of((group // 2) * slab_rows, slab_rows)
        return bias_ref[0, a, pl.ds(slab_row, slab_rows), :]

    def produce(g, parity, a):
        score_unit(g, parity, q_heads[a], bias_slab(a, g), s_refs[parity][a], bm_refs[parity][a])

    def produce_or_prefetch(g, a):
        live = g < live_groups
        group = jnp.where(live, g, 0)
        q_head = jnp.where(live, q_heads[a], q_heads_next[a])
        slab = jnp.where(live, bias_slab(a, group), bias_next_ref[0, a])
        score_unit(group, 0, q_head, slab, s_refs[0][a], bm_refs[0][a])

    def pv_operand(v_t):
        return jnp.concatenate([v_t, jnp.broadcast_to(ones_rows, (SUM_ROWS, v_t.shape[1]))], axis=0)

    def consume(g, parity, a, state):
        j0 = g * KV_GROUP
        rows = slice(a * HEAD_DIM, (a + 1) * HEAD_DIM)
        v_g = jnp.concatenate([vt_ref[0, j0 + c, rows, :] for c in range(KV_GROUP)], axis=1)
        m, acc = state
        m_new = jnp.maximum(m, bm_refs[parity][a][...])
        alpha = jnp.exp2(m - m_new)
        p = jnp.exp2(s_refs[parity][a][...] - m_new).astype(BF16)
        pv = jnp.dot(pv_operand(v_g), p, preferred_element_type=F32)
        return m_new, alpha * acc + pv

    key_idx = lax.broadcasted_iota(jnp.int32, (MOBA_BLOCK, TQ), 0)
    qry_idx = lax.broadcasted_iota(jnp.int32, (MOBA_BLOCK, TQ), 1)
    causal = key_idx <= qry_idx

    def own_block(block, heads):
        k_own = k_ref[0, pl.ds(pl.multiple_of(block * MOBA_BLOCK, MOBA_BLOCK), MOBA_BLOCK), :]
        v_own = vt_ref[0, block]
        s_own = [jnp.where(causal, jnp.dot(k_own, heads[a], preferred_element_type=F32), NEG)
                 for a in range(2)]
        for a in range(2):
            m = jnp.max(s_own[a], axis=0, keepdims=True)
            p = jnp.exp2(s_own[a] - m).astype(BF16)
            own_m_refs[a][...] = m
            own_acc_refs[a][...] = jnp.dot(
                pv_operand(v_own[a * HEAD_DIM:(a + 1) * HEAD_DIM]), p, preferred_element_type=F32)

    @pl.when(i == 0)
    def _():
        for a in range(2):
            produce_or_prefetch(0, a)
        own_block(i, q_heads)

    init = [(own_m_refs[a][...], own_acc_refs[a][...]) for a in range(2)]

    def run_steps(groups_per_step, first_group, n_steps, carry):
        def body(step, carry):
            g = first_group + groups_per_step * step
            state = list(carry)
            for sub in range(groups_per_step):
                cur = sub % 2
                for a in range(2):
                    if sub == groups_per_step - 1:
                        produce_or_prefetch(g + sub + 1, a)
                    else:
                        produce(g + sub + 1, 1 - cur, a)
                    state[a] = consume(g + sub, cur, a, state[a])
            return tuple(state)
        return lax.fori_loop(0, n_steps, body, carry)

    carry, done = tuple(init), 0
    for size in STEP_GROUPS:
        n_steps = (live_groups - done) // size
        carry = run_steps(size, done, n_steps, carry)
        done = done + n_steps * size

    def last_group(_, carry):
        state = list(carry)
        for a in range(2):
            state[a] = consume(done, 0, a, state[a])
            produce_or_prefetch(done + 1, a)
        return tuple(state)

    final = lax.fori_loop(0, live_groups - done, last_group, carry)
    own_block(jnp.minimum(i + 1, pl.num_programs(2) - 1), q_heads_next)
    o_t = jnp.concatenate([acc[:HEAD_DIM] / acc[HEAD_DIM:HEAD_DIM + 1] for (_, acc) in final], axis=0)
    o_ref[0] = o_t.T.astype(BF16)


def _block_onehot():
    key_block = jnp.arange(KV_GROUP * MOBA_BLOCK, dtype=jnp.int32) // MOBA_BLOCK
    col = jnp.arange(LANES, dtype=jnp.int32)
    parity = jnp.arange(2, dtype=jnp.int32)
    hit = col[None, None, :] == (parity[:, None, None] * KV_GROUP + key_block[None, :, None])
    return hit.astype(BF16)


def _moba_attn(qt, k, vt, bias):
    B, _, S = qt.shape
    nb = S // MOBA_BLOCK
    assert 2 * KV_GROUP == F32_SUBLANES and nb % F32_SUBLANES == 0, "bias slabs are whole f32 sublane tiles"
    assert STEP_GROUPS[-1] == 2 and all(size % 2 == 0 for size in STEP_GROUPS), "score buffers alternate by group parity"
    pairs = MOBA_HEADS // 2
    group_keys = KV_GROUP * MOBA_BLOCK
    return pl.pallas_call(
        _moba_attn_kernel,
        grid=(B, pairs, nb),
        in_specs=[
            pl.BlockSpec((1, LANES, MOBA_BLOCK), lambda b, p, i: (b, p, i)),
            pl.BlockSpec((1, LANES, MOBA_BLOCK), lambda b, p, i: (b, p, jnp.minimum(i + 1, nb - 1))),
            pl.BlockSpec((1, S, LANES), lambda b, p, i: (b, 0, p)),
            pl.BlockSpec((1, nb, LANES, MOBA_BLOCK), lambda b, p, i: (b, 0, p, 0)),
            pl.BlockSpec((1, 2, nb, MOBA_BLOCK), lambda b, p, i: (b, p, 0, i)),
            pl.BlockSpec((1, 2, F32_SUBLANES, MOBA_BLOCK), lambda b, p, i: (b, p, 0, jnp.minimum(i + 1, nb - 1))),
            pl.BlockSpec((2, group_keys, LANES), lambda b, p, i: (0, 0, 0)),
        ],
        out_specs=pl.BlockSpec((1, MOBA_BLOCK, LANES), lambda b, p, i: (b, i, p)),
        out_shape=jax.ShapeDtypeStruct((B, S, TOK_WIDTH), BF16),
        scratch_shapes=([pltpu.VMEM((group_keys, MOBA_BLOCK), F32)] * 4 + [pltpu.VMEM((1, MOBA_BLOCK), F32)] * 6
                        + [pltpu.VMEM((HEAD_DIM + SUM_ROWS, MOBA_BLOCK), F32)] * 2),
        compiler_params=pltpu.CompilerParams(
            dimension_semantics=("arbitrary", "arbitrary", "arbitrary"), vmem_limit_bytes=VMEM_LIMIT_BYTES),
        name="moba_attn",
    )(qt, qt, k, vt, bias, bias, _block_onehot())


def _post_kernel(x_ref, tok_ref, mo_ref, wo_ref, g_ref, w1_ref, w2_ref, o_ref):
    mixed = jnp.concatenate([tok_ref[0], mo_ref[0]], axis=-1)
    x1 = x_ref[0] + jnp.dot(mixed, wo_ref[...], preferred_element_type=F32)
    u = _rms_rows(x1, g_ref[...]).astype(BF16)
    h = jnp.dot(u, w1_ref[...], preferred_element_type=F32)
    h = jnp.square(jnp.maximum(h, 0.0)).astype(BF16)
    o_ref[0] = x1 + jnp.dot(h, w2_ref[...], preferred_element_type=F32)


def _post(x, tok, mo, wo, g, w1, w2):
    B, S, D = x.shape
    T = min(TOKEN_TILE, S)
    const = lambda b, t: (0, 0)
    resident = functools.partial(pl.BlockSpec, index_map=const, pipeline_mode=pl.Buffered(1))
    return pl.pallas_call(
        _post_kernel,
        grid=(B, S // T),
        in_specs=[
            pl.BlockSpec((1, T, D), lambda b, t: (b, t, 0)),
            pl.BlockSpec((1, T, TOK_WIDTH), lambda b, t: (b, t, 0)),
            pl.BlockSpec((1, T, MEM_WIDTH), lambda b, t: (b, t, 0)),
            resident((D, D)),
            pl.BlockSpec((1, D), const),
            resident((D, D_FF)),
            resident((D_FF, D)),
        ],
        out_specs=pl.BlockSpec((1, T, D), lambda b, t: (b, t, 0)),
        out_shape=jax.ShapeDtypeStruct((B, S, D), F32),
        compiler_params=_params(),
        name="post",
    )(x, tok, mo, wo, g, w1, w2)


def _rope_tables_t(seq_len):
    pos = jnp.arange(seq_len, dtype=F32)
    inv = ROPE_THETA ** (-jnp.arange(0, HEAD_DIM, 2, dtype=F32) / HEAD_DIM)
    ang = pos[:, None] * inv[None, :]
    return jnp.cos(ang).T, jnp.sin(ang).T


def _block_diag(w_group):
    G, C, _ = w_group.shape
    eye = jnp.eye(G, dtype=w_group.dtype)
    return (eye[:, None, :, None] * w_group[:, :, None, :]).reshape(G * C, G * C)


def kernel(x, mem, g_mix, g_mem, g_mlp, w_in_pool, w_pool_group, pool_scale, w_in_moba, moba_q_gain,
           moba_k_gain, w_mem_kv, mem_q_gain, mem_k_gain, w_out, w_ff1, w_ff2):
    depth = g_mix.shape[0]
    S = x.shape[1]
    cos_t, sin_t = _rope_tables_t(S)
    col = lambda v: v.reshape(-1, 1)
    row = lambda v: v.reshape(1, -1)
    for i in range(depth):
        j = i // 2
        mk, mvt = _mem_kv(mem, row(g_mem[i]), w_mem_kv[i].T.astype(BF16), col(mem_k_gain[i]))
        if i % 2 == 0:
            w = w_in_pool[j]
            tok, mo = _pool_in(
                x, row(g_mix[i]), w[:, :TOK_WIDTH].astype(BF16), w[:, TOK_WIDTH:].T.astype(BF16),
                _block_diag(w_pool_group[j]).astype(BF16), row(pool_scale[j]), col(mem_q_gain[i]), mk, mvt)
        else:
            w = w_in_moba[j]
            qt, k, vt, bias, mo = _moba_in(
                x, row(g_mix[i]), w[:, :3 * TOK_WIDTH].T.astype(BF16), w[:, 3 * TOK_WIDTH:].T.astype(BF16),
                cos_t, sin_t, col(moba_q_gain[j]), col(moba_k_gain[j]), col(mem_q_gain[i]), mk, mvt)
            tok = _moba_attn(qt, k, vt, bias)
        x = _post(x, tok, mo, w_out[i].astype(BF16), row(g_mlp[i]), w_ff1[i].astype(BF16), w_ff2[i].astype(BF16))
    return x
```

```python
import functools

import jax
import jax.numpy as jnp
from jax import lax
from jax.experimental import pallas as pl
from jax.experimental.pallas import tpu as pltpu

D_MODEL = 1024
HEAD_DIM = 64
HALF_DIM = HEAD_DIM // 2
MEM_HEADS = 4
MEM_WIDTH = MEM_HEADS * HEAD_DIM
TOK_WIDTH = D_MODEL - MEM_WIDTH
MOBA_HEADS = TOK_WIDTH // HEAD_DIM
MOBA_BLOCK = 256
MOBA_TOPK = 3
POOL_WINDOWS = (2, 4, 8, 16)
POOL_GROUP_WIDTH = TOK_WIDTH // len(POOL_WINDOWS)
POOL_HALO = 16
D_FF = 4 * D_MODEL
ROPE_THETA = 10000.0
EPS = 1e-6
NEG = -1e30
SM_SCALE = HEAD_DIM ** -0.5
LOG2_E = 1.4426950408889634

LANES = 128
F32_SUBLANES = 8
BIAS_ROWS = 16
SUM_ROWS = 16
MXU_DEPTH = 256
VMEM_LIMIT_BYTES = 56 * 1024 * 1024

TOKEN_TILE = 512
VALUE_CHUNKS = 3
QUERY_BLOCKS = 1
TILES_PER_STEP = 2
KV_GROUP = 4
STEP_GROUPS = (8, 4, 2)

F32 = jnp.float32
BF16 = jnp.bfloat16
NT_DIMS = (((1,), (1,)), ((), ()))


def _params():
    return pltpu.CompilerParams(
        dimension_semantics=("arbitrary", "arbitrary"), vmem_limit_bytes=VMEM_LIMIT_BYTES)


def _rms_rows(x, g):
    ms = jnp.mean(x * x, axis=-1, keepdims=True)
    return x * lax.rsqrt(ms + EPS) * g


def _head_rms_t(h, gain_col):
    ms = jnp.mean(h * h, axis=0, keepdims=True)
    return h * lax.rsqrt(ms + EPS) * gain_col


def _rope_t(h, cos, sin):
    h1, h2 = h[:HALF_DIM], h[HALF_DIM:]
    return jnp.concatenate([h1 * cos - h2 * sin, h2 * cos + h1 * sin], axis=0)


def _split_bf16(x):
    hi = x.astype(BF16)
    return hi, (x - hi.astype(F32)).astype(BF16)


def _pair_operand(q_bf, slot):
    z = jnp.zeros_like(q_bf)
    return jnp.concatenate([q_bf, z] if slot == 0 else [z, q_bf], axis=0)


def _mem_attention_t(qm_t, mq_gain, mk_ref, mvt_ref):
    scores = []
    for h in range(MEM_HEADS):
        pair, slot = divmod(h, 2)
        q = _head_rms_t(qm_t[h * HEAD_DIM:(h + 1) * HEAD_DIM], mq_gain)
        q2 = _pair_operand((q * (SM_SCALE * LOG2_E)).astype(BF16), slot)
        k_pair = mk_ref[0, :, pair * LANES:(pair + 1) * LANES]
        scores.append(jnp.dot(k_pair, q2, preferred_element_type=F32))
    probs = [jnp.exp2(s - jnp.max(s, axis=0, keepdims=True)).astype(BF16) for s in scores]
    outs = []
    for h in range(MEM_HEADS):
        v_t = mvt_ref[0, h * HEAD_DIM:(h + 1) * HEAD_DIM, :]
        ones = jnp.ones((SUM_ROWS, v_t.shape[1]), BF16)
        r = jnp.dot(jnp.concatenate([v_t, ones], axis=0), probs[h], preferred_element_type=F32)
        outs.append(r[:HEAD_DIM] / r[HEAD_DIM:HEAD_DIM + 1])
    return jnp.concatenate(outs, axis=0).T


def _mem_kv_kernel(mem_ref, g_ref, wkv_t_ref, kg_ref, mk_ref, mvt_ref):
    mem_n = _rms_rows(mem_ref[0], g_ref[...]).astype(BF16)
    kv_t = lax.dot_general(wkv_t_ref[...], mem_n, NT_DIMS, preferred_element_type=F32)
    k_heads = [_head_rms_t(kv_t[h * HEAD_DIM:(h + 1) * HEAD_DIM], kg_ref[...]) for h in range(MEM_HEADS)]
    mk_ref[0] = jnp.concatenate(k_heads, axis=0).T.astype(BF16)
    mvt_ref[0] = kv_t[MEM_WIDTH:].astype(BF16)


def _mem_kv(mem, g, wkv_t, k_gain):
    B, M, D = mem.shape
    return pl.pallas_call(
        _mem_kv_kernel,
        grid=(B, 1),
        in_specs=[
            pl.BlockSpec((1, M, D), lambda b, _: (b, 0, 0)),
            pl.BlockSpec((1, D), lambda b, _: (0, 0)),
            pl.BlockSpec((2 * MEM_WIDTH, D), lambda b, _: (0, 0)),
            pl.BlockSpec((HEAD_DIM, 1), lambda b, _: (0, 0)),
        ],
        out_specs=[
            pl.BlockSpec((1, M, MEM_WIDTH), lambda b, _: (b, 0, 0)),
            pl.BlockSpec((1, MEM_WIDTH, M), lambda b, _: (b, 0, 0)),
        ],
        out_shape=[
            jax.ShapeDtypeStruct((B, M, MEM_WIDTH), BF16),
            jax.ShapeDtypeStruct((B, MEM_WIDTH, M), BF16),
        ],
        compiler_params=_params(),
        name="mem_kv",
    )(mem, g, wkv_t, k_gain)


def _pool_in_kernel(x_ref, g_ref, w_tok_ref, wm_t_ref, wg_ref, scale_ref, mqg_ref, mk_ref, mvt_ref,
                    tok_ref, mo_ref, halo_ref):
    t = pl.program_id(1)
    T = x_ref.shape[1]

    @pl.when(t == 0)
    def _():
        halo_ref[...] = jnp.zeros_like(halo_ref)

    u = _rms_rows(x_ref[0], g_ref[...]).astype(BF16)
    h = jnp.dot(u, w_tok_ref[...], preferred_element_type=F32)
    qm_t = lax.dot_general(wm_t_ref[...], u, NT_DIMS, preferred_element_type=F32)

    ext = jnp.concatenate([halo_ref[...], h], axis=0)
    halo_ref[...] = h[T - POOL_HALO:]
    s2 = ext + pltpu.roll(ext, 1, 0)
    s4 = s2 + pltpu.roll(s2, 2, 0)
    s8 = s4 + pltpu.roll(s4, 4, 0)
    s16 = s8 + pltpu.roll(s8, 8, 0)
    lane = lax.broadcasted_iota(jnp.int32, (T, TOK_WIDTH), 1)
    pos = t * T + lax.broadcasted_iota(jnp.int32, (T, TOK_WIDTH), 0)
    sums = (s2, s4, s8, s16)
    win_sum = sums[-1][POOL_HALO:]
    window = jnp.full((T, TOK_WIDTH), POOL_WINDOWS[-1], jnp.int32)
    for gi in range(len(POOL_WINDOWS) - 2, -1, -1):
        in_group = lane < (gi + 1) * POOL_GROUP_WIDTH
        win_sum = jnp.where(in_group, sums[gi][POOL_HALO:], win_sum)
        window = jnp.where(in_group, POOL_WINDOWS[gi], window)
    count = jnp.minimum(pos + 1, window).astype(F32)
    d = (win_sum / count - h).astype(BF16)
    y = jnp.dot(d, wg_ref[...], preferred_element_type=F32) * scale_ref[...]
    tok_ref[0] = y.astype(BF16)

    mo_ref[0] = _mem_attention_t(qm_t, mqg_ref[...], mk_ref, mvt_ref).astype(BF16)


def _pool_in(x, g, w_tok, wm_t, wg, scale, mq_gain, mk, mvt):
    B, S, D = x.shape
    M = mk.shape[1]
    T = min(TOKEN_TILE, S)
    const = lambda b, t: (0, 0)
    return pl.pallas_call(
        _pool_in_kernel,
        grid=(B, S // T),
        in_specs=[
            pl.BlockSpec((1, T, D), lambda b, t: (b, t, 0)),
            pl.BlockSpec((1, D), const),
            pl.BlockSpec((D, TOK_WIDTH), const),
            pl.BlockSpec((MEM_WIDTH, D), const),
            pl.BlockSpec((TOK_WIDTH, TOK_WIDTH), const),
            pl.BlockSpec((1, TOK_WIDTH), const),
            pl.BlockSpec((HEAD_DIM, 1), const),
            pl.BlockSpec((1, M, MEM_WIDTH), lambda b, t: (b, 0, 0)),
            pl.BlockSpec((1, MEM_WIDTH, M), lambda b, t: (b, 0, 0)),
        ],
        out_specs=[
            pl.BlockSpec((1, T, TOK_WIDTH), lambda b, t: (b, t, 0)),
            pl.BlockSpec((1, T, MEM_WIDTH), lambda b, t: (b, t, 0)),
        ],
        out_shape=[
            jax.ShapeDtypeStruct((B, S, TOK_WIDTH), BF16),
            jax.ShapeDtypeStruct((B, S, MEM_WIDTH), BF16),
        ],
        scratch_shapes=[pltpu.VMEM((POOL_HALO, TOK_WIDTH), F32)],
        compiler_params=_params(),
        name="pool_in",
    )(x, g, w_tok, wm_t, wg, scale, mq_gain, mk, mvt)


def _selection_bias(gate, t, T):
    nb, width = gate.shape
    blk = lax.broadcasted_iota(jnp.int32, (nb, width), 0)
    pos = t * T + lax.rem(lax.broadcasted_iota(jnp.int32, (nb, width), 1), T)
    own = lax.shift_right_logical(pos, MOBA_BLOCK.bit_length() - 1)
    g = jnp.where(blk < own, gate, -jnp.inf)
    bias = jnp.full((nb, width), NEG, F32)
    for _ in range(MOBA_TOPK):
        mx = jnp.max(g, axis=0, keepdims=True)
        idx = jnp.min(jnp.where(g == mx, blk, nb), axis=0, keepdims=True)
        idx = jnp.where(mx > -jnp.inf, idx, -1)
        pick = blk == idx
        bias = jnp.where(pick, 0.0, bias)
        g = jnp.where(pick, -jnp.inf, g)
    return bias


def _moba_in_kernel(x_ref, g_ref, wqkv_t_ref, wm_t_ref, cos_ref, sin_ref, qg_ref, kg_ref, mqg_ref,
                    mk_ref, mvt_ref, qt_ref, k_ref, vt_ref, bias_ref, mo_ref, kmean_ref):
    t = pl.program_id(1)
    T = x_ref.shape[1]
    blocks_per_tile = T // MOBA_BLOCK

    @pl.when(t == 0)
    def _():
        kmean_ref[...] = jnp.zeros_like(kmean_ref)

    u = _rms_rows(x_ref[0], g_ref[...]).astype(BF16)
    cos, sin = cos_ref[...], sin_ref[...]

    def proj_t(lo, hi):
        return lax.dot_general(wqkv_t_ref[lo:hi, :], u, NT_DIMS, preferred_element_type=F32)

    k_t = proj_t(TOK_WIDTH, 2 * TOK_WIDTH)
    q_t = proj_t(0, TOK_WIDTH)

    k_heads = [_rope_t(_head_rms_t(k_t[h * HEAD_DIM:(h + 1) * HEAD_DIM], kg_ref[...]), cos, sin)
               for h in range(MOBA_HEADS)]
    k_nat = jnp.concatenate(k_heads, axis=0).T
    k_ref[0] = k_nat.astype(BF16)
    for c in range(blocks_per_tile):
        kmean_ref[pl.ds(t * blocks_per_tile + c, 1), :] = jnp.mean(
            k_nat[c * MOBA_BLOCK:(c + 1) * MOBA_BLOCK], axis=0, keepdims=True)

    km_hi, km_lo = _split_bf16(kmean_ref[...])
    heads_per_chunk = MOBA_HEADS // VALUE_CHUNKS
    chunk_rows = TOK_WIDTH // VALUE_CHUNKS
    for chunk in range(VALUE_CHUNKS):
        lo = 2 * TOK_WIDTH + chunk * chunk_rows
        v_t = proj_t(lo, lo + chunk_rows)
        for c in range(blocks_per_tile):
            vt_ref[0, c, chunk * chunk_rows:(chunk + 1) * chunk_rows, :] = (
                v_t[:, c * MOBA_BLOCK:(c + 1) * MOBA_BLOCK].astype(BF16))
        heads = range(chunk * heads_per_chunk, (chunk + 1) * heads_per_chunk)
        gates = []
        for h in heads:
            rows = slice(h * HEAD_DIM, (h + 1) * HEAD_DIM)
            pair, slot = divmod(h, 2)
            lanes = slice(pair * LANES, (pair + 1) * LANES)
            q = _rope_t(_head_rms_t(q_t[rows], qg_ref[...]), cos, sin)
            qt_ref[0, rows, :] = (q * (SM_SCALE * LOG2_E)).astype(BF16)
            q_hi, q_lo = _split_bf16(q)
            q_hi2, q_lo2 = _pair_operand(q_hi, slot), _pair_operand(q_lo, slot)
            gates.append(jnp.dot(jnp.concatenate([km_hi[:, lanes], km_lo[:, lanes]], axis=1),
                                 jnp.concatenate([q_hi2, q_hi2], axis=0), preferred_element_type=F32)
                         + jnp.dot(km_hi[:, lanes], q_lo2, preferred_element_type=F32))
        bias = _selection_bias(jnp.concatenate(gates, axis=1), t, T)
        for n, h in enumerate(heads):
            bias_ref[0, h] = bias[:, n * T:(n + 1) * T]

    qm_t = lax.dot_general(wm_t_ref[...], u, NT_DIMS, preferred_element_type=F32)
    mo_ref[0] = _mem_attention_t(qm_t, mqg_ref[...], mk_ref, mvt_ref).astype(BF16)


def _moba_in(x, g, wqkv_t, wm_t, cos_t, sin_t, q_gain, k_gain, mq_gain, mk, mvt):
    B, S, D = x.shape
    M = mk.shape[1]
    T = min(TOKEN_TILE, S)
    nb = S // MOBA_BLOCK
    const = lambda b, t: (0, 0)
    return pl.pallas_call(
        _moba_in_kernel,
        grid=(B, S // T),
        in_specs=[
            pl.BlockSpec((1, T, D), lambda b, t: (b, t, 0)),
            pl.BlockSpec((1, D), const),
            pl.BlockSpec((3 * TOK_WIDTH, D), const),
            pl.BlockSpec((MEM_WIDTH, D), const),
            pl.BlockSpec((HALF_DIM, T), lambda b, t: (0, t)),
            pl.BlockSpec((HALF_DIM, T), lambda b, t: (0, t)),
            pl.BlockSpec((HEAD_DIM, 1), const),
            pl.BlockSpec((HEAD_DIM, 1), const),
            pl.BlockSpec((HEAD_DIM, 1), const),
            pl.BlockSpec((1, M, MEM_WIDTH), lambda b, t: (b, 0, 0)),
            pl.BlockSpec((1, MEM_WIDTH, M), lambda b, t: (b, 0, 0)),
        ],
        out_specs=[
            pl.BlockSpec((1, TOK_WIDTH, T), lambda b, t: (b, 0, t)),
            pl.BlockSpec((1, T, TOK_WIDTH), lambda b, t: (b, t, 0)),
            pl.BlockSpec((1, T // MOBA_BLOCK, TOK_WIDTH, MOBA_BLOCK), lambda b, t: (b, t, 0, 0)),
            pl.BlockSpec((1, MOBA_HEADS, nb, T), lambda b, t: (b, 0, 0, t)),
            pl.BlockSpec((1, T, MEM_WIDTH), lambda b, t: (b, t, 0)),
        ],
        out_shape=[
            jax.ShapeDtypeStruct((B, TOK_WIDTH, S), BF16),
            jax.ShapeDtypeStruct((B, S, TOK_WIDTH), BF16),
            jax.ShapeDtypeStruct((B, nb, TOK_WIDTH, MOBA_BLOCK), BF16),
            jax.ShapeDtypeStruct((B, MOBA_HEADS, nb, S), F32),
            jax.ShapeDtypeStruct((B, S, MEM_WIDTH), BF16),
        ],
        scratch_shapes=[pltpu.VMEM((nb, TOK_WIDTH), F32)],
        compiler_params=_params(),
        name="moba_in",
    )(x, g, wqkv_t, wm_t, cos_t, sin_t, q_gain, k_gain, mq_gain, mk, mvt)


def _moba_attn_kernel(qt_ref, k_ref, vt_ref, bias_ref, onehot_ref, o_ref, *scratch):
    for n in range(TILES_PER_STEP):
        _moba_attn_tile(pl.program_id(2) * TILES_PER_STEP + n, n, qt_ref, k_ref, vt_ref, bias_ref, onehot_ref,
                        o_ref, scratch)


def _moba_attn_tile(tile, slot, qt_ref, k_ref, vt_ref, bias_ref, onehot_ref, o_ref, scratch):
    TQ = QUERY_BLOCKS * MOBA_BLOCK
    cols = slice(slot * TQ, (slot + 1) * TQ)
    first_block = tile * QUERY_BLOCKS
    s_refs = (scratch[0:2], scratch[2:4])
    bm_refs = (scratch[4:6], scratch[6:8])
    group_keys = KV_GROUP * MOBA_BLOCK
    total_groups = k_ref.shape[1] // group_keys
    past_blocks = first_block + QUERY_BLOCKS - 1
    live_groups = (past_blocks + KV_GROUP - 1) // KV_GROUP

    q_pair = qt_ref[0, :, cols]
    row = lax.broadcasted_iota(jnp.int32, q_pair.shape, 0)
    zero = jnp.zeros_like(q_pair)
    q_heads = (jnp.where(row < HEAD_DIM, q_pair, zero), jnp.where(row >= HEAD_DIM, q_pair, zero))

    zeros_tail = jnp.zeros((MXU_DEPTH - LANES - BIAS_ROWS, TQ), BF16)
    ones_rows = jnp.ones((SUM_ROWS, 1), BF16)

    def produce(g, parity, a, s_ref, bm_ref):
        gc = jnp.minimum(g, total_groups - 1)
        k_g = k_ref[0, pl.ds(pl.multiple_of(gc * group_keys, group_keys), group_keys), :]
        lhs = jnp.concatenate([k_g, onehot_ref[parity]], axis=1)
        penalty = jnp.where(g >= live_groups, NEG, 0.0)
        slab_row = pl.multiple_of((gc // 2) * 2 * KV_GROUP, 2 * KV_GROUP)
        slab = bias_ref[0, a, pl.ds(slab_row, 2 * KV_GROUP), cols] + penalty
        slab = jnp.concatenate([slab, jnp.zeros_like(slab)], axis=0).astype(BF16)
        rhs = jnp.concatenate([q_heads[a], slab, zeros_tail], axis=0)
        s = jnp.dot(lhs, rhs, preferred_element_type=F32)
        s_ref[...] = s
        bm_ref[...] = jnp.max(s, axis=0, keepdims=True)

    def pv_operand(v_t):
        return jnp.concatenate([v_t, jnp.broadcast_to(ones_rows, (SUM_ROWS, v_t.shape[1]))], axis=0)

    def consume(g, a, s_ref, bm_ref, state):
        gc = jnp.minimum(g, total_groups - 1)
        j0 = gc * KV_GROUP
        rows = slice(a * HEAD_DIM, (a + 1) * HEAD_DIM)
        v_g = jnp.concatenate([vt_ref[0, j0 + c, rows, :] for c in range(KV_GROUP)], axis=1)
        m, acc = state
        m_new = jnp.maximum(m, bm_ref[...])
        alpha = jnp.exp2(m - m_new)
        p = jnp.exp2(s_ref[...] - m_new).astype(BF16)
        pv = jnp.dot(pv_operand(v_g), p, preferred_element_type=F32)
        return m_new, alpha * acc + pv

    key_idx = lax.broadcasted_iota(jnp.int32, (MOBA_BLOCK, MOBA_BLOCK), 0)
    qry_idx = lax.broadcasted_iota(jnp.int32, (MOBA_BLOCK, MOBA_BLOCK), 1)
    causal = key_idx <= qry_idx
    own = []
    for qb in range(QUERY_BLOCKS):
        qcols = slice(qb * MOBA_BLOCK, (qb + 1) * MOBA_BLOCK)
        k_own = k_ref[0, pl.ds(pl.multiple_of((first_block + qb) * MOBA_BLOCK, MOBA_BLOCK), MOBA_BLOCK), :]
        own.append([jnp.where(causal, jnp.dot(k_own, q_heads[a][:, qcols], preferred_element_type=F32), NEG)
                    for a in range(2)])
    produce(0, 0, 0, s_refs[0][0], bm_refs[0][0])
    produce(0, 0, 1, s_refs[0][1], bm_refs[0][1])
    init = []
    for a in range(2):
        rows = slice(a * HEAD_DIM, (a + 1) * HEAD_DIM)
        ms, accs = [], []
        for qb in range(QUERY_BLOCKS):
            m = jnp.max(own[qb][a], axis=0, keepdims=True)
            p = jnp.exp2(own[qb][a] - m).astype(BF16)
            ms.append(m)
            accs.append(jnp.dot(pv_operand(vt_ref[0, first_block + qb, rows, :]), p, preferred_element_type=F32))
        init.append((jnp.concatenate(ms, axis=1), jnp.concatenate(accs, axis=1)))

    def run_steps(groups_per_step, first_group, n_steps, carry):
        def body(step, carry):
            g = first_group + groups_per_step * step
            state = list(carry)
            for sub in range(groups_per_step):
                cur = sub % 2
                nxt = 1 - cur
                for a in range(2):
                    produce(g + sub + 1, nxt, a, s_refs[nxt][a], bm_refs[nxt][a])
                    state[a] = consume(g + sub, a, s_refs[cur][a], bm_refs[cur][a], state[a])
            return tuple(state)
        return lax.fori_loop(0, n_steps, body, carry)

    carry, done = tuple(init), 0
    for size in STEP_GROUPS:
        n_steps = (live_groups - done) // size
        carry = run_steps(size, done, n_steps, carry)
        done = done + n_steps * size

    def last_group(_, carry):
        return tuple(consume(done, a, s_refs[0][a], bm_refs[0][a], carry[a]) for a in range(2))

    final = lax.fori_loop(0, live_groups - done, last_group, carry)
    o_t = jnp.concatenate([acc[:HEAD_DIM] / acc[HEAD_DIM:HEAD_DIM + 1] for (_, acc) in final], axis=0)
    o_ref[0, cols, :] = o_t.T.astype(BF16)


def _block_onehot():
    key_block = jnp.arange(KV_GROUP * MOBA_BLOCK, dtype=jnp.int32) // MOBA_BLOCK
    col = jnp.arange(LANES, dtype=jnp.int32)
    parity = jnp.arange(2, dtype=jnp.int32)
    hit = col[None, None, :] == (parity[:, None, None] * KV_GROUP + key_block[None, :, None])
    return hit.astype(BF16)


def _moba_attn(qt, k, vt, bias):
    B, _, S = qt.shape
    nb = S // MOBA_BLOCK
    tq = TILES_PER_STEP * QUERY_BLOCKS * MOBA_BLOCK
    assert nb % (TILES_PER_STEP * QUERY_BLOCKS) == 0
    assert 2 * KV_GROUP == F32_SUBLANES and nb % F32_SUBLANES == 0, "bias slabs are whole f32 sublane tiles"
    assert STEP_GROUPS[-1] == 2 and all(size % 2 == 0 for size in STEP_GROUPS), "score buffers alternate by group parity"
    pairs = MOBA_HEADS // 2
    group_keys = KV_GROUP * MOBA_BLOCK
    return pl.pallas_call(
        _moba_attn_kernel,
        grid=(B, pairs, nb // (TILES_PER_STEP * QUERY_BLOCKS)),
        in_specs=[
            pl.BlockSpec((1, LANES, tq), lambda b, p, i: (b, p, i)),
            pl.BlockSpec((1, S, LANES), lambda b, p, i: (b, 0, p)),
            pl.BlockSpec((1, nb, LANES, MOBA_BLOCK), lambda b, p, i: (b, 0, p, 0)),
            pl.BlockSpec((1, 2, nb, tq), lambda b, p, i: (b, p, 0, i)),
            pl.BlockSpec((2, group_keys, LANES), lambda b, p, i: (0, 0, 0)),
        ],
        out_specs=pl.BlockSpec((1, tq, LANES), lambda b, p, i: (b, i, p)),
        out_shape=jax.ShapeDtypeStruct((B, S, TOK_WIDTH), BF16),
        scratch_shapes=([pltpu.VMEM((group_keys, QUERY_BLOCKS * MOBA_BLOCK), F32)] * 4
                        + [pltpu.VMEM((1, QUERY_BLOCKS * MOBA_BLOCK), F32)] * 4),
        compiler_params=pltpu.CompilerParams(
            dimension_semantics=("arbitrary", "arbitrary", "arbitrary"), vmem_limit_bytes=VMEM_LIMIT_BYTES),
        name="moba_attn",
    )(qt, k, vt, bias, _block_onehot())


def _post_kernel(x_ref, tok_ref, mo_ref, wo_ref, g_ref, w1_ref, w2_ref, o_ref):
    mixed = jnp.concatenate([tok_ref[0], mo_ref[0]], axis=-1)
    x1 = x_ref[0] + jnp.dot(mixed, wo_ref[...], preferred_element_type=F32)
    u = _rms_rows(x1, g_ref[...]).astype(BF16)
    h = jnp.dot(u, w1_ref[...], preferred_element_type=F32)
    h = jnp.square(jnp.maximum(h, 0.0)).astype(BF16)
    o_ref[0] = x1 + jnp.dot(h, w2_ref[...], preferred_element_type=F32)


def _post(x, tok, mo, wo, g, w1, w2):
    B, S, D = x.shape
    T = min(TOKEN_TILE, S)
    const = lambda b, t: (0, 0)
    resident = functools.partial(pl.BlockSpec, index_map=const, pipeline_mode=pl.Buffered(1))
    return pl.pallas_call(
        _post_kernel,
        grid=(B, S // T),
        in_specs=[
            pl.BlockSpec((1, T, D), lambda b, t: (b, t, 0)),
            pl.BlockSpec((1, T, TOK_WIDTH), lambda b, t: (b, t, 0)),
            pl.BlockSpec((1, T, MEM_WIDTH), lambda b, t: (b, t, 0)),
            resident((D, D)),
            pl.BlockSpec((1, D), const),
            resident((D, D_FF)),
            resident((D_FF, D)),
        ],
        out_specs=pl.BlockSpec((1, T, D), lambda b, t: (b, t, 0)),
        out_shape=jax.ShapeDtypeStruct((B, S, D), F32),
        compiler_params=_params(),
        name="post",
    )(x, tok, mo, wo, g, w1, w2)


def _rope_tables_t(seq_len):
    pos = jnp.arange(seq_len, dtype=F32)
    inv = ROPE_THETA ** (-jnp.arange(0, HEAD_DIM, 2, dtype=F32) / HEAD_DIM)
    ang = pos[:, None] * inv[None, :]
    return jnp.cos(ang).T, jnp.sin(ang).T


def _block_diag(w_group):
    G, C, _ = w_group.shape
    eye = jnp.eye(G, dtype=w_group.dtype)
    return (eye[:, None, :, None] * w_group[:, :, None, :]).reshape(G * C, G * C)


def kernel(x, mem, g_mix, g_mem, g_mlp, w_in_pool, w_pool_group, pool_scale, w_in_moba, moba_q_gain,
           moba_k_gain, w_mem_kv, mem_q_gain, mem_k_gain, w_out, w_ff1, w_ff2):
    depth = g_mix.shape[0]
    S = x.shape[1]
    cos_t, sin_t = _rope_tables_t(S)
    col = lambda v: v.reshape(-1, 1)
    row = lambda v: v.reshape(1, -1)
    for i in range(depth):
        j = i // 2
        mk, mvt = _mem_kv(mem, row(g_mem[i]), w_mem_kv[i].T.astype(BF16), col(mem_k_gain[i]))
        if i % 2 == 0:
            w = w_in_pool[j]
            tok, mo = _pool_in(
                x, row(g_mix[i]), w[:, :TOK_WIDTH].astype(BF16), w[:, TOK_WIDTH:].T.astype(BF16),
                _block_diag(w_pool_group[j]).astype(BF16), row(pool_scale[j]), col(mem_q_gain[i]), mk, mvt)
        else:
            w = w_in_moba[j]
            qt, k, vt, bias, mo = _moba_in(
                x, row(g_mix[i]), w[:, :3 * TOK_WIDTH].T.astype(BF16), w[:, 3 * TOK_WIDTH:].T.astype(BF16),
                cos_t, sin_t, col(moba_q_gain[j]), col(moba_k_gain[j]), col(mem_q_gain[i]), mk, mvt)
            tok = _moba_attn(qt, k, vt, bias)
        x = _post(x, tok, mo, w_out[i].astype(BF16), row(g_mlp[i]), w_ff1[i].astype(BF16), w_ff2[i].astype(BF16))
    return x
```

```python
import functools

import jax
import jax.numpy as jnp
from jax import lax
from jax.experimental import pallas as pl
from jax.experimental.pallas import tpu as pltpu

D_MODEL = 1024
HEAD_DIM = 64
HALF_DIM = HEAD_DIM // 2
MEM_HEADS = 4
MEM_WIDTH = MEM_HEADS * HEAD_DIM
TOK_WIDTH = D_MODEL - MEM_WIDTH
MOBA_HEADS = TOK_WIDTH // HEAD_DIM
MOBA_BLOCK = 256
MOBA_TOPK = 3
POOL_WINDOWS = (2, 4, 8, 16)
POOL_GROUP_WIDTH = TOK_WIDTH // len(POOL_WINDOWS)
POOL_HALO = 16
D_FF = 4 * D_MODEL
ROPE_THETA = 10000.0
EPS = 1e-6
NEG = -1e30
SM_SCALE = HEAD_DIM ** -0.5
LOG2_E = 1.4426950408889634

LANES = 128
F32_SUBLANES = 8
BIAS_ROWS = 16
SUM_ROWS = 16
MXU_DEPTH = 256
VMEM_LIMIT_BYTES = 56 * 1024 * 1024

TOKEN_TILE = 512
VALUE_CHUNKS = 3
QUERY_BLOCKS = 1
TILES_PER_STEP = 4
KV_GROUP = 4
STEP_GROUPS = (8, 4, 2)

F32 = jnp.float32
BF16 = jnp.bfloat16
NT_DIMS = (((1,), (1,)), ((), ()))


def _params():
    return pltpu.CompilerParams(
        dimension_semantics=("arbitrary", "arbitrary"), vmem_limit_bytes=VMEM_LIMIT_BYTES)


def _rms_rows(x, g):
    ms = jnp.mean(x * x, axis=-1, keepdims=True)
    return x * lax.rsqrt(ms + EPS) * g


def _head_rms_t(h, gain_col):
    ms = jnp.mean(h * h, axis=0, keepdims=True)
    return h * lax.rsqrt(ms + EPS) * gain_col


def _rope_t(h, cos, sin):
    h1, h2 = h[:HALF_DIM], h[HALF_DIM:]
    return jnp.concatenate([h1 * cos - h2 * sin, h2 * cos + h1 * sin], axis=0)


def _split_bf16(x):
    hi = x.astype(BF16)
    return hi, (x - hi.astype(F32)).astype(BF16)


def _pair_operand(q_bf, slot):
    z = jnp.zeros_like(q_bf)
    return jnp.concatenate([q_bf, z] if slot == 0 else [z, q_bf], axis=0)


def _mem_attention_t(qm_t, mq_gain, mk_ref, mvt_ref):
    scores = []
    for h in range(MEM_HEADS):
        pair, slot = divmod(h, 2)
        q = _head_rms_t(qm_t[h * HEAD_DIM:(h + 1) * HEAD_DIM], mq_gain)
        q2 = _pair_operand((q * (SM_SCALE * LOG2_E)).astype(BF16), slot)
        k_pair = mk_ref[0, :, pair * LANES:(pair + 1) * LANES]
        scores.append(jnp.dot(k_pair, q2, preferred_element_type=F32))
    probs = [jnp.exp2(s - jnp.max(s, axis=0, keepdims=True)).astype(BF16) for s in scores]
    outs = []
    for h in range(MEM_HEADS):
        v_t = mvt_ref[0, h * HEAD_DIM:(h + 1) * HEAD_DIM, :]
        ones = jnp.ones((SUM_ROWS, v_t.shape[1]), BF16)
        r = jnp.dot(jnp.concatenate([v_t, ones], axis=0), probs[h], preferred_element_type=F32)
        outs.append(r[:HEAD_DIM] / r[HEAD_DIM:HEAD_DIM + 1])
    return jnp.concatenate(outs, axis=0).T


def _mem_kv_kernel(mem_ref, g_ref, wkv_t_ref, kg_ref, mk_ref, mvt_ref):
    mem_n = _rms_rows(mem_ref[0], g_ref[...]).astype(BF16)
    kv_t = lax.dot_general(wkv_t_ref[...], mem_n, NT_DIMS, preferred_element_type=F32)
    k_heads = [_head_rms_t(kv_t[h * HEAD_DIM:(h + 1) * HEAD_DIM], kg_ref[...]) for h in range(MEM_HEADS)]
    mk_ref[0] = jnp.concatenate(k_heads, axis=0).T.astype(BF16)
    mvt_ref[0] = kv_t[MEM_WIDTH:].astype(BF16)


def _mem_kv(mem, g, wkv_t, k_gain):
    B, M, D = mem.shape
    return pl.pallas_call(
        _mem_kv_kernel,
        grid=(B, 1),
        in_specs=[
            pl.BlockSpec((1, M, D), lambda b, _: (b, 0, 0)),
            pl.BlockSpec((1, D), lambda b, _: (0, 0)),
            pl.BlockSpec((2 * MEM_WIDTH, D), lambda b, _: (0, 0)),
            pl.BlockSpec((HEAD_DIM, 1), lambda b, _: (0, 0)),
        ],
        out_specs=[
            pl.BlockSpec((1, M, MEM_WIDTH), lambda b, _: (b, 0, 0)),
            pl.BlockSpec((1, MEM_WIDTH, M), lambda b, _: (b, 0, 0)),
        ],
        out_shape=[
            jax.ShapeDtypeStruct((B, M, MEM_WIDTH), BF16),
            jax.ShapeDtypeStruct((B, MEM_WIDTH, M), BF16),
        ],
        compiler_params=_params(),
        name="mem_kv",
    )(mem, g, wkv_t, k_gain)


def _pool_in_kernel(x_ref, g_ref, w_tok_ref, wm_t_ref, wg_ref, scale_ref, mqg_ref, mk_ref, mvt_ref,
                    tok_ref, mo_ref, halo_ref):
    t = pl.program_id(1)
    T = x_ref.shape[1]

    @pl.when(t == 0)
    def _():
        halo_ref[...] = jnp.zeros_like(halo_ref)

    u = _rms_rows(x_ref[0], g_ref[...]).astype(BF16)
    h = jnp.dot(u, w_tok_ref[...], preferred_element_type=F32)
    qm_t = lax.dot_general(wm_t_ref[...], u, NT_DIMS, preferred_element_type=F32)

    ext = jnp.concatenate([halo_ref[...], h], axis=0)
    halo_ref[...] = h[T - POOL_HALO:]
    s2 = ext + pltpu.roll(ext, 1, 0)
    s4 = s2 + pltpu.roll(s2, 2, 0)
    s8 = s4 + pltpu.roll(s4, 4, 0)
    s16 = s8 + pltpu.roll(s8, 8, 0)
    lane = lax.broadcasted_iota(jnp.int32, (T, TOK_WIDTH), 1)
    pos = t * T + lax.broadcasted_iota(jnp.int32, (T, TOK_WIDTH), 0)
    sums = (s2, s4, s8, s16)
    win_sum = sums[-1][POOL_HALO:]
    window = jnp.full((T, TOK_WIDTH), POOL_WINDOWS[-1], jnp.int32)
    for gi in range(len(POOL_WINDOWS) - 2, -1, -1):
        in_group = lane < (gi + 1) * POOL_GROUP_WIDTH
        win_sum = jnp.where(in_group, sums[gi][POOL_HALO:], win_sum)
        window = jnp.where(in_group, POOL_WINDOWS[gi], window)
    count = jnp.minimum(pos + 1, window).astype(F32)
    d = (win_sum / count - h).astype(BF16)
    y = jnp.dot(d, wg_ref[...], preferred_element_type=F32) * scale_ref[...]
    tok_ref[0] = y.astype(BF16)

    mo_ref[0] = _mem_attention_t(qm_t, mqg_ref[...], mk_ref, mvt_ref).astype(BF16)


def _pool_in(x, g, w_tok, wm_t, wg, scale, mq_gain, mk, mvt):
    B, S, D = x.shape
    M = mk.shape[1]
    T = min(TOKEN_TILE, S)
    const = lambda b, t: (0, 0)
    return pl.pallas_call(
        _pool_in_kernel,
        grid=(B, S // T),
        in_specs=[
            pl.BlockSpec((1, T, D), lambda b, t: (b, t, 0)),
            pl.BlockSpec((1, D), const),
            pl.BlockSpec((D, TOK_WIDTH), const),
            pl.BlockSpec((MEM_WIDTH, D), const),
            pl.BlockSpec((TOK_WIDTH, TOK_WIDTH), const),
            pl.BlockSpec((1, TOK_WIDTH), const),
            pl.BlockSpec((HEAD_DIM, 1), const),
            pl.BlockSpec((1, M, MEM_WIDTH), lambda b, t: (b, 0, 0)),
            pl.BlockSpec((1, MEM_WIDTH, M), lambda b, t: (b, 0, 0)),
        ],
        out_specs=[
            pl.BlockSpec((1, T, TOK_WIDTH), lambda b, t: (b, t, 0)),
            pl.BlockSpec((1, T, MEM_WIDTH), lambda b, t: (b, t, 0)),
        ],
        out_shape=[
            jax.ShapeDtypeStruct((B, S, TOK_WIDTH), BF16),
            jax.ShapeDtypeStruct((B, S, MEM_WIDTH), BF16),
        ],
        scratch_shapes=[pltpu.VMEM((POOL_HALO, TOK_WIDTH), F32)],
        compiler_params=_params(),
        name="pool_in",
    )(x, g, w_tok, wm_t, wg, scale, mq_gain, mk, mvt)


def _selection_bias(gate, t, T):
    nb, width = gate.shape
    blk = lax.broadcasted_iota(jnp.int32, (nb, width), 0)
    pos = t * T + lax.rem(lax.broadcasted_iota(jnp.int32, (nb, width), 1), T)
    own = lax.shift_right_logical(pos, MOBA_BLOCK.bit_length() - 1)
    g = jnp.where(blk < own, gate, -jnp.inf)
    bias = jnp.full((nb, width), NEG, F32)
    for _ in range(MOBA_TOPK):
        mx = jnp.max(g, axis=0, keepdims=True)
        idx = jnp.min(jnp.where(g == mx, blk, nb), axis=0, keepdims=True)
        idx = jnp.where(mx > -jnp.inf, idx, -1)
        pick = blk == idx
        bias = jnp.where(pick, 0.0, bias)
        g = jnp.where(pick, -jnp.inf, g)
    return bias


def _moba_in_kernel(x_ref, g_ref, wqkv_t_ref, wm_t_ref, cos_ref, sin_ref, qg_ref, kg_ref, mqg_ref,
                    mk_ref, mvt_ref, qt_ref, k_ref, vt_ref, bias_ref, mo_ref, kmean_ref):
    t = pl.program_id(1)
    T = x_ref.shape[1]
    blocks_per_tile = T // MOBA_BLOCK

    @pl.when(t == 0)
    def _():
        kmean_ref[...] = jnp.zeros_like(kmean_ref)

    u = _rms_rows(x_ref[0], g_ref[...]).astype(BF16)
    cos, sin = cos_ref[...], sin_ref[...]

    def proj_t(lo, hi):
        return lax.dot_general(wqkv_t_ref[lo:hi, :], u, NT_DIMS, preferred_element_type=F32)

    k_t = proj_t(TOK_WIDTH, 2 * TOK_WIDTH)
    q_t = proj_t(0, TOK_WIDTH)

    k_heads = [_rope_t(_head_rms_t(k_t[h * HEAD_DIM:(h + 1) * HEAD_DIM], kg_ref[...]), cos, sin)
               for h in range(MOBA_HEADS)]
    k_nat = jnp.concatenate(k_heads, axis=0).T
    k_ref[0] = k_nat.astype(BF16)
    for c in range(blocks_per_tile):
        kmean_ref[pl.ds(t * blocks_per_tile + c, 1), :] = jnp.mean(
            k_nat[c * MOBA_BLOCK:(c + 1) * MOBA_BLOCK], axis=0, keepdims=True)

    km_hi, km_lo = _split_bf16(kmean_ref[...])
    heads_per_chunk = MOBA_HEADS // VALUE_CHUNKS
    chunk_rows = TOK_WIDTH // VALUE_CHUNKS
    for chunk in range(VALUE_CHUNKS):
        lo = 2 * TOK_WIDTH + chunk * chunk_rows
        v_t = proj_t(lo, lo + chunk_rows)
        for c in range(blocks_per_tile):
            vt_ref[0, c, chunk * chunk_rows:(chunk + 1) * chunk_rows, :] = (
                v_t[:, c * MOBA_BLOCK:(c + 1) * MOBA_BLOCK].astype(BF16))
        heads = range(chunk * heads_per_chunk, (chunk + 1) * heads_per_chunk)
        gates = []
        for h in heads:
            rows = slice(h * HEAD_DIM, (h + 1) * HEAD_DIM)
            pair, slot = divmod(h, 2)
            lanes = slice(pair * LANES, (pair + 1) * LANES)
            q = _rope_t(_head_rms_t(q_t[rows], qg_ref[...]), cos, sin)
            qt_ref[0, rows, :] = (q * (SM_SCALE * LOG2_E)).astype(BF16)
            q_hi, q_lo = _split_bf16(q)
            q_hi2, q_lo2 = _pair_operand(q_hi, slot), _pair_operand(q_lo, slot)
            gates.append(jnp.dot(jnp.concatenate([km_hi[:, lanes], km_lo[:, lanes]], axis=1),
                                 jnp.concatenate([q_hi2, q_hi2], axis=0), preferred_element_type=F32)
                         + jnp.dot(km_hi[:, lanes], q_lo2, preferred_element_type=F32))
        bias = _selection_bias(jnp.concatenate(gates, axis=1), t, T)
        for n, h in enumerate(heads):
            bias_ref[0, h] = bias[:, n * T:(n + 1) * T]

    qm_t = lax.dot_general(wm_t_ref[...], u, NT_DIMS, preferred_element_type=F32)
    mo_ref[0] = _mem_attention_t(qm_t, mqg_ref[...], mk_ref, mvt_ref).astype(BF16)


def _moba_in(x, g, wqkv_t, wm_t, cos_t, sin_t, q_gain, k_gain, mq_gain, mk, mvt):
    B, S, D = x.shape
    M = mk.shape[1]
    T = min(TOKEN_TILE, S)
    nb = S // MOBA_BLOCK
    const = lambda b, t: (0, 0)
    return pl.pallas_call(
        _moba_in_kernel,
        grid=(B, S // T),
        in_specs=[
            pl.BlockSpec((1, T, D), lambda b, t: (b, t, 0)),
            pl.BlockSpec((1, D), const),
            pl.BlockSpec((3 * TOK_WIDTH, D), const),
            pl.BlockSpec((MEM_WIDTH, D), const),
            pl.BlockSpec((HALF_DIM, T), lambda b, t: (0, t)),
            pl.BlockSpec((HALF_DIM, T), lambda b, t: (0, t)),
            pl.BlockSpec((HEAD_DIM, 1), const),
            pl.BlockSpec((HEAD_DIM, 1), const),
            pl.BlockSpec((HEAD_DIM, 1), const),
            pl.BlockSpec((1, M, MEM_WIDTH), lambda b, t: (b, 0, 0)),
            pl.BlockSpec((1, MEM_WIDTH, M), lambda b, t: (b, 0, 0)),
        ],
        out_specs=[
            pl.BlockSpec((1, TOK_WIDTH, T), lambda b, t: (b, 0, t)),
            pl.BlockSpec((1, T, TOK_WIDTH), lambda b, t: (b, t, 0)),
            pl.BlockSpec((1, T // MOBA_BLOCK, TOK_WIDTH, MOBA_BLOCK), lambda b, t: (b, t, 0, 0)),
            pl.BlockSpec((1, MOBA_HEADS, nb, T), lambda b, t: (b, 0, 0, t)),
            pl.BlockSpec((1, T, MEM_WIDTH), lambda b, t: (b, t, 0)),
        ],
        out_shape=[
            jax.ShapeDtypeStruct((B, TOK_WIDTH, S), BF16),
            jax.ShapeDtypeStruct((B, S, TOK_WIDTH), BF16),
            jax.ShapeDtypeStruct((B, nb, TOK_WIDTH, MOBA_BLOCK), BF16),
            jax.ShapeDtypeStruct((B, MOBA_HEADS, nb, S), F32),
            jax.ShapeDtypeStruct((B, S, MEM_WIDTH), BF16),
        ],
        scratch_shapes=[pltpu.VMEM((nb, TOK_WIDTH), F32)],
        compiler_params=_params(),
        name="moba_in",
    )(x, g, wqkv_t, wm_t, cos_t, sin_t, q_gain, k_gain, mq_gain, mk, mvt)


def _moba_attn_kernel(qt_ref, k_ref, vt_ref, bias_ref, onehot_ref, o_ref, *scratch):
    for n in range(TILES_PER_STEP):
        _moba_attn_tile(pl.program_id(2) * TILES_PER_STEP + n, n, qt_ref, k_ref, vt_ref, bias_ref, onehot_ref,
                        o_ref, scratch)


def _moba_attn_tile(tile, slot, qt_ref, k_ref, vt_ref, bias_ref, onehot_ref, o_ref, scratch):
    TQ = QUERY_BLOCKS * MOBA_BLOCK
    cols = slice(slot * TQ, (slot + 1) * TQ)
    first_block = tile * QUERY_BLOCKS
    s_refs = (scratch[0:2], scratch[2:4])
    bm_refs = (scratch[4:6], scratch[6:8])
    group_keys = KV_GROUP * MOBA_BLOCK
    total_groups = k_ref.shape[1] // group_keys
    past_blocks = first_block + QUERY_BLOCKS - 1
    live_groups = (past_blocks + KV_GROUP - 1) // KV_GROUP

    q_pair = qt_ref[0, :, cols]
    row = lax.broadcasted_iota(jnp.int32, q_pair.shape, 0)
    zero = jnp.zeros_like(q_pair)
    q_heads = (jnp.where(row < HEAD_DIM, q_pair, zero), jnp.where(row >= HEAD_DIM, q_pair, zero))

    zeros_tail = jnp.zeros((MXU_DEPTH - LANES - BIAS_ROWS, TQ), BF16)
    ones_rows = jnp.ones((SUM_ROWS, 1), BF16)

    def produce(g, parity, a, s_ref, bm_ref):
        gc = jnp.minimum(g, total_groups - 1)
        k_g = k_ref[0, pl.ds(pl.multiple_of(gc * group_keys, group_keys), group_keys), :]
        lhs = jnp.concatenate([k_g, onehot_ref[parity]], axis=1)
        penalty = jnp.where(g >= live_groups, NEG, 0.0)
        slab_row = pl.multiple_of((gc // 2) * 2 * KV_GROUP, 2 * KV_GROUP)
        slab = bias_ref[0, a, pl.ds(slab_row, 2 * KV_GROUP), cols] + penalty
        slab = jnp.concatenate([slab, jnp.zeros_like(slab)], axis=0).astype(BF16)
        rhs = jnp.concatenate([q_heads[a], slab, zeros_tail], axis=0)
        s = jnp.dot(lhs, rhs, preferred_element_type=F32)
        s_ref[...] = s
        bm_ref[...] = jnp.max(s, axis=0, keepdims=True)

    def pv_operand(v_t):
        return jnp.concatenate([v_t, jnp.broadcast_to(ones_rows, (SUM_ROWS, v_t.shape[1]))], axis=0)

    def consume(g, a, s_ref, bm_ref, state):
        gc = jnp.minimum(g, total_groups - 1)
        j0 = gc * KV_GROUP
        rows = slice(a * HEAD_DIM, (a + 1) * HEAD_DIM)
        v_g = jnp.concatenate([vt_ref[0, j0 + c, rows, :] for c in range(KV_GROUP)], axis=1)
        m, acc = state
        m_new = jnp.maximum(m, bm_ref[...])
        alpha = jnp.exp2(m - m_new)
        p = jnp.exp2(s_ref[...] - m_new).astype(BF16)
        pv = jnp.dot(pv_operand(v_g), p, preferred_element_type=F32)
        return m_new, alpha * acc + pv

    key_idx = lax.broadcasted_iota(jnp.int32, (MOBA_BLOCK, MOBA_BLOCK), 0)
    qry_idx = lax.broadcasted_iota(jnp.int32, (MOBA_BLOCK, MOBA_BLOCK), 1)
    causal = key_idx <= qry_idx
    own = []
    for qb in range(QUERY_BLOCKS):
        qcols = slice(qb * MOBA_BLOCK, (qb + 1) * MOBA_BLOCK)
        k_own = k_ref[0, pl.ds(pl.multiple_of((first_block + qb) * MOBA_BLOCK, MOBA_BLOCK), MOBA_BLOCK), :]
        own.append([jnp.where(causal, jnp.dot(k_own, q_heads[a][:, qcols], preferred_element_type=F32), NEG)
                    for a in range(2)])
    produce(0, 0, 0, s_refs[0][0], bm_refs[0][0])
    produce(0, 0, 1, s_refs[0][1], bm_refs[0][1])
    init = []
    for a in range(2):
        rows = slice(a * HEAD_DIM, (a + 1) * HEAD_DIM)
        ms, accs = [], []
        for qb in range(QUERY_BLOCKS):
            m = jnp.max(own[qb][a], axis=0, keepdims=True)
            p = jnp.exp2(own[qb][a] - m).astype(BF16)
            ms.append(m)
            accs.append(jnp.dot(pv_operand(vt_ref[0, first_block + qb, rows, :]), p, preferred_element_type=F32))
        init.append((jnp.concatenate(ms, axis=1), jnp.concatenate(accs, axis=1)))

    def run_steps(groups_per_step, first_group, n_steps, carry):
        def body(step, carry):
            g = first_group + groups_per_step * step
            state = list(carry)
            for sub in range(groups_per_step):
                cur = sub % 2
                nxt = 1 - cur
                for a in range(2):
                    produce(g + sub + 1, nxt, a, s_refs[nxt][a], bm_refs[nxt][a])
                    state[a] = consume(g + sub, a, s_refs[cur][a], bm_refs[cur][a], state[a])
            return tuple(state)
        return lax.fori_loop(0, n_steps, body, carry)

    carry, done = tuple(init), 0
    for size in STEP_GROUPS:
        n_steps = (live_groups - done) // size
        carry = run_steps(size, done, n_steps, carry)
        done = done + n_steps * size

    def last_group(_, carry):
        return tuple(consume(done, a, s_refs[0][a], bm_refs[0][a], carry[a]) for a in range(2))

    final = lax.fori_loop(0, live_groups - done, last_group, carry)
    o_t = jnp.concatenate([acc[:HEAD_DIM] / acc[HEAD_DIM:HEAD_DIM + 1] for (_, acc) in final], axis=0)
    o_ref[0, cols, :] = o_t.T.astype(BF16)


def _block_onehot():
    key_block = jnp.arange(KV_GROUP * MOBA_BLOCK, dtype=jnp.int32) // MOBA_BLOCK
    col = jnp.arange(LANES, dtype=jnp.int32)
    parity = jnp.arange(2, dtype=jnp.int32)
    hit = col[None, None, :] == (parity[:, None, None] * KV_GROUP + key_block[None, :, None])
    return hit.astype(BF16)


def _moba_attn(qt, k, vt, bias):
    B, _, S = qt.shape
    nb = S // MOBA_BLOCK
    tq = TILES_PER_STEP * QUERY_BLOCKS * MOBA_BLOCK
    assert nb % (TILES_PER_STEP * QUERY_BLOCKS) == 0
    assert 2 * KV_GROUP == F32_SUBLANES and nb % F32_SUBLANES == 0, "bias slabs are whole f32 sublane tiles"
    assert STEP_GROUPS[-1] == 2 and all(size % 2 == 0 for size in STEP_GROUPS), "score buffers alternate by group parity"
    pairs = MOBA_HEADS // 2
    group_keys = KV_GROUP * MOBA_BLOCK
    return pl.pallas_call(
        _moba_attn_kernel,
        grid=(B, pairs, nb // (TILES_PER_STEP * QUERY_BLOCKS)),
        in_specs=[
            pl.BlockSpec((1, LANES, tq), lambda b, p, i: (b, p, i)),
            pl.BlockSpec((1, S, LANES), lambda b, p, i: (b, 0, p)),
            pl.BlockSpec((1, nb, LANES, MOBA_BLOCK), lambda b, p, i: (b, 0, p, 0)),
            pl.BlockSpec((1, 2, nb, tq), lambda b, p, i: (b, p, 0, i)),
            pl.BlockSpec((2, group_keys, LANES), lambda b, p, i: (0, 0, 0)),
        ],
        out_specs=pl.BlockSpec((1, tq, LANES), lambda b, p, i: (b, i, p)),
        out_shape=jax.ShapeDtypeStruct((B, S, TOK_WIDTH), BF16),
        scratch_shapes=([pltpu.VMEM((group_keys, QUERY_BLOCKS * MOBA_BLOCK), F32)] * 4
                        + [pltpu.VMEM((1, QUERY_BLOCKS * MOBA_BLOCK), F32)] * 4),
        compiler_params=pltpu.CompilerParams(
            dimension_semantics=("arbitrary", "arbitrary", "arbitrary"), vmem_limit_bytes=VMEM_LIMIT_BYTES),
        name="moba_attn",
    )(qt, k, vt, bias, _block_onehot())


def _post_kernel(x_ref, tok_ref, mo_ref, wo_ref, g_ref, w1_ref, w2_ref, o_ref):
    mixed = jnp.concatenate([tok_ref[0], mo_ref[0]], axis=-1)
    x1 = x_ref[0] + jnp.dot(mixed, wo_ref[...], preferred_element_type=F32)
    u = _rms_rows(x1, g_ref[...]).astype(BF16)
    h = jnp.dot(u, w1_ref[...], preferred_element_type=F32)
    h = jnp.square(jnp.maximum(h, 0.0)).astype(BF16)
    o_ref[0] = x1 + jnp.dot(h, w2_ref[...], preferred_element_type=F32)


def _post(x, tok, mo, wo, g, w1, w2):
    B, S, D = x.shape
    T = min(TOKEN_TILE, S)
    const = lambda b, t: (0, 0)
    resident = functools.partial(pl.BlockSpec, index_map=const, pipeline_mode=pl.Buffered(1))
    return pl.pallas_call(
        _post_kernel,
        grid=(B, S // T),
        in_specs=[
            pl.BlockSpec((1, T, D), lambda b, t: (b, t, 0)),
            pl.BlockSpec((1, T, TOK_WIDTH), lambda b, t: (b, t, 0)),
            pl.BlockSpec((1, T, MEM_WIDTH), lambda b, t: (b, t, 0)),
            resident((D, D)),
            pl.BlockSpec((1, D), const),
            resident((D, D_FF)),
            resident((D_FF, D)),
        ],
        out_specs=pl.BlockSpec((1, T, D), lambda b, t: (b, t, 0)),
        out_shape=jax.ShapeDtypeStruct((B, S, D), F32),
        compiler_params=_params(),
        name="post",
    )(x, tok, mo, wo, g, w1, w2)


def _rope_tables_t(seq_len):
    pos = jnp.arange(seq_len, dtype=F32)
    inv = ROPE_THETA ** (-jnp.arange(0, HEAD_DIM, 2, dtype=F32) / HEAD_DIM)
    ang = pos[:, None] * inv[None, :]
    return jnp.cos(ang).T, jnp.sin(ang).T


def _block_diag(w_group):
    G, C, _ = w_group.shape
    eye = jnp.eye(G, dtype=w_group.dtype)
    return (eye[:, None, :, None] * w_group[:, :, None, :]).reshape(G * C, G * C)


def kernel(x, mem, g_mix, g_mem, g_mlp, w_in_pool, w_pool_group, pool_scale, w_in_moba, moba_q_gain,
           moba_k_gain, w_mem_kv, mem_q_gain, mem_k_gain, w_out, w_ff1, w_ff2):
    depth = g_mix.shape[0]
    S = x.shape[1]
    cos_t, sin_t = _rope_tables_t(S)
    col = lambda v: v.reshape(-1, 1)
    row = lambda v: v.reshape(1, -1)
    for i in range(depth):
        j = i // 2
        mk, mvt = _mem_kv(mem, row(g_mem[i]), w_mem_kv[i].T.astype(BF16), col(mem_k_gain[i]))
        if i % 2 == 0:
            w = w_in_pool[j]
            tok, mo = _pool_in(
                x, row(g_mix[i]), w[:, :TOK_WIDTH].astype(BF16), w[:, TOK_WIDTH:].T.astype(BF16),
                _block_diag(w_pool_group[j]).astype(BF16), row(pool_scale[j]), col(mem_q_gain[i]), mk, mvt)
        else:
            w = w_in_moba[j]
            qt, k, vt, bias, mo = _moba_in(
                x, row(g_mix[i]), w[:, :3 * TOK_WIDTH].T.astype(BF16), w[:, 3 * TOK_WIDTH:].T.astype(BF16),
                cos_t, sin_t, col(moba_q_gain[j]), col(moba_k_gain[j]), col(mem_q_gain[i]), mk, mvt)
            tok = _moba_attn(qt, k, vt, bias)
        x = _post(x, tok, mo, w_out[i].astype(BF16), row(g_mlp[i]), w_ff1[i].astype(BF16), w_ff2[i].astype(BF16))
    return x
```

```python
import functools

import jax
import jax.numpy as jnp
from jax import lax
from jax.experimental import pallas as pl
from jax.experimental.pallas import tpu as pltpu

D_MODEL = 1024
HEAD_DIM = 64
HALF_DIM = HEAD_DIM // 2
MEM_HEADS = 4
MEM_WIDTH = MEM_HEADS * HEAD_DIM
TOK_WIDTH = D_MODEL - MEM_WIDTH
MOBA_HEADS = TOK_WIDTH // HEAD_DIM
MOBA_BLOCK = 256
MOBA_TOPK = 3
POOL_WINDOWS = (2, 4, 8, 16)
POOL_GROUP_WIDTH = TOK_WIDTH // len(POOL_WINDOWS)
POOL_HALO = 16
D_FF = 4 * D_MODEL
ROPE_THETA = 10000.0
EPS = 1e-6
NEG = -1e30
SM_SCALE = HEAD_DIM ** -0.5
LOG2_E = 1.4426950408889634

LANES = 128
F32_SUBLANES = 8
BIAS_ROWS = 16
SUM_ROWS = 16
MXU_DEPTH = 256
VMEM_LIMIT_BYTES = 56 * 1024 * 1024

TOKEN_TILE = 512
VALUE_CHUNKS = 3
QUERY_BLOCKS = 1
TILES_PER_STEP = 8
KV_GROUP = 4
STEP_GROUPS = (8, 4, 2)

F32 = jnp.float32
BF16 = jnp.bfloat16
NT_DIMS = (((1,), (1,)), ((), ()))


def _params():
    return pltpu.CompilerParams(
        dimension_semantics=("arbitrary", "arbitrary"), vmem_limit_bytes=VMEM_LIMIT_BYTES)


def _rms_rows(x, g):
    ms = jnp.mean(x * x, axis=-1, keepdims=True)
    return x * lax.rsqrt(ms + EPS) * g


def _head_rms_t(h, gain_col):
    ms = jnp.mean(h * h, axis=0, keepdims=True)
    return h * lax.rsqrt(ms + EPS) * gain_col


def _rope_t(h, cos, sin):
    h1, h2 = h[:HALF_DIM], h[HALF_DIM:]
    return jnp.concatenate([h1 * cos - h2 * sin, h2 * cos + h1 * sin], axis=0)


def _split_bf16(x):
    hi = x.astype(BF16)
    return hi, (x - hi.astype(F32)).astype(BF16)


def _pair_operand(q_bf, slot):
    z = jnp.zeros_like(q_bf)
    return jnp.concatenate([q_bf, z] if slot == 0 else [z, q_bf], axis=0)


def _mem_attention_t(qm_t, mq_gain, mk_ref, mvt_ref):
    scores = []
    for h in range(MEM_HEADS):
        pair, slot = divmod(h, 2)
        q = _head_rms_t(qm_t[h * HEAD_DIM:(h + 1) * HEAD_DIM], mq_gain)
        q2 = _pair_operand((q * (SM_SCALE * LOG2_E)).astype(BF16), slot)
        k_pair = mk_ref[0, :, pair * LANES:(pair + 1) * LANES]
        scores.append(jnp.dot(k_pair, q2, preferred_element_type=F32))
    probs = [jnp.exp2(s - jnp.max(s, axis=0, keepdims=True)).astype(BF16) for s in scores]
    outs = []
    for h in range(MEM_HEADS):
        v_t = mvt_ref[0, h * HEAD_DIM:(h + 1) * HEAD_DIM, :]
        ones = jnp.ones((SUM_ROWS, v_t.shape[1]), BF16)
        r = jnp.dot(jnp.concatenate([v_t, ones], axis=0), probs[h], preferred_element_type=F32)
        outs.append(r[:HEAD_DIM] / r[HEAD_DIM:HEAD_DIM + 1])
    return jnp.concatenate(outs, axis=0).T


def _mem_kv_kernel(mem_ref, g_ref, wkv_t_ref, kg_ref, mk_ref, mvt_ref):
    mem_n = _rms_rows(mem_ref[0], g_ref[...]).astype(BF16)
    kv_t = lax.dot_general(wkv_t_ref[...], mem_n, NT_DIMS, preferred_element_type=F32)
    k_heads = [_head_rms_t(kv_t[h * HEAD_DIM:(h + 1) * HEAD_DIM], kg_ref[...]) for h in range(MEM_HEADS)]
    mk_ref[0] = jnp.concatenate(k_heads, axis=0).T.astype(BF16)
    mvt_ref[0] = kv_t[MEM_WIDTH:].astype(BF16)


def _mem_kv(mem, g, wkv_t, k_gain):
    B, M, D = mem.shape
    return pl.pallas_call(
        _mem_kv_kernel,
        grid=(B, 1),
        in_specs=[
            pl.BlockSpec((1, M, D), lambda b, _: (b, 0, 0)),
            pl.BlockSpec((1, D), lambda b, _: (0, 0)),
            pl.BlockSpec((2 * MEM_WIDTH, D), lambda b, _: (0, 0)),
            pl.BlockSpec((HEAD_DIM, 1), lambda b, _: (0, 0)),
        ],
        out_specs=[
            pl.BlockSpec((1, M, MEM_WIDTH), lambda b, _: (b, 0, 0)),
            pl.BlockSpec((1, MEM_WIDTH, M), lambda b, _: (b, 0, 0)),
        ],
        out_shape=[
            jax.ShapeDtypeStruct((B, M, MEM_WIDTH), BF16),
            jax.ShapeDtypeStruct((B, MEM_WIDTH, M), BF16),
        ],
        compiler_params=_params(),
        name="mem_kv",
    )(mem, g, wkv_t, k_gain)


def _pool_in_kernel(x_ref, g_ref, w_tok_ref, wm_t_ref, wg_ref, scale_ref, mqg_ref, mk_ref, mvt_ref,
                    tok_ref, mo_ref, halo_ref):
    t = pl.program_id(1)
    T = x_ref.shape[1]

    @pl.when(t == 0)
    def _():
        halo_ref[...] = jnp.zeros_like(halo_ref)

    u = _rms_rows(x_ref[0], g_ref[...]).astype(BF16)
    h = jnp.dot(u, w_tok_ref[...], preferred_element_type=F32)
    qm_t = lax.dot_general(wm_t_ref[...], u, NT_DIMS, preferred_element_type=F32)

    ext = jnp.concatenate([halo_ref[...], h], axis=0)
    halo_ref[...] = h[T - POOL_HALO:]
    s2 = ext + pltpu.roll(ext, 1, 0)
    s4 = s2 + pltpu.roll(s2, 2, 0)
    s8 = s4 + pltpu.roll(s4, 4, 0)
    s16 = s8 + pltpu.roll(s8, 8, 0)
    lane = lax.broadcasted_iota(jnp.int32, (T, TOK_WIDTH), 1)
    pos = t * T + lax.broadcasted_iota(jnp.int32, (T, TOK_WIDTH), 0)
    sums = (s2, s4, s8, s16)
    win_sum = sums[-1][POOL_HALO:]
    window = jnp.full((T, TOK_WIDTH), POOL_WINDOWS[-1], jnp.int32)
    for gi in range(len(POOL_WINDOWS) - 2, -1, -1):
        in_group = lane < (gi + 1) * POOL_GROUP_WIDTH
        win_sum = jnp.where(in_group, sums[gi][POOL_HALO:], win_sum)
        window = jnp.where(in_group, POOL_WINDOWS[gi], window)
    count = jnp.minimum(pos + 1, window).astype(F32)
    d = (win_sum / count - h).astype(BF16)
    y = jnp.dot(d, wg_ref[...], preferred_element_type=F32) * scale_ref[...]
    tok_ref[0] = y.astype(BF16)

    mo_ref[0] = _mem_attention_t(qm_t, mqg_ref[...], mk_ref, mvt_ref).astype(BF16)


def _pool_in(x, g, w_tok, wm_t, wg, scale, mq_gain, mk, mvt):
    B, S, D = x.shape
    M = mk.shape[1]
    T = min(TOKEN_TILE, S)
    const = lambda b, t: (0, 0)
    return pl.pallas_call(
        _pool_in_kernel,
        grid=(B, S // T),
        in_specs=[
            pl.BlockSpec((1, T, D), lambda b, t: (b, t, 0)),
            pl.BlockSpec((1, D), const),
            pl.BlockSpec((D, TOK_WIDTH), const),
            pl.BlockSpec((MEM_WIDTH, D), const),
            pl.BlockSpec((TOK_WIDTH, TOK_WIDTH), const),
            pl.BlockSpec((1, TOK_WIDTH), const),
            pl.BlockSpec((HEAD_DIM, 1), const),
            pl.BlockSpec((1, M, MEM_WIDTH), lambda b, t: (b, 0, 0)),
            pl.BlockSpec((1, MEM_WIDTH, M), lambda b, t: (b, 0, 0)),
        ],
        out_specs=[
            pl.BlockSpec((1, T, TOK_WIDTH), lambda b, t: (b, t, 0)),
            pl.BlockSpec((1, T, MEM_WIDTH), lambda b, t: (b, t, 0)),
        ],
        out_shape=[
            jax.ShapeDtypeStruct((B, S, TOK_WIDTH), BF16),
            jax.ShapeDtypeStruct((B, S, MEM_WIDTH), BF16),
        ],
        scratch_shapes=[pltpu.VMEM((POOL_HALO, TOK_WIDTH), F32)],
        compiler_params=_params(),
        name="pool_in",
    )(x, g, w_tok, wm_t, wg, scale, mq_gain, mk, mvt)


def _selection_bias(gate, t, T):
    nb, width = gate.shape
    blk = lax.broadcasted_iota(jnp.int32, (nb, width), 0)
    pos = t * T + lax.rem(lax.broadcasted_iota(jnp.int32, (nb, width), 1), T)
    own = lax.shift_right_logical(pos, MOBA_BLOCK.bit_length() - 1)
    g = jnp.where(blk < own, gate, -jnp.inf)
    bias = jnp.full((nb, width), NEG, F32)
    for _ in range(MOBA_TOPK):
        mx = jnp.max(g, axis=0, keepdims=True)
        idx = jnp.min(jnp.where(g == mx, blk, nb), axis=0, keepdims=True)
        idx = jnp.where(mx > -jnp.inf, idx, -1)
        pick = blk == idx
        bias = jnp.where(pick, 0.0, bias)
        g = jnp.where(pick, -jnp.inf, g)
    return bias


def _moba_in_kernel(x_ref, g_ref, wqkv_t_ref, wm_t_ref, cos_ref, sin_ref, qg_ref, kg_ref, mqg_ref,
                    mk_ref, mvt_ref, qt_ref, k_ref, vt_ref, bias_ref, mo_ref, kmean_ref):
    t = pl.program_id(1)
    T = x_ref.shape[1]
    blocks_per_tile = T // MOBA_BLOCK

    @pl.when(t == 0)
    def _():
        kmean_ref[...] = jnp.zeros_like(kmean_ref)

    u = _rms_rows(x_ref[0], g_ref[...]).astype(BF16)
    cos, sin = cos_ref[...], sin_ref[...]

    def proj_t(lo, hi):
        return lax.dot_general(wqkv_t_ref[lo:hi, :], u, NT_DIMS, preferred_element_type=F32)

    k_t = proj_t(TOK_WIDTH, 2 * TOK_WIDTH)
    q_t = proj_t(0, TOK_WIDTH)

    k_heads = [_rope_t(_head_rms_t(k_t[h * HEAD_DIM:(h + 1) * HEAD_DIM], kg_ref[...]), cos, sin)
               for h in range(MOBA_HEADS)]
    k_nat = jnp.concatenate(k_heads, axis=0).T
    k_ref[0] = k_nat.astype(BF16)
    for c in range(blocks_per_tile):
        kmean_ref[pl.ds(t * blocks_per_tile + c, 1), :] = jnp.mean(
            k_nat[c * MOBA_BLOCK:(c + 1) * MOBA_BLOCK], axis=0, keepdims=True)

    km_hi, km_lo = _split_bf16(kmean_ref[...])
    heads_per_chunk = MOBA_HEADS // VALUE_CHUNKS
    chunk_rows = TOK_WIDTH // VALUE_CHUNKS
    for chunk in range(VALUE_CHUNKS):
        lo = 2 * TOK_WIDTH + chunk * chunk_rows
        v_t = proj_t(lo, lo + chunk_rows)
        for c in range(blocks_per_tile):
            vt_ref[0, c, chunk * chunk_rows:(chunk + 1) * chunk_rows, :] = (
                v_t[:, c * MOBA_BLOCK:(c + 1) * MOBA_BLOCK].astype(BF16))
        heads = range(chunk * heads_per_chunk, (chunk + 1) * heads_per_chunk)
        gates = []
        for h in heads:
            rows = slice(h * HEAD_DIM, (h + 1) * HEAD_DIM)
            pair, slot = divmod(h, 2)
            lanes = slice(pair * LANES, (pair + 1) * LANES)
            q = _rope_t(_head_rms_t(q_t[rows], qg_ref[...]), cos, sin)
            qt_ref[0, rows, :] = (q * (SM_SCALE * LOG2_E)).astype(BF16)
            q_hi, q_lo = _split_bf16(q)
            q_hi2, q_lo2 = _pair_operand(q_hi, slot), _pair_operand(q_lo, slot)
            gates.append(jnp.dot(jnp.concatenate([km_hi[:, lanes], km_lo[:, lanes]], axis=1),
                                 jnp.concatenate([q_hi2, q_hi2], axis=0), preferred_element_type=F32)
                         + jnp.dot(km_hi[:, lanes], q_lo2, preferred_element_type=F32))
        bias = _selection_bias(jnp.concatenate(gates, axis=1), t, T)
        for n, h in enumerate(heads):
            bias_ref[0, h] = bias[:, n * T:(n + 1) * T]

    qm_t = lax.dot_general(wm_t_ref[...], u, NT_DIMS, preferred_element_type=F32)
    mo_ref[0] = _mem_attention_t(qm_t, mqg_ref[...], mk_ref, mvt_ref).astype(BF16)


def _moba_in(x, g, wqkv_t, wm_t, cos_t, sin_t, q_gain, k_gain, mq_gain, mk, mvt):
    B, S, D = x.shape
    M = mk.shape[1]
    T = min(TOKEN_TILE, S)
    nb = S // MOBA_BLOCK
    const = lambda b, t: (0, 0)
    return pl.pallas_call(
        _moba_in_kernel,
        grid=(B, S // T),
        in_specs=[
            pl.BlockSpec((1, T, D), lambda b, t: (b, t, 0)),
            pl.BlockSpec((1, D), const),
            pl.BlockSpec((3 * TOK_WIDTH, D), const),
            pl.BlockSpec((MEM_WIDTH, D), const),
            pl.BlockSpec((HALF_DIM, T), lambda b, t: (0, t)),
            pl.BlockSpec((HALF_DIM, T), lambda b, t: (0, t)),
            pl.BlockSpec((HEAD_DIM, 1), const),
            pl.BlockSpec((HEAD_DIM, 1), const),
            pl.BlockSpec((HEAD_DIM, 1), const),
            pl.BlockSpec((1, M, MEM_WIDTH), lambda b, t: (b, 0, 0)),
            pl.BlockSpec((1, MEM_WIDTH, M), lambda b, t: (b, 0, 0)),
        ],
        out_specs=[
            pl.BlockSpec((1, TOK_WIDTH, T), lambda b, t: (b, 0, t)),
            pl.BlockSpec((1, T, TOK_WIDTH), lambda b, t: (b, t, 0)),
            pl.BlockSpec((1, T // MOBA_BLOCK, TOK_WIDTH, MOBA_BLOCK), lambda b, t: (b, t, 0, 0)),
            pl.BlockSpec((1, MOBA_HEADS, nb, T), lambda b, t: (b, 0, 0, t)),
            pl.BlockSpec((1, T, MEM_WIDTH), lambda b, t: (b, t, 0)),
        ],
        out_shape=[
            jax.ShapeDtypeStruct((B, TOK_WIDTH, S), BF16),
            jax.ShapeDtypeStruct((B, S, TOK_WIDTH), BF16),
            jax.ShapeDtypeStruct((B, nb, TOK_WIDTH, MOBA_BLOCK), BF16),
            jax.ShapeDtypeStruct((B, MOBA_HEADS, nb, S), F32),
            jax.ShapeDtypeStruct((B, S, MEM_WIDTH), BF16),
        ],
        scratch_shapes=[pltpu.VMEM((nb, TOK_WIDTH), F32)],
        compiler_params=_params(),
        name="moba_in",
    )(x, g, wqkv_t, wm_t, cos_t, sin_t, q_gain, k_gain, mq_gain, mk, mvt)


def _moba_attn_kernel(qt_ref, k_ref, vt_ref, bias_ref, onehot_ref, o_ref, *scratch):
    for n in range(TILES_PER_STEP):
        _moba_attn_tile(pl.program_id(2) * TILES_PER_STEP + n, n, qt_ref, k_ref, vt_ref, bias_ref, onehot_ref,
                        o_ref, scratch)


def _moba_attn_tile(tile, slot, qt_ref, k_ref, vt_ref, bias_ref, onehot_ref, o_ref, scratch):
    TQ = QUERY_BLOCKS * MOBA_BLOCK
    cols = slice(slot * TQ, (slot + 1) * TQ)
    first_block = tile * QUERY_BLOCKS
    s_refs = (scratch[0:2], scratch[2:4])
    bm_refs = (scratch[4:6], scratch[6:8])
    group_keys = KV_GROUP * MOBA_BLOCK
    total_groups = k_ref.shape[1] // group_keys
    past_blocks = first_block + QUERY_BLOCKS - 1
    live_groups = (past_blocks + KV_GROUP - 1) // KV_GROUP

    q_pair = qt_ref[0, :, cols]
    row = lax.broadcasted_iota(jnp.int32, q_pair.shape, 0)
    zero = jnp.zeros_like(q_pair)
    q_heads = (jnp.where(row < HEAD_DIM, q_pair, zero), jnp.where(row >= HEAD_DIM, q_pair, zero))

    zeros_tail = jnp.zeros((MXU_DEPTH - LANES - BIAS_ROWS, TQ), BF16)
    ones_rows = jnp.ones((SUM_ROWS, 1), BF16)

    def produce(g, parity, a, s_ref, bm_ref):
        gc = jnp.minimum(g, total_groups - 1)
        k_g = k_ref[0, pl.ds(pl.multiple_of(gc * group_keys, group_keys), group_keys), :]
        lhs = jnp.concatenate([k_g, onehot_ref[parity]], axis=1)
        penalty = jnp.where(g >= live_groups, NEG, 0.0)
        slab_row = pl.multiple_of((gc // 2) * 2 * KV_GROUP, 2 * KV_GROUP)
        slab = bias_ref[0, a, pl.ds(slab_row, 2 * KV_GROUP), cols] + penalty
        slab = jnp.concatenate([slab, jnp.zeros_like(slab)], axis=0).astype(BF16)
        rhs = jnp.concatenate([q_heads[a], slab, zeros_tail], axis=0)
        s = jnp.dot(lhs, rhs, preferred_element_type=F32)
        s_ref[...] = s
        bm_ref[...] = jnp.max(s, axis=0, keepdims=True)

    def pv_operand(v_t):
        return jnp.concatenate([v_t, jnp.broadcast_to(ones_rows, (SUM_ROWS, v_t.shape[1]))], axis=0)

    def consume(g, a, s_ref, bm_ref, state):
        gc = jnp.minimum(g, total_groups - 1)
        j0 = gc * KV_GROUP
        rows = slice(a * HEAD_DIM, (a + 1) * HEAD_DIM)
        v_g = jnp.concatenate([vt_ref[0, j0 + c, rows, :] for c in range(KV_GROUP)], axis=1)
        m, acc = state
        m_new = jnp.maximum(m, bm_ref[...])
        alpha = jnp.exp2(m - m_new)
        p = jnp.exp2(s_ref[...] - m_new).astype(BF16)
        pv = jnp.dot(pv_operand(v_g), p, preferred_element_type=F32)
        return m_new, alpha * acc + pv

    key_idx = lax.broadcasted_iota(jnp.int32, (MOBA_BLOCK, MOBA_BLOCK), 0)
    qry_idx = lax.broadcasted_iota(jnp.int32, (MOBA_BLOCK, MOBA_BLOCK), 1)
    causal = key_idx <= qry_idx
    own = []
    for qb in range(QUERY_BLOCKS):
        qcols = slice(qb * MOBA_BLOCK, (qb + 1) * MOBA_BLOCK)
        k_own = k_ref[0, pl.ds(pl.multiple_of((first_block + qb) * MOBA_BLOCK, MOBA_BLOCK), MOBA_BLOCK), :]
        own.append([jnp.where(causal, jnp.dot(k_own, q_heads[a][:, qcols], preferred_element_type=F32), NEG)
                    for a in range(2)])
    produce(0, 0, 0, s_refs[0][0], bm_refs[0][0])
    produce(0, 0, 1, s_refs[0][1], bm_refs[0][1])
    init = []
    for a in range(2):
        rows = slice(a * HEAD_DIM, (a + 1) * HEAD_DIM)
        ms, accs = [], []
        for qb in range(QUERY_BLOCKS):
            m = jnp.max(own[qb][a], axis=0, keepdims=True)
            p = jnp.exp2(own[qb][a] - m).astype(BF16)
            ms.append(m)
            accs.append(jnp.dot(pv_operand(vt_ref[0, first_block + qb, rows, :]), p, preferred_element_type=F32))
        init.append((jnp.concatenate(ms, axis=1), jnp.concatenate(accs, axis=1)))

    def run_steps(groups_per_step, first_group, n_steps, carry):
        def body(step, carry):
            g = first_group + groups_per_step * step
            state = list(carry)
            for sub in range(groups_per_step):
                cur = sub % 2
                nxt = 1 - cur
                for a in range(2):
                    produce(g + sub + 1, nxt, a, s_refs[nxt][a], bm_refs[nxt][a])
                    state[a] = consume(g + sub, a, s_refs[cur][a], bm_refs[cur][a], state[a])
            return tuple(state)
        return lax.fori_loop(0, n_steps, body, carry)

    carry, done = tuple(init), 0
    for size in STEP_GROUPS:
        n_steps = (live_groups - done) // size
        carry = run_steps(size, done, n_steps, carry)
        done = done + n_steps * size

    def last_group(_, carry):
        return tuple(consume(done, a, s_refs[0][a], bm_refs[0][a], carry[a]) for a in range(2))

    final = lax.fori_loop(0, live_groups - done, last_group, carry)
    o_t = jnp.concatenate([acc[:HEAD_DIM] / acc[HEAD_DIM:HEAD_DIM + 1] for (_, acc) in final], axis=0)
    o_ref[0, cols, :] = o_t.T.astype(BF16)


def _block_onehot():
    key_block = jnp.arange(KV_GROUP * MOBA_BLOCK, dtype=jnp.int32) // MOBA_BLOCK
    col = jnp.arange(LANES, dtype=jnp.int32)
    parity = jnp.arange(2, dtype=jnp.int32)
    hit = col[None, None, :] == (parity[:, None, None] * KV_GROUP + key_block[None, :, None])
    return hit.astype(BF16)


def _moba_attn(qt, k, vt, bias):
    B, _, S = qt.shape
    nb = S // MOBA_BLOCK
    tq = TILES_PER_STEP * QUERY_BLOCKS * MOBA_BLOCK
    assert nb % (TILES_PER_STEP * QUERY_BLOCKS) == 0
    assert 2 * KV_GROUP == F32_SUBLANES and nb % F32_SUBLANES == 0, "bias slabs are whole f32 sublane tiles"
    assert STEP_GROUPS[-1] == 2 and all(size % 2 == 0 for size in STEP_GROUPS), "score buffers alternate by group parity"
    pairs = MOBA_HEADS // 2
    group_keys = KV_GROUP * MOBA_BLOCK
    return pl.pallas_call(
        _moba_attn_kernel,
        grid=(B, pairs, nb // (TILES_PER_STEP * QUERY_BLOCKS)),
        in_specs=[
            pl.BlockSpec((1, LANES, tq), lambda b, p, i: (b, p, i)),
            pl.BlockSpec((1, S, LANES), lambda b, p, i: (b, 0, p)),
            pl.BlockSpec((1, nb, LANES, MOBA_BLOCK), lambda b, p, i: (b, 0, p, 0)),
            pl.BlockSpec((1, 2, nb, tq), lambda b, p, i: (b, p, 0, i)),
            pl.BlockSpec((2, group_keys, LANES), lambda b, p, i: (0, 0, 0)),
        ],
        out_specs=pl.BlockSpec((1, tq, LANES), lambda b, p, i: (b, i, p)),
        out_shape=jax.ShapeDtypeStruct((B, S, TOK_WIDTH), BF16),
        scratch_shapes=([pltpu.VMEM((group_keys, QUERY_BLOCKS * MOBA_BLOCK), F32)] * 4
                        + [pltpu.VMEM((1, QUERY_BLOCKS * MOBA_BLOCK), F32)] * 4),
        compiler_params=pltpu.CompilerParams(
            dimension_semantics=("arbitrary", "arbitrary", "arbitrary"), vmem_limit_bytes=VMEM_LIMIT_BYTES),
        name="moba_attn",
    )(qt, k, vt, bias, _block_onehot())


def _post_kernel(x_ref, tok_ref, mo_ref, wo_ref, g_ref, w1_ref, w2_ref, o_ref):
    mixed = jnp.concatenate([tok_ref[0], mo_ref[0]], axis=-1)
    x1 = x_ref[0] + jnp.dot(mixed, wo_ref[...], preferred_element_type=F32)
    u = _rms_rows(x1, g_ref[...]).astype(BF16)
    h = jnp.dot(u, w1_ref[...], preferred_element_type=F32)
    h = jnp.square(jnp.maximum(h, 0.0)).astype(BF16)
    o_ref[0] = x1 + jnp.dot(h, w2_ref[...], preferred_element_type=F32)


def _post(x, tok, mo, wo, g, w1, w2):
    B, S, D = x.shape
    T = min(TOKEN_TILE, S)
    const = lambda b, t: (0, 0)
    resident = functools.partial(pl.BlockSpec, index_map=const, pipeline_mode=pl.Buffered(1))
    return pl.pallas_call(
        _post_kernel,
        grid=(B, S // T),
        in_specs=[
            pl.BlockSpec((1, T, D), lambda b, t: (b, t, 0)),
            pl.BlockSpec((1, T, TOK_WIDTH), lambda b, t: (b, t, 0)),
            pl.BlockSpec((1, T, MEM_WIDTH), lambda b, t: (b, t, 0)),
            resident((D, D)),
            pl.BlockSpec((1, D), const),
            resident((D, D_FF)),
            resident((D_FF, D)),
        ],
        out_specs=pl.BlockSpec((1, T, D), lambda b, t: (b, t, 0)),
        out_shape=jax.ShapeDtypeStruct((B, S, D), F32),
        compiler_params=_params(),
        name="post",
    )(x, tok, mo, wo, g, w1, w2)


def _rope_tables_t(seq_len):
    pos = jnp.arange(seq_len, dtype=F32)
    inv = ROPE_THETA ** (-jnp.arange(0, HEAD_DIM, 2, dtype=F32) / HEAD_DIM)
    ang = pos[:, None] * inv[None, :]
    return jnp.cos(ang).T, jnp.sin(ang).T


def _block_diag(w_group):
    G, C, _ = w_group.shape
    eye = jnp.eye(G, dtype=w_group.dtype)
    return (eye[:, None, :, None] * w_group[:, :, None, :]).reshape(G * C, G * C)


def kernel(x, mem, g_mix, g_mem, g_mlp, w_in_pool, w_pool_group, pool_scale, w_in_moba, moba_q_gain,
           moba_k_gain, w_mem_kv, mem_q_gain, mem_k_gain, w_out, w_ff1, w_ff2):
    depth = g_mix.shape[0]
    S = x.shape[1]
    cos_t, sin_t = _rope_tables_t(S)
    col = lambda v: v.reshape(-1, 1)
    row = lambda v: v.reshape(1, -1)
    for i in range(depth):
        j = i // 2
        mk, mvt = _mem_kv(mem, row(g_mem[i]), w_mem_kv[i].T.astype(BF16), col(mem_k_gain[i]))
        if i % 2 == 0:
            w = w_in_pool[j]
            tok, mo = _pool_in(
                x, row(g_mix[i]), w[:, :TOK_WIDTH].astype(BF16), w[:, TOK_WIDTH:].T.astype(BF16),
                _block_diag(w_pool_group[j]).astype(BF16), row(pool_scale[j]), col(mem_q_gain[i]), mk, mvt)
        else:
            w = w_in_moba[j]
            qt, k, vt, bias, mo = _moba_in(
                x, row(g_mix[i]), w[:, :3 * TOK_WIDTH].T.astype(BF16), w[:, 3 * TOK_WIDTH:].T.astype(BF16),
                cos_t, sin_t, col(moba_q_gain[j]), col(moba_k_gain[j]), col(mem_q_gain[i]), mk, mvt)
            tok = _moba_attn(qt, k, vt, bias)
        x = _post(x, tok, mo, w_out[i].astype(BF16), row(g_mlp[i]), w_ff1[i].astype(BF16), w_ff2[i].astype(BF16))
    return x
```

```python
import functools

import jax
import jax.numpy as jnp
from jax import lax
from jax.experimental import pallas as pl
from jax.experimental.pallas import tpu as pltpu

D_MODEL = 1024
HEAD_DIM = 64
HALF_DIM = HEAD_DIM // 2
MEM_HEADS = 4
MEM_WIDTH = MEM_HEADS * HEAD_DIM
TOK_WIDTH = D_MODEL - MEM_WIDTH
MOBA_HEADS = TOK_WIDTH // HEAD_DIM
MOBA_BLOCK = 256
MOBA_TOPK = 3
POOL_WINDOWS = (2, 4, 8, 16)
POOL_GROUP_WIDTH = TOK_WIDTH // len(POOL_WINDOWS)
POOL_HALO = 16
D_FF = 4 * D_MODEL
ROPE_THETA = 10000.0
EPS = 1e-6
NEG = -1e30
SM_SCALE = HEAD_DIM ** -0.5
LOG2_E = 1.4426950408889634

LANES = 128
F32_SUBLANES = 8
BIAS_ROWS = 16
SUM_ROWS = 16
MXU_DEPTH = 256
VMEM_LIMIT_BYTES = 56 * 1024 * 1024

TOKEN_TILE = 512
VALUE_CHUNKS = 3
TILES_PER_STEP = 4
KV_GROUP = 4
ITEM_STEPS = (8, 2)

F32 = jnp.float32
BF16 = jnp.bfloat16
NT_DIMS = (((1,), (1,)), ((), ()))


def _params():
    return pltpu.CompilerParams(
        dimension_semantics=("arbitrary", "arbitrary"), vmem_limit_bytes=VMEM_LIMIT_BYTES)


def _rms_rows(x, g):
    ms = jnp.mean(x * x, axis=-1, keepdims=True)
    return x * lax.rsqrt(ms + EPS) * g


def _head_rms_t(h, gain_col):
    ms = jnp.mean(h * h, axis=0, keepdims=True)
    return h * lax.rsqrt(ms + EPS) * gain_col


def _rope_t(h, cos, sin):
    h1, h2 = h[:HALF_DIM], h[HALF_DIM:]
    return jnp.concatenate([h1 * cos - h2 * sin, h2 * cos + h1 * sin], axis=0)


def _split_bf16(x):
    hi = x.astype(BF16)
    return hi, (x - hi.astype(F32)).astype(BF16)


def _pair_operand(q_bf, slot):
    z = jnp.zeros_like(q_bf)
    return jnp.concatenate([q_bf, z] if slot == 0 else [z, q_bf], axis=0)


def _mem_attention_t(qm_t, mq_gain, mk_ref, mvt_ref):
    scores = []
    for h in range(MEM_HEADS):
        pair, slot = divmod(h, 2)
        q = _head_rms_t(qm_t[h * HEAD_DIM:(h + 1) * HEAD_DIM], mq_gain)
        q2 = _pair_operand((q * (SM_SCALE * LOG2_E)).astype(BF16), slot)
        k_pair = mk_ref[0, :, pair * LANES:(pair + 1) * LANES]
        scores.append(jnp.dot(k_pair, q2, preferred_element_type=F32))
    probs = [jnp.exp2(s - jnp.max(s, axis=0, keepdims=True)).astype(BF16) for s in scores]
    outs = []
    for h in range(MEM_HEADS):
        v_t = mvt_ref[0, h * HEAD_DIM:(h + 1) * HEAD_DIM, :]
        ones = jnp.ones((SUM_ROWS, v_t.shape[1]), BF16)
        r = jnp.dot(jnp.concatenate([v_t, ones], axis=0), probs[h], preferred_element_type=F32)
        outs.append(r[:HEAD_DIM] / r[HEAD_DIM:HEAD_DIM + 1])
    return jnp.concatenate(outs, axis=0).T


def _mem_kv_kernel(mem_ref, g_ref, wkv_t_ref, kg_ref, mk_ref, mvt_ref):
    mem_n = _rms_rows(mem_ref[0], g_ref[...]).astype(BF16)
    kv_t = lax.dot_general(wkv_t_ref[...], mem_n, NT_DIMS, preferred_element_type=F32)
    k_heads = [_head_rms_t(kv_t[h * HEAD_DIM:(h + 1) * HEAD_DIM], kg_ref[...]) for h in range(MEM_HEADS)]
    mk_ref[0] = jnp.concatenate(k_heads, axis=0).T.astype(BF16)
    mvt_ref[0] = kv_t[MEM_WIDTH:].astype(BF16)


def _mem_kv(mem, g, wkv_t, k_gain):
    B, M, D = mem.shape
    return pl.pallas_call(
        _mem_kv_kernel,
        grid=(B, 1),
        in_specs=[
            pl.BlockSpec((1, M, D), lambda b, _: (b, 0, 0)),
            pl.BlockSpec((1, D), lambda b, _: (0, 0)),
            pl.BlockSpec((2 * MEM_WIDTH, D), lambda b, _: (0, 0)),
            pl.BlockSpec((HEAD_DIM, 1), lambda b, _: (0, 0)),
        ],
        out_specs=[
            pl.BlockSpec((1, M, MEM_WIDTH), lambda b, _: (b, 0, 0)),
            pl.BlockSpec((1, MEM_WIDTH, M), lambda b, _: (b, 0, 0)),
        ],
        out_shape=[
            jax.ShapeDtypeStruct((B, M, MEM_WIDTH), BF16),
            jax.ShapeDtypeStruct((B, MEM_WIDTH, M), BF16),
        ],
        compiler_params=_params(),
        name="mem_kv",
    )(mem, g, wkv_t, k_gain)


def _pool_in_kernel(x_ref, g_ref, w_tok_ref, wm_t_ref, wg_ref, scale_ref, mqg_ref, mk_ref, mvt_ref,
                    tok_ref, mo_ref, halo_ref):
    t = pl.program_id(1)
    T = x_ref.shape[1]

    @pl.when(t == 0)
    def _():
        halo_ref[...] = jnp.zeros_like(halo_ref)

    u = _rms_rows(x_ref[0], g_ref[...]).astype(BF16)
    h = jnp.dot(u, w_tok_ref[...], preferred_element_type=F32)
    qm_t = lax.dot_general(wm_t_ref[...], u, NT_DIMS, preferred_element_type=F32)

    ext = jnp.concatenate([halo_ref[...], h], axis=0)
    halo_ref[...] = h[T - POOL_HALO:]
    s2 = ext + pltpu.roll(ext, 1, 0)
    s4 = s2 + pltpu.roll(s2, 2, 0)
    s8 = s4 + pltpu.roll(s4, 4, 0)
    s16 = s8 + pltpu.roll(s8, 8, 0)
    lane = lax.broadcasted_iota(jnp.int32, (T, TOK_WIDTH), 1)
    pos = t * T + lax.broadcasted_iota(jnp.int32, (T, TOK_WIDTH), 0)
    sums = (s2, s4, s8, s16)
    win_sum = sums[-1][POOL_HALO:]
    window = jnp.full((T, TOK_WIDTH), POOL_WINDOWS[-1], jnp.int32)
    for gi in range(len(POOL_WINDOWS) - 2, -1, -1):
        in_group = lane < (gi + 1) * POOL_GROUP_WIDTH
        win_sum = jnp.where(in_group, sums[gi][POOL_HALO:], win_sum)
        window = jnp.where(in_group, POOL_WINDOWS[gi], window)
    count = jnp.minimum(pos + 1, window).astype(F32)
    d = (win_sum / count - h).astype(BF16)
    y = jnp.dot(d, wg_ref[...], preferred_element_type=F32) * scale_ref[...]
    tok_ref[0] = y.astype(BF16)

    mo_ref[0] = _mem_attention_t(qm_t, mqg_ref[...], mk_ref, mvt_ref).astype(BF16)


def _pool_in(x, g, w_tok, wm_t, wg, scale, mq_gain, mk, mvt):
    B, S, D = x.shape
    M = mk.shape[1]
    T = min(TOKEN_TILE, S)
    const = lambda b, t: (0, 0)
    return pl.pallas_call(
        _pool_in_kernel,
        grid=(B, S // T),
        in_specs=[
            pl.BlockSpec((1, T, D), lambda b, t: (b, t, 0)),
            pl.BlockSpec((1, D), const),
            pl.BlockSpec((D, TOK_WIDTH), const),
            pl.BlockSpec((MEM_WIDTH, D), const),
            pl.BlockSpec((TOK_WIDTH, TOK_WIDTH), const),
            pl.BlockSpec((1, TOK_WIDTH), const),
            pl.BlockSpec((HEAD_DIM, 1), const),
            pl.BlockSpec((1, M, MEM_WIDTH), lambda b, t: (b, 0, 0)),
            pl.BlockSpec((1, MEM_WIDTH, M), lambda b, t: (b, 0, 0)),
        ],
        out_specs=[
            pl.BlockSpec((1, T, TOK_WIDTH), lambda b, t: (b, t, 0)),
            pl.BlockSpec((1, T, MEM_WIDTH), lambda b, t: (b, t, 0)),
        ],
        out_shape=[
            jax.ShapeDtypeStruct((B, S, TOK_WIDTH), BF16),
            jax.ShapeDtypeStruct((B, S, MEM_WIDTH), BF16),
        ],
        scratch_shapes=[pltpu.VMEM((POOL_HALO, TOK_WIDTH), F32)],
        compiler_params=_params(),
        name="pool_in",
    )(x, g, w_tok, wm_t, wg, scale, mq_gain, mk, mvt)


def _selection_bias(gate, t, T):
    nb, width = gate.shape
    blk = lax.broadcasted_iota(jnp.int32, (nb, width), 0)
    pos = t * T + lax.rem(lax.broadcasted_iota(jnp.int32, (nb, width), 1), T)
    own = lax.shift_right_logical(pos, MOBA_BLOCK.bit_length() - 1)
    g = jnp.where(blk < own, gate, -jnp.inf)
    bias = jnp.full((nb, width), NEG, F32)
    for _ in range(MOBA_TOPK):
        mx = jnp.max(g, axis=0, keepdims=True)
        idx = jnp.min(jnp.where(g == mx, blk, nb), axis=0, keepdims=True)
        idx = jnp.where(mx > -jnp.inf, idx, -1)
        pick = blk == idx
        bias = jnp.where(pick, 0.0, bias)
        g = jnp.where(pick, -jnp.inf, g)
    return bias


def _moba_in_kernel(x_ref, g_ref, wqkv_t_ref, wm_t_ref, cos_ref, sin_ref, qg_ref, kg_ref, mqg_ref,
                    mk_ref, mvt_ref, qt_ref, k_ref, vt_ref, bias_ref, mo_ref, kmean_ref):
    t = pl.program_id(1)
    T = x_ref.shape[1]
    blocks_per_tile = T // MOBA_BLOCK

    @pl.when(t == 0)
    def _():
        kmean_ref[...] = jnp.zeros_like(kmean_ref)

    u = _rms_rows(x_ref[0], g_ref[...]).astype(BF16)
    cos, sin = cos_ref[...], sin_ref[...]

    def proj_t(lo, hi):
        return lax.dot_general(wqkv_t_ref[lo:hi, :], u, NT_DIMS, preferred_element_type=F32)

    k_t = proj_t(TOK_WIDTH, 2 * TOK_WIDTH)
    q_t = proj_t(0, TOK_WIDTH)

    k_heads = [_rope_t(_head_rms_t(k_t[h * HEAD_DIM:(h + 1) * HEAD_DIM], kg_ref[...]), cos, sin)
               for h in range(MOBA_HEADS)]
    k_nat = jnp.concatenate(k_heads, axis=0).T
    k_ref[0] = k_nat.astype(BF16)
    for c in range(blocks_per_tile):
        kmean_ref[pl.ds(t * blocks_per_tile + c, 1), :] = jnp.mean(
            k_nat[c * MOBA_BLOCK:(c + 1) * MOBA_BLOCK], axis=0, keepdims=True)

    km_hi, km_lo = _split_bf16(kmean_ref[...])
    heads_per_chunk = MOBA_HEADS // VALUE_CHUNKS
    chunk_rows = TOK_WIDTH // VALUE_CHUNKS
    for chunk in range(VALUE_CHUNKS):
        lo = 2 * TOK_WIDTH + chunk * chunk_rows
        v_t = proj_t(lo, lo + chunk_rows)
        for c in range(blocks_per_tile):
            vt_ref[0, c, chunk * chunk_rows:(chunk + 1) * chunk_rows, :] = (
                v_t[:, c * MOBA_BLOCK:(c + 1) * MOBA_BLOCK].astype(BF16))
        heads = range(chunk * heads_per_chunk, (chunk + 1) * heads_per_chunk)
        gates = []
        for h in heads:
            rows = slice(h * HEAD_DIM, (h + 1) * HEAD_DIM)
            pair, slot = divmod(h, 2)
            lanes = slice(pair * LANES, (pair + 1) * LANES)
            q = _rope_t(_head_rms_t(q_t[rows], qg_ref[...]), cos, sin)
            qt_ref[0, rows, :] = (q * (SM_SCALE * LOG2_E)).astype(BF16)
            q_hi, q_lo = _split_bf16(q)
            q_hi2, q_lo2 = _pair_operand(q_hi, slot), _pair_operand(q_lo, slot)
            gates.append(jnp.dot(jnp.concatenate([km_hi[:, lanes], km_lo[:, lanes]], axis=1),
                                 jnp.concatenate([q_hi2, q_hi2], axis=0), preferred_element_type=F32)
                         + jnp.dot(km_hi[:, lanes], q_lo2, preferred_element_type=F32))
        bias = _selection_bias(jnp.concatenate(gates, axis=1), t, T)
        for n, h in enumerate(heads):
            bias_ref[0, h] = bias[:, n * T:(n + 1) * T]

    qm_t = lax.dot_general(wm_t_ref[...], u, NT_DIMS, preferred_element_type=F32)
    mo_ref[0] = _mem_attention_t(qm_t, mqg_ref[...], mk_ref, mvt_ref).astype(BF16)


def _moba_in(x, g, wqkv_t, wm_t, cos_t, sin_t, q_gain, k_gain, mq_gain, mk, mvt):
    B, S, D = x.shape
    M = mk.shape[1]
    T = min(TOKEN_TILE, S)
    nb = S // MOBA_BLOCK
    const = lambda b, t: (0, 0)
    return pl.pallas_call(
        _moba_in_kernel,
        grid=(B, S // T),
        in_specs=[
            pl.BlockSpec((1, T, D), lambda b, t: (b, t, 0)),
            pl.BlockSpec((1, D), const),
            pl.BlockSpec((3 * TOK_WIDTH, D), const),
            pl.BlockSpec((MEM_WIDTH, D), const),
            pl.BlockSpec((HALF_DIM, T), lambda b, t: (0, t)),
            pl.BlockSpec((HALF_DIM, T), lambda b, t: (0, t)),
            pl.BlockSpec((HEAD_DIM, 1), const),
            pl.BlockSpec((HEAD_DIM, 1), const),
            pl.BlockSpec((HEAD_DIM, 1), const),
            pl.BlockSpec((1, M, MEM_WIDTH), lambda b, t: (b, 0, 0)),
            pl.BlockSpec((1, MEM_WIDTH, M), lambda b, t: (b, 0, 0)),
        ],
        out_specs=[
            pl.BlockSpec((1, TOK_WIDTH, T), lambda b, t: (b, 0, t)),
            pl.BlockSpec((1, T, TOK_WIDTH), lambda b, t: (b, t, 0)),
            pl.BlockSpec((1, T // MOBA_BLOCK, TOK_WIDTH, MOBA_BLOCK), lambda b, t: (b, t, 0, 0)),
            pl.BlockSpec((1, MOBA_HEADS, nb, T), lambda b, t: (b, 0, 0, t)),
            pl.BlockSpec((1, T, MEM_WIDTH), lambda b, t: (b, t, 0)),
        ],
        out_shape=[
            jax.ShapeDtypeStruct((B, TOK_WIDTH, S), BF16),
            jax.ShapeDtypeStruct((B, S, TOK_WIDTH), BF16),
            jax.ShapeDtypeStruct((B, nb, TOK_WIDTH, MOBA_BLOCK), BF16),
            jax.ShapeDtypeStruct((B, MOBA_HEADS, nb, S), F32),
            jax.ShapeDtypeStruct((B, S, MEM_WIDTH), BF16),
        ],
        scratch_shapes=[pltpu.VMEM((nb, TOK_WIDTH), F32)],
        compiler_params=_params(),
        name="moba_in",
    )(x, g, wqkv_t, wm_t, cos_t, sin_t, q_gain, k_gain, mq_gain, mk, mvt)


def _moba_attn_kernel(qt_ref, k_ref, vt_ref, bias_ref, onehot_ref, o_ref, *scratch):
    s_refs = (scratch[0:2], scratch[2:4])
    bm_refs = (scratch[4:6], scratch[6:8])
    q_scr, bias_scr, state_m, state_acc = scratch[8:12]
    TQ = MOBA_BLOCK
    group_keys = KV_GROUP * MOBA_BLOCK
    total_groups = k_ref.shape[1] // group_keys
    first_tile = pl.program_id(2) * TILES_PER_STEP

    live = [(first_tile + s + KV_GROUP - 1) // KV_GROUP for s in range(TILES_PER_STEP)]
    starts = [sum(live[:s]) for s in range(TILES_PER_STEP)]
    total_items = sum(live)

    def locate(e):
        slot = jnp.int32(0)
        start = jnp.int32(0)
        for s in range(1, TILES_PER_STEP):
            here = e >= starts[s]
            slot = slot + here.astype(jnp.int32)
            start = jnp.where(here, starts[s], start)
        return slot, e - start, e >= total_items

    zeros_tail = jnp.zeros((MXU_DEPTH - LANES - BIAS_ROWS, TQ), BF16)
    ones_rows = jnp.ones((SUM_ROWS, 1), BF16)

    def pv_operand(v_t):
        return jnp.concatenate([v_t, jnp.broadcast_to(ones_rows, (SUM_ROWS, v_t.shape[1]))], axis=0)

    def produce(item, parity, a):
        slot, group, past_end = item
        gc = jnp.minimum(group, total_groups - 1)
        k_g = k_ref[0, pl.ds(pl.multiple_of(gc * group_keys, group_keys), group_keys), :]
        lhs = jnp.concatenate([k_g, onehot_ref[gc % 2]], axis=1)
        penalty = jnp.where(past_end, NEG, 0.0)
        slab_row = pl.multiple_of((gc // 2) * 2 * KV_GROUP, 2 * KV_GROUP)
        slab = bias_scr[slot, a, pl.ds(slab_row, 2 * KV_GROUP), :] + penalty
        slab = jnp.concatenate([slab, jnp.zeros_like(slab)], axis=0).astype(BF16)
        rhs = jnp.concatenate([q_scr[slot, a], slab, zeros_tail], axis=0)
        s = jnp.dot(lhs, rhs, preferred_element_type=F32)
        s_refs[parity][a][...] = s
        bm_refs[parity][a][...] = jnp.max(s, axis=0, keepdims=True)

    def consume(item, parity, a):
        slot, group, _ = item
        j0 = jnp.minimum(group, total_groups - 1) * KV_GROUP
        rows = slice(a * HEAD_DIM, (a + 1) * HEAD_DIM)
        v_g = jnp.concatenate([vt_ref[0, j0 + c, rows, :] for c in range(KV_GROUP)], axis=1)
        m = state_m[slot, a]
        m_new = jnp.maximum(m, bm_refs[parity][a][...])
        alpha = jnp.exp2(m - m_new)
        p = jnp.exp2(s_refs[parity][a][...] - m_new).astype(BF16)
        pv = jnp.dot(pv_operand(v_g), p, preferred_element_type=F32)
        state_m[slot, a] = m_new
        state_acc[slot, a] = alpha * state_acc[slot, a] + pv

    key_idx = lax.broadcasted_iota(jnp.int32, (MOBA_BLOCK, TQ), 0)
    qry_idx = lax.broadcasted_iota(jnp.int32, (MOBA_BLOCK, TQ), 1)
    causal = key_idx <= qry_idx
    own = []
    for s in range(TILES_PER_STEP):
        cols = slice(s * TQ, (s + 1) * TQ)
        q_pair = qt_ref[0, :, cols]
        row = lax.broadcasted_iota(jnp.int32, q_pair.shape, 0)
        zero = jnp.zeros_like(q_pair)
        q_heads = (jnp.where(row < HEAD_DIM, q_pair, zero), jnp.where(row >= HEAD_DIM, q_pair, zero))
        k_own = k_ref[0, pl.ds(pl.multiple_of((first_tile + s) * MOBA_BLOCK, MOBA_BLOCK), MOBA_BLOCK), :]
        for a in range(2):
            q_scr[s, a] = q_heads[a]
            bias_scr[s, a] = bias_ref[0, a, :, cols]
        own.append([jnp.where(causal, jnp.dot(k_own, q_heads[a], preferred_element_type=F32), NEG)
                    for a in range(2)])
    first_item = locate(jnp.int32(0))
    for a in range(2):
        produce(first_item, 0, a)
    for s in range(TILES_PER_STEP):
        for a in range(2):
            rows = slice(a * HEAD_DIM, (a + 1) * HEAD_DIM)
            m = jnp.max(own[s][a], axis=0, keepdims=True)
            p = jnp.exp2(own[s][a] - m).astype(BF16)
            state_m[s, a] = m
            state_acc[s, a] = jnp.dot(pv_operand(vt_ref[0, first_tile + s, rows, :]), p,
                                      preferred_element_type=F32)

    def run_items(items_per_step, first, n_steps):
        def body(step, carry):
            e = first + items_per_step * step
            for sub in range(items_per_step):
                cur = sub % 2
                this_item, next_item = locate(e + sub), locate(e + sub + 1)
                for a in range(2):
                    produce(next_item, 1 - cur, a)
                    consume(this_item, cur, a)
            return carry
        lax.fori_loop(0, n_steps, body, 0)

    done = 0
    for size in ITEM_STEPS[:-1]:
        n_steps = (total_items - done) // size
        run_items(size, done, n_steps)
        done = done + n_steps * size
    run_items(ITEM_STEPS[-1], done, (total_items - done + ITEM_STEPS[-1] - 1) // ITEM_STEPS[-1])

    for s in range(TILES_PER_STEP):
        heads = []
        for a in range(2):
            acc = state_acc[s, a]
            heads.append(acc[:HEAD_DIM] / acc[HEAD_DIM:HEAD_DIM + 1])
        o_ref[0, s * TQ:(s + 1) * TQ, :] = jnp.concatenate(heads, axis=0).T.astype(BF16)


def _block_onehot():
    key_block = jnp.arange(KV_GROUP * MOBA_BLOCK, dtype=jnp.int32) // MOBA_BLOCK
    col = jnp.arange(LANES, dtype=jnp.int32)
    parity = jnp.arange(2, dtype=jnp.int32)
    hit = col[None, None, :] == (parity[:, None, None] * KV_GROUP + key_block[None, :, None])
    return hit.astype(BF16)


def _moba_attn(qt, k, vt, bias):
    B, _, S = qt.shape
    nb = S // MOBA_BLOCK
    tq = TILES_PER_STEP * MOBA_BLOCK
    assert nb % TILES_PER_STEP == 0
    assert 2 * KV_GROUP == F32_SUBLANES and nb % F32_SUBLANES == 0, "bias slabs are whole f32 sublane tiles"
    assert all(size % 2 == 0 for size in ITEM_STEPS), "score buffers alternate by item parity"
    pairs = MOBA_HEADS // 2
    group_keys = KV_GROUP * MOBA_BLOCK
    return pl.pallas_call(
        _moba_attn_kernel,
        grid=(B, pairs, nb // TILES_PER_STEP),
        in_specs=[
            pl.BlockSpec((1, LANES, tq), lambda b, p, i: (b, p, i)),
            pl.BlockSpec((1, S, LANES), lambda b, p, i: (b, 0, p)),
            pl.BlockSpec((1, nb, LANES, MOBA_BLOCK), lambda b, p, i: (b, 0, p, 0)),
            pl.BlockSpec((1, 2, nb, tq), lambda b, p, i: (b, p, 0, i)),
            pl.BlockSpec((2, group_keys, LANES), lambda b, p, i: (0, 0, 0)),
        ],
        out_specs=pl.BlockSpec((1, tq, LANES), lambda b, p, i: (b, i, p)),
        out_shape=jax.ShapeDtypeStruct((B, S, TOK_WIDTH), BF16),
        scratch_shapes=([pltpu.VMEM((group_keys, MOBA_BLOCK), F32)] * 4 + [pltpu.VMEM((1, MOBA_BLOCK), F32)] * 4
                        + [pltpu.VMEM((TILES_PER_STEP, 2, LANES, MOBA_BLOCK), BF16),
                           pltpu.VMEM((TILES_PER_STEP, 2, nb, MOBA_BLOCK), F32),
                           pltpu.VMEM((TILES_PER_STEP, 2, 1, MOBA_BLOCK), F32),
                           pltpu.VMEM((TILES_PER_STEP, 2, HEAD_DIM + SUM_ROWS, MOBA_BLOCK), F32)]),
        compiler_params=pltpu.CompilerParams(
            dimension_semantics=("arbitrary", "arbitrary", "arbitrary"), vmem_limit_bytes=VMEM_LIMIT_BYTES),
        name="moba_attn",
    )(qt, k, vt, bias, _block_onehot())


def _post_kernel(x_ref, tok_ref, mo_ref, wo_ref, g_ref, w1_ref, w2_ref, o_ref):
    mixed = jnp.concatenate([tok_ref[0], mo_ref[0]], axis=-1)
    x1 = x_ref[0] + jnp.dot(mixed, wo_ref[...], preferred_element_type=F32)
    u = _rms_rows(x1, g_ref[...]).astype(BF16)
    h = jnp.dot(u, w1_ref[...], preferred_element_type=F32)
    h = jnp.square(jnp.maximum(h, 0.0)).astype(BF16)
    o_ref[0] = x1 + jnp.dot(h, w2_ref[...], preferred_element_type=F32)


def _post(x, tok, mo, wo, g, w1, w2):
    B, S, D = x.shape
    T = min(TOKEN_TILE, S)
    const = lambda b, t: (0, 0)
    resident = functools.partial(pl.BlockSpec, index_map=const, pipeline_mode=pl.Buffered(1))
    return pl.pallas_call(
        _post_kernel,
        grid=(B, S // T),
        in_specs=[
            pl.BlockSpec((1, T, D), lambda b, t: (b, t, 0)),
            pl.BlockSpec((1, T, TOK_WIDTH), lambda b, t: (b, t, 0)),
            pl.BlockSpec((1, T, MEM_WIDTH), lambda b, t: (b, t, 0)),
            resident((D, D)),
            pl.BlockSpec((1, D), const),
            resident((D, D_FF)),
            resident((D_FF, D)),
        ],
        out_specs=pl.BlockSpec((1, T, D), lambda b, t: (b, t, 0)),
        out_shape=jax.ShapeDtypeStruct((B, S, D), F32),
        compiler_params=_params(),
        name="post",
    )(x, tok, mo, wo, g, w1, w2)


def _rope_tables_t(seq_len):
    pos = jnp.arange(seq_len, dtype=F32)
    inv = ROPE_THETA ** (-jnp.arange(0, HEAD_DIM, 2, dtype=F32) / HEAD_DIM)
    ang = pos[:, None] * inv[None, :]
    return jnp.cos(ang).T, jnp.sin(ang).T


def _block_diag(w_group):
    G, C, _ = w_group.shape
    eye = jnp.eye(G, dtype=w_group.dtype)
    return (eye[:, None, :, None] * w_group[:, :, None, :]).reshape(G * C, G * C)


def kernel(x, mem, g_mix, g_mem, g_mlp, w_in_pool, w_pool_group, pool_scale, w_in_moba, moba_q_gain,
           moba_k_gain, w_mem_kv, mem_q_gain, mem_k_gain, w_out, w_ff1, w_ff2):
    depth = g_mix.shape[0]
    S = x.shape[1]
    cos_t, sin_t = _rope_tables_t(S)
    col = lambda v: v.reshape(-1, 1)
    row = lambda v: v.reshape(1, -1)
    for i in range(depth):
        j = i // 2
        mk, mvt = _mem_kv(mem, row(g_mem[i]), w_mem_kv[i].T.astype(BF16), col(mem_k_gain[i]))
        if i % 2 == 0:
            w = w_in_pool[j]
            tok, mo = _pool_in(
                x, row(g_mix[i]), w[:, :TOK_WIDTH].astype(BF16), w[:, TOK_WIDTH:].T.astype(BF16),
                _block_diag(w_pool_group[j]).astype(BF16), row(pool_scale[j]), col(mem_q_gain[i]), mk, mvt)
        else:
            w = w_in_moba[j]
            qt, k, vt, bias, mo = _moba_in(
                x, row(g_mix[i]), w[:, :3 * TOK_WIDTH].T.astype(BF16), w[:, 3 * TOK_WIDTH:].T.astype(BF16),
                cos_t, sin_t, col(moba_q_gain[j]), col(moba_k_gain[j]), col(mem_q_gain[i]), mk, mvt)
            tok = _moba_attn(qt, k, vt, bias)
        x = _post(x, tok, mo, w_out[i].astype(BF16), row(g_mlp[i]), w_ff1[i].astype(BF16), w_ff2[i].astype(BF16))
    return x
```

```python
import functools

import jax
import jax.numpy as jnp
from jax import lax
from jax.experimental import pallas as pl
from jax.experimental.pallas import tpu as pltpu

D_MODEL = 1024
HEAD_DIM = 64
HALF_DIM = HEAD_DIM // 2
MEM_HEADS = 4
MEM_WIDTH = MEM_HEADS * HEAD_DIM
TOK_WIDTH = D_MODEL - MEM_WIDTH
MOBA_HEADS = TOK_WIDTH // HEAD_DIM
MOBA_BLOCK = 256
MOBA_TOPK = 3
POOL_WINDOWS = (2, 4, 8, 16)
POOL_GROUP_WIDTH = TOK_WIDTH // len(POOL_WINDOWS)
POOL_HALO = 16
D_FF = 4 * D_MODEL
ROPE_THETA = 10000.0
EPS = 1e-6
NEG = -1e30
SM_SCALE = HEAD_DIM ** -0.5
LOG2_E = 1.4426950408889634

LANES = 128
F32_SUBLANES = 8
BIAS_ROWS = 16
SUM_ROWS = 16
MXU_DEPTH = 256
VMEM_LIMIT_BYTES = 56 * 1024 * 1024

TOKEN_TILE = 512
VALUE_CHUNKS = 3
TILES_PER_STEP = 16
KV_GROUP = 4
ITEM_STEPS = (8, 4, 2)

F32 = jnp.float32
BF16 = jnp.bfloat16
NT_DIMS = (((1,), (1,)), ((), ()))


def _params():
    return pltpu.CompilerParams(
        dimension_semantics=("arbitrary", "arbitrary"), vmem_limit_bytes=VMEM_LIMIT_BYTES)


def _rms_rows(x, g):
    ms = jnp.mean(x * x, axis=-1, keepdims=True)
    return x * lax.rsqrt(ms + EPS) * g


def _head_rms_t(h, gain_col):
    ms = jnp.mean(h * h, axis=0, keepdims=True)
    return h * lax.rsqrt(ms + EPS) * gain_col


def _rope_t(h, cos, sin):
    h1, h2 = h[:HALF_DIM], h[HALF_DIM:]
    return jnp.concatenate([h1 * cos - h2 * sin, h2 * cos + h1 * sin], axis=0)


def _split_bf16(x):
    hi = x.astype(BF16)
    return hi, (x - hi.astype(F32)).astype(BF16)


def _pair_operand(q_bf, slot):
    z = jnp.zeros_like(q_bf)
    return jnp.concatenate([q_bf, z] if slot == 0 else [z, q_bf], axis=0)


def _mem_attention_t(qm_t, mq_gain, mk_ref, mvt_ref):
    scores = []
    for h in range(MEM_HEADS):
        pair, slot = divmod(h, 2)
        q = _head_rms_t(qm_t[h * HEAD_DIM:(h + 1) * HEAD_DIM], mq_gain)
        q2 = _pair_operand((q * (SM_SCALE * LOG2_E)).astype(BF16), slot)
        k_pair = mk_ref[0, :, pair * LANES:(pair + 1) * LANES]
        scores.append(jnp.dot(k_pair, q2, preferred_element_type=F32))
    probs = [jnp.exp2(s - jnp.max(s, axis=0, keepdims=True)).astype(BF16) for s in scores]
    outs = []
    for h in range(MEM_HEADS):
        v_t = mvt_ref[0, h * HEAD_DIM:(h + 1) * HEAD_DIM, :]
        ones = jnp.ones((SUM_ROWS, v_t.shape[1]), BF16)
        r = jnp.dot(jnp.concatenate([v_t, ones], axis=0), probs[h], preferred_element_type=F32)
        outs.append(r[:HEAD_DIM] / r[HEAD_DIM:HEAD_DIM + 1])
    return jnp.concatenate(outs, axis=0).T


def _mem_kv_kernel(mem_ref, g_ref, wkv_t_ref, kg_ref, mk_ref, mvt_ref):
    mem_n = _rms_rows(mem_ref[0], g_ref[...]).astype(BF16)
    kv_t = lax.dot_general(wkv_t_ref[...], mem_n, NT_DIMS, preferred_element_type=F32)
    k_heads = [_head_rms_t(kv_t[h * HEAD_DIM:(h + 1) * HEAD_DIM], kg_ref[...]) for h in range(MEM_HEADS)]
    mk_ref[0] = jnp.concatenate(k_heads, axis=0).T.astype(BF16)
    mvt_ref[0] = kv_t[MEM_WIDTH:].astype(BF16)


def _mem_kv(mem, g, wkv_t, k_gain):
    B, M, D = mem.shape
    return pl.pallas_call(
        _mem_kv_kernel,
        grid=(B, 1),
        in_specs=[
            pl.BlockSpec((1, M, D), lambda b, _: (b, 0, 0)),
            pl.BlockSpec((1, D), lambda b, _: (0, 0)),
            pl.BlockSpec((2 * MEM_WIDTH, D), lambda b, _: (0, 0)),
            pl.BlockSpec((HEAD_DIM, 1), lambda b, _: (0, 0)),
        ],
        out_specs=[
            pl.BlockSpec((1, M, MEM_WIDTH), lambda b, _: (b, 0, 0)),
            pl.BlockSpec((1, MEM_WIDTH, M), lambda b, _: (b, 0, 0)),
        ],
        out_shape=[
            jax.ShapeDtypeStruct((B, M, MEM_WIDTH), BF16),
            jax.ShapeDtypeStruct((B, MEM_WIDTH, M), BF16),
        ],
        compiler_params=_params(),
        name="mem_kv",
    )(mem, g, wkv_t, k_gain)


def _pool_in_kernel(x_ref, g_ref, w_tok_ref, wm_t_ref, wg_ref, scale_ref, mqg_ref, mk_ref, mvt_ref,
                    tok_ref, mo_ref, halo_ref):
    t = pl.program_id(1)
    T = x_ref.shape[1]

    @pl.when(t == 0)
    def _():
        halo_ref[...] = jnp.zeros_like(halo_ref)

    u = _rms_rows(x_ref[0], g_ref[...]).astype(BF16)
    h = jnp.dot(u, w_tok_ref[...], preferred_element_type=F32)
    qm_t = lax.dot_general(wm_t_ref[...], u, NT_DIMS, preferred_element_type=F32)

    ext = jnp.concatenate([halo_ref[...], h], axis=0)
    halo_ref[...] = h[T - POOL_HALO:]
    s2 = ext + pltpu.roll(ext, 1, 0)
    s4 = s2 + pltpu.roll(s2, 2, 0)
    s8 = s4 + pltpu.roll(s4, 4, 0)
    s16 = s8 + pltpu.roll(s8, 8, 0)
    lane = lax.broadcasted_iota(jnp.int32, (T, TOK_WIDTH), 1)
    pos = t * T + lax.broadcasted_iota(jnp.int32, (T, TOK_WIDTH), 0)
    sums = (s2, s4, s8, s16)
    win_sum = sums[-1][POOL_HALO:]
    window = jnp.full((T, TOK_WIDTH), POOL_WINDOWS[-1], jnp.int32)
    for gi in range(len(POOL_WINDOWS) - 2, -1, -1):
        in_group = lane < (gi + 1) * POOL_GROUP_WIDTH
        win_sum = jnp.where(in_group, sums[gi][POOL_HALO:], win_sum)
        window = jnp.where(in_group, POOL_WINDOWS[gi], window)
    count = jnp.minimum(pos + 1, window).astype(F32)
    d = (win_sum / count - h).astype(BF16)
    y = jnp.dot(d, wg_ref[...], preferred_element_type=F32) * scale_ref[...]
    tok_ref[0] = y.astype(BF16)

    mo_ref[0] = _mem_attention_t(qm_t, mqg_ref[...], mk_ref, mvt_ref).astype(BF16)


def _pool_in(x, g, w_tok, wm_t, wg, scale, mq_gain, mk, mvt):
    B, S, D = x.shape
    M = mk.shape[1]
    T = min(TOKEN_TILE, S)
    const = lambda b, t: (0, 0)
    return pl.pallas_call(
        _pool_in_kernel,
        grid=(B, S // T),
        in_specs=[
            pl.BlockSpec((1, T, D), lambda b, t: (b, t, 0)),
            pl.BlockSpec((1, D), const),
            pl.BlockSpec((D, TOK_WIDTH), const),
            pl.BlockSpec((MEM_WIDTH, D), const),
            pl.BlockSpec((TOK_WIDTH, TOK_WIDTH), const),
            pl.BlockSpec((1, TOK_WIDTH), const),
            pl.BlockSpec((HEAD_DIM, 1), const),
            pl.BlockSpec((1, M, MEM_WIDTH), lambda b, t: (b, 0, 0)),
            pl.BlockSpec((1, MEM_WIDTH, M), lambda b, t: (b, 0, 0)),
        ],
        out_specs=[
            pl.BlockSpec((1, T, TOK_WIDTH), lambda b, t: (b, t, 0)),
            pl.BlockSpec((1, T, MEM_WIDTH), lambda b, t: (b, t, 0)),
        ],
        out_shape=[
            jax.ShapeDtypeStruct((B, S, TOK_WIDTH), BF16),
            jax.ShapeDtypeStruct((B, S, MEM_WIDTH), BF16),
        ],
        scratch_shapes=[pltpu.VMEM((POOL_HALO, TOK_WIDTH), F32)],
        compiler_params=_params(),
        name="pool_in",
    )(x, g, w_tok, wm_t, wg, scale, mq_gain, mk, mvt)


def _selection_bias(gate, t, T):
    nb, width = gate.shape
    blk = lax.broadcasted_iota(jnp.int32, (nb, width), 0)
    pos = t * T + lax.rem(lax.broadcasted_iota(jnp.int32, (nb, width), 1), T)
    own = lax.shift_right_logical(pos, MOBA_BLOCK.bit_length() - 1)
    g = jnp.where(blk < own, gate, -jnp.inf)
    bias = jnp.full((nb, width), NEG, F32)
    for _ in range(MOBA_TOPK):
        mx = jnp.max(g, axis=0, keepdims=True)
        idx = jnp.min(jnp.where(g == mx, blk, nb), axis=0, keepdims=True)
        idx = jnp.where(mx > -jnp.inf, idx, -1)
        pick = blk == idx
        bias = jnp.where(pick, 0.0, bias)
        g = jnp.where(pick, -jnp.inf, g)
    return bias


def _moba_in_kernel(x_ref, g_ref, wqkv_t_ref, wm_t_ref, cos_ref, sin_ref, qg_ref, kg_ref, mqg_ref,
                    mk_ref, mvt_ref, qt_ref, k_ref, vt_ref, bias_ref, mo_ref, kmean_ref):
    t = pl.program_id(1)
    T = x_ref.shape[1]
    blocks_per_tile = T // MOBA_BLOCK

    @pl.when(t == 0)
    def _():
        kmean_ref[...] = jnp.zeros_like(kmean_ref)

    u = _rms_rows(x_ref[0], g_ref[...]).astype(BF16)
    cos, sin = cos_ref[...], sin_ref[...]

    def proj_t(lo, hi):
        return lax.dot_general(wqkv_t_ref[lo:hi, :], u, NT_DIMS, preferred_element_type=F32)

    k_t = proj_t(TOK_WIDTH, 2 * TOK_WIDTH)
    q_t = proj_t(0, TOK_WIDTH)

    k_heads = [_rope_t(_head_rms_t(k_t[h * HEAD_DIM:(h + 1) * HEAD_DIM], kg_ref[...]), cos, sin)
               for h in range(MOBA_HEADS)]
    k_nat = jnp.concatenate(k_heads, axis=0).T
    k_ref[0] = k_nat.astype(BF16)
    for c in range(blocks_per_tile):
        kmean_ref[pl.ds(t * blocks_per_tile + c, 1), :] = jnp.mean(
            k_nat[c * MOBA_BLOCK:(c + 1) * MOBA_BLOCK], axis=0, keepdims=True)

    km_hi, km_lo = _split_bf16(kmean_ref[...])
    heads_per_chunk = MOBA_HEADS // VALUE_CHUNKS
    chunk_rows = TOK_WIDTH // VALUE_CHUNKS
    for chunk in range(VALUE_CHUNKS):
        lo = 2 * TOK_WIDTH + chunk * chunk_rows
        v_t = proj_t(lo, lo + chunk_rows)
        for c in range(blocks_per_tile):
            vt_ref[0, c, chunk * chunk_rows:(chunk + 1) * chunk_rows, :] = (
                v_t[:, c * MOBA_BLOCK:(c + 1) * MOBA_BLOCK].astype(BF16))
        heads = range(chunk * heads_per_chunk, (chunk + 1) * heads_per_chunk)
        gates = []
        for h in heads:
            rows = slice(h * HEAD_DIM, (h + 1) * HEAD_DIM)
            pair, slot = divmod(h, 2)
            lanes = slice(pair * LANES, (pair + 1) * LANES)
            q = _rope_t(_head_rms_t(q_t[rows], qg_ref[...]), cos, sin)
            qt_ref[0, rows, :] = (q * (SM_SCALE * LOG2_E)).astype(BF16)
            q_hi, q_lo = _split_bf16(q)
            q_hi2, q_lo2 = _pair_operand(q_hi, slot), _pair_operand(q_lo, slot)
            gates.append(jnp.dot(jnp.concatenate([km_hi[:, lanes], km_lo[:, lanes]], axis=1),
                                 jnp.concatenate([q_hi2, q_hi2], axis=0), preferred_element_type=F32)
                         + jnp.dot(km_hi[:, lanes], q_lo2, preferred_element_type=F32))
        bias = _selection_bias(jnp.concatenate(gates, axis=1), t, T)
        for n, h in enumerate(heads):
            bias_ref[0, h] = bias[:, n * T:(n + 1) * T]

    qm_t = lax.dot_general(wm_t_ref[...], u, NT_DIMS, preferred_element_type=F32)
    mo_ref[0] = _mem_attention_t(qm_t, mqg_ref[...], mk_ref, mvt_ref).astype(BF16)


def _moba_in(x, g, wqkv_t, wm_t, cos_t, sin_t, q_gain, k_gain, mq_gain, mk, mvt):
    B, S, D = x.shape
    M = mk.shape[1]
    T = min(TOKEN_TILE, S)
    nb = S // MOBA_BLOCK
    const = lambda b, t: (0, 0)
    return pl.pallas_call(
        _moba_in_kernel,
        grid=(B, S // T),
        in_specs=[
            pl.BlockSpec((1, T, D), lambda b, t: (b, t, 0)),
            pl.BlockSpec((1, D), const),
            pl.BlockSpec((3 * TOK_WIDTH, D), const),
            pl.BlockSpec((MEM_WIDTH, D), const),
            pl.BlockSpec((HALF_DIM, T), lambda b, t: (0, t)),
            pl.BlockSpec((HALF_DIM, T), lambda b, t: (0, t)),
            pl.BlockSpec((HEAD_DIM, 1), const),
            pl.BlockSpec((HEAD_DIM, 1), const),
            pl.BlockSpec((HEAD_DIM, 1), const),
            pl.BlockSpec((1, M, MEM_WIDTH), lambda b, t: (b, 0, 0)),
            pl.BlockSpec((1, MEM_WIDTH, M), lambda b, t: (b, 0, 0)),
        ],
        out_specs=[
            pl.BlockSpec((1, TOK_WIDTH, T), lambda b, t: (b, 0, t)),
            pl.BlockSpec((1, T, TOK_WIDTH), lambda b, t: (b, t, 0)),
            pl.BlockSpec((1, T // MOBA_BLOCK, TOK_WIDTH, MOBA_BLOCK), lambda b, t: (b, t, 0, 0)),
            pl.BlockSpec((1, MOBA_HEADS, nb, T), lambda b, t: (b, 0, 0, t)),
            pl.BlockSpec((1, T, MEM_WIDTH), lambda b, t: (b, t, 0)),
        ],
        out_shape=[
            jax.ShapeDtypeStruct((B, TOK_WIDTH, S), BF16),
            jax.ShapeDtypeStruct((B, S, TOK_WIDTH), BF16),
            jax.ShapeDtypeStruct((B, nb, TOK_WIDTH, MOBA_BLOCK), BF16),
            jax.ShapeDtypeStruct((B, MOBA_HEADS, nb, S), F32),
            jax.ShapeDtypeStruct((B, S, MEM_WIDTH), BF16),
        ],
        scratch_shapes=[pltpu.VMEM((nb, TOK_WIDTH), F32)],
        compiler_params=_params(),
        name="moba_in",
    )(x, g, wqkv_t, wm_t, cos_t, sin_t, q_gain, k_gain, mq_gain, mk, mvt)


def _moba_attn_kernel(qt_ref, k_ref, vt_ref, bias_ref, onehot_ref, o_ref, *scratch):
    s_refs = (scratch[0:2], scratch[2:4])
    bm_refs = (scratch[4:6], scratch[6:8])
    q_scr, bias_scr, state_m, state_acc = scratch[8:12]
    TQ = MOBA_BLOCK
    group_keys = KV_GROUP * MOBA_BLOCK
    total_groups = k_ref.shape[1] // group_keys
    first_tile = pl.program_id(2) * TILES_PER_STEP

    live = [(first_tile + s + KV_GROUP - 1) // KV_GROUP for s in range(TILES_PER_STEP)]
    starts = [sum(live[:s]) for s in range(TILES_PER_STEP)]
    total_items = sum(live)

    def locate(e):
        slot = jnp.int32(0)
        start = jnp.int32(0)
        for s in range(1, TILES_PER_STEP):
            here = e >= starts[s]
            slot = slot + here.astype(jnp.int32)
            start = jnp.where(here, starts[s], start)
        return slot, e - start, e >= total_items

    zeros_tail = jnp.zeros((MXU_DEPTH - LANES - BIAS_ROWS, TQ), BF16)
    ones_rows = jnp.ones((SUM_ROWS, 1), BF16)

    def pv_operand(v_t):
        return jnp.concatenate([v_t, jnp.broadcast_to(ones_rows, (SUM_ROWS, v_t.shape[1]))], axis=0)

    def produce(item, parity, a):
        slot, group, past_end = item
        gc = jnp.minimum(group, total_groups - 1)
        k_g = k_ref[0, pl.ds(pl.multiple_of(gc * group_keys, group_keys), group_keys), :]
        lhs = jnp.concatenate([k_g, onehot_ref[gc % 2]], axis=1)
        penalty = jnp.where(past_end, NEG, 0.0)
        slab_row = pl.multiple_of((gc // 2) * 2 * KV_GROUP, 2 * KV_GROUP)
        slab = bias_scr[slot, a, pl.ds(slab_row, 2 * KV_GROUP), :] + penalty
        slab = jnp.concatenate([slab, jnp.zeros_like(slab)], axis=0).astype(BF16)
        rhs = jnp.concatenate([q_scr[slot, a], slab, zeros_tail], axis=0)
        s = jnp.dot(lhs, rhs, preferred_element_type=F32)
        s_refs[parity][a][...] = s
        bm_refs[parity][a][...] = jnp.max(s, axis=0, keepdims=True)

    def consume(item, parity, a):
        slot, group, _ = item
        j0 = jnp.minimum(group, total_groups - 1) * KV_GROUP
        rows = slice(a * HEAD_DIM, (a + 1) * HEAD_DIM)
        v_g = jnp.concatenate([vt_ref[0, j0 + c, rows, :] for c in range(KV_GROUP)], axis=1)
        m = state_m[slot, a]
        m_new = jnp.maximum(m, bm_refs[parity][a][...])
        alpha = jnp.exp2(m - m_new)
        p = jnp.exp2(s_refs[parity][a][...] - m_new).astype(BF16)
        pv = jnp.dot(pv_operand(v_g), p, preferred_element_type=F32)
        state_m[slot, a] = m_new
        state_acc[slot, a] = alpha * state_acc[slot, a] + pv

    key_idx = lax.broadcasted_iota(jnp.int32, (MOBA_BLOCK, TQ), 0)
    qry_idx = lax.broadcasted_iota(jnp.int32, (MOBA_BLOCK, TQ), 1)
    causal = key_idx <= qry_idx
    own = []
    for s in range(TILES_PER_STEP):
        cols = slice(s * TQ, (s + 1) * TQ)
        q_pair = qt_ref[0, :, cols]
        row = lax.broadcasted_iota(jnp.int32, q_pair.shape, 0)
        zero = jnp.zeros_like(q_pair)
        q_heads = (jnp.where(row < HEAD_DIM, q_pair, zero), jnp.where(row >= HEAD_DIM, q_pair, zero))
        k_own = k_ref[0, pl.ds(pl.multiple_of((first_tile + s) * MOBA_BLOCK, MOBA_BLOCK), MOBA_BLOCK), :]
        for a in range(2):
            q_scr[s, a] = q_heads[a]
            bias_scr[s, a] = bias_ref[0, a, :, cols]
        own.append([jnp.where(causal, jnp.dot(k_own, q_heads[a], preferred_element_type=F32), NEG)
                    for a in range(2)])
    first_item = locate(jnp.int32(0))
    for a in range(2):
        produce(first_item, 0, a)
    for s in range(TILES_PER_STEP):
        for a in range(2):
            rows = slice(a * HEAD_DIM, (a + 1) * HEAD_DIM)
            m = jnp.max(own[s][a], axis=0, keepdims=True)
            p = jnp.exp2(own[s][a] - m).astype(BF16)
            state_m[s, a] = m
            state_acc[s, a] = jnp.dot(pv_operand(vt_ref[0, first_tile + s, rows, :]), p,
                                      preferred_element_type=F32)

    def run_items(items_per_step, first, n_steps):
        def body(step, carry):
            e = first + items_per_step * step
            for sub in range(items_per_step):
                cur = sub % 2
                this_item, next_item = locate(e + sub), locate(e + sub + 1)
                for a in range(2):
                    produce(next_item, 1 - cur, a)
                    consume(this_item, cur, a)
            return carry
        lax.fori_loop(0, n_steps, body, 0)

    done = 0
    for size in ITEM_STEPS[:-1]:
        n_steps = (total_items - done) // size
        run_items(size, done, n_steps)
        done = done + n_steps * size
    run_items(ITEM_STEPS[-1], done, (total_items - done + ITEM_STEPS[-1] - 1) // ITEM_STEPS[-1])

    for s in range(TILES_PER_STEP):
        heads = []
        for a in range(2):
            acc = state_acc[s, a]
            heads.append(acc[:HEAD_DIM] / acc[HEAD_DIM:HEAD_DIM + 1])
        o_ref[0, s * TQ:(s + 1) * TQ, :] = jnp.concatenate(heads, axis=0).T.astype(BF16)


def _block_onehot():
    key_block = jnp.arange(KV_GROUP * MOBA_BLOCK, dtype=jnp.int32) // MOBA_BLOCK
    col = jnp.arange(LANES, dtype=jnp.int32)
    parity = jnp.arange(2, dtype=jnp.int32)
    hit = col[None, None, :] == (parity[:, None, None] * KV_GROUP + key_block[None, :, None])
    return hit.astype(BF16)


def _moba_attn(qt, k, vt, bias):
    B, _, S = qt.shape
    nb = S // MOBA_BLOCK
    tq = TILES_PER_STEP * MOBA_BLOCK
    assert nb % TILES_PER_STEP == 0
    assert 2 * KV_GROUP == F32_SUBLANES and nb % F32_SUBLANES == 0, "bias slabs are whole f32 sublane tiles"
    assert all(size % 2 == 0 for size in ITEM_STEPS), "score buffers alternate by item parity"
    pairs = MOBA_HEADS // 2
    group_keys = KV_GROUP * MOBA_BLOCK
    return pl.pallas_call(
        _moba_attn_kernel,
        grid=(B, pairs, nb // TILES_PER_STEP),
        in_specs=[
            pl.BlockSpec((1, LANES, tq), lambda b, p, i: (b, p, i)),
            pl.BlockSpec((1, S, LANES), lambda b, p, i: (b, 0, p)),
            pl.BlockSpec((1, nb, LANES, MOBA_BLOCK), lambda b, p, i: (b, 0, p, 0)),
            pl.BlockSpec((1, 2, nb, tq), lambda b, p, i: (b, p, 0, i)),
            pl.BlockSpec((2, group_keys, LANES), lambda b, p, i: (0, 0, 0)),
        ],
        out_specs=pl.BlockSpec((1, tq, LANES), lambda b, p, i: (b, i, p)),
        out_shape=jax.ShapeDtypeStruct((B, S, TOK_WIDTH), BF16),
        scratch_shapes=([pltpu.VMEM((group_keys, MOBA_BLOCK), F32)] * 4 + [pltpu.VMEM((1, MOBA_BLOCK), F32)] * 4
                        + [pltpu.VMEM((TILES_PER_STEP, 2, LANES, MOBA_BLOCK), BF16),
                           pltpu.VMEM((TILES_PER_STEP, 2, nb, MOBA_BLOCK), F32),
                           pltpu.VMEM((TILES_PER_STEP, 2, 1, MOBA_BLOCK), F32),
                           pltpu.VMEM((TILES_PER_STEP, 2, HEAD_DIM + SUM_ROWS, MOBA_BLOCK), F32)]),
        compiler_params=pltpu.CompilerParams(
            dimension_semantics=("arbitrary", "arbitrary", "arbitrary"), vmem_limit_bytes=VMEM_LIMIT_BYTES),
        name="moba_attn",
    )(qt, k, vt, bias, _block_onehot())


def _post_kernel(x_ref, tok_ref, mo_ref, wo_ref, g_ref, w1_ref, w2_ref, o_ref):
    mixed = jnp.concatenate([tok_ref[0], mo_ref[0]], axis=-1)
    x1 = x_ref[0] + jnp.dot(mixed, wo_ref[...], preferred_element_type=F32)
    u = _rms_rows(x1, g_ref[...]).astype(BF16)
    h = jnp.dot(u, w1_ref[...], preferred_element_type=F32)
    h = jnp.square(jnp.maximum(h, 0.0)).astype(BF16)
    o_ref[0] = x1 + jnp.dot(h, w2_ref[...], preferred_element_type=F32)


def _post(x, tok, mo, wo, g, w1, w2):
    B, S, D = x.shape
    T = min(TOKEN_TILE, S)
    const = lambda b, t: (0, 0)
    resident = functools.partial(pl.BlockSpec, index_map=const, pipeline_mode=pl.Buffered(1))
    return pl.pallas_call(
        _post_kernel,
        grid=(B, S // T),
        in_specs=[
            pl.BlockSpec((1, T, D), lambda b, t: (b, t, 0)),
            pl.BlockSpec((1, T, TOK_WIDTH), lambda b, t: (b, t, 0)),
            pl.BlockSpec((1, T, MEM_WIDTH), lambda b, t: (b, t, 0)),
            resident((D, D)),
            pl.BlockSpec((1, D), const),
            resident((D, D_FF)),
            resident((D_FF, D)),
        ],
        out_specs=pl.BlockSpec((1, T, D), lambda b, t: (b, t, 0)),
        out_shape=jax.ShapeDtypeStruct((B, S, D), F32),
        compiler_params=_params(),
        name="post",
    )(x, tok, mo, wo, g, w1, w2)


def _rope_tables_t(seq_len):
    pos = jnp.arange(seq_len, dtype=F32)
    inv = ROPE_THETA ** (-jnp.arange(0, HEAD_DIM, 2, dtype=F32) / HEAD_DIM)
    ang = pos[:, None] * inv[None, :]
    return jnp.cos(ang).T, jnp.sin(ang).T


def _block_diag(w_group):
    G, C, _ = w_group.shape
    eye = jnp.eye(G, dtype=w_group.dtype)
    return (eye[:, None, :, None] * w_group[:, :, None, :]).reshape(G * C, G * C)


def kernel(x, mem, g_mix, g_mem, g_mlp, w_in_pool, w_pool_group, pool_scale, w_in_moba, moba_q_gain,
           moba_k_gain, w_mem_kv, mem_q_gain, mem_k_gain, w_out, w_ff1, w_ff2):
    depth = g_mix.shape[0]
    S = x.shape[1]
    cos_t, sin_t = _rope_tables_t(S)
    col = lambda v: v.reshape(-1, 1)
    row = lambda v: v.reshape(1, -1)
    for i in range(depth):
        j = i // 2
        mk, mvt = _mem_kv(mem, row(g_mem[i]), w_mem_kv[i].T.astype(BF16), col(mem_k_gain[i]))
        if i % 2 == 0:
            w = w_in_pool[j]
            tok, mo = _pool_in(
                x, row(g_mix[i]), w[:, :TOK_WIDTH].astype(BF16), w[:, TOK_WIDTH:].T.astype(BF16),
                _block_diag(w_pool_group[j]).astype(BF16), row(pool_scale[j]), col(mem_q_gain[i]), mk, mvt)
        else:
            w = w_in_moba[j]
            qt, k, vt, bias, mo = _moba_in(
                x, row(g_mix[i]), w[:, :3 * TOK_WIDTH].T.astype(BF16), w[:, 3 * TOK_WIDTH:].T.astype(BF16),
                cos_t, sin_t, col(moba_q_gain[j]), col(moba_k_gain[j]), col(mem_q_gain[i]), mk, mvt)
            tok = _moba_attn(qt, k, vt, bias)
        x = _post(x, tok, mo, w_out[i].astype(BF16), row(g_mlp[i]), w_ff1[i].astype(BF16), w_ff2[i].astype(BF16))
    return x
```

```python
import functools

import jax
import jax.numpy as jnp
from jax import lax
from jax.experimental import pallas as pl
from jax.experimental.pallas import tpu as pltpu

D_MODEL = 1024
HEAD_DIM = 64
HALF_DIM = HEAD_DIM // 2
MEM_HEADS = 4
MEM_WIDTH = MEM_HEADS * HEAD_DIM
TOK_WIDTH = D_MODEL - MEM_WIDTH
MOBA_HEADS = TOK_WIDTH // HEAD_DIM
MOBA_BLOCK = 256
MOBA_TOPK = 3
POOL_WINDOWS = (2, 4, 8, 16)
POOL_GROUP_WIDTH = TOK_WIDTH // len(POOL_WINDOWS)
POOL_HALO = 16
D_FF = 4 * D_MODEL
ROPE_THETA = 10000.0
EPS = 1e-6
NEG = -1e30
SM_SCALE = HEAD_DIM ** -0.5
LOG2_E = 1.4426950408889634

LANES = 128
F32_SUBLANES = 8
BIAS_ROWS = 16
SUM_ROWS = 16
MXU_DEPTH = 256
VMEM_LIMIT_BYTES = 56 * 1024 * 1024

TOKEN_TILE = 512
VALUE_CHUNKS = 3
TILES_PER_STEP = 16
KV_GROUP = 4
ITEM_STEPS = (32, 4, 2)

F32 = jnp.float32
BF16 = jnp.bfloat16
NT_DIMS = (((1,), (1,)), ((), ()))


def _params():
    return pltpu.CompilerParams(
        dimension_semantics=("arbitrary", "arbitrary"), vmem_limit_bytes=VMEM_LIMIT_BYTES)


def _rms_rows(x, g):
    ms = jnp.mean(x * x, axis=-1, keepdims=True)
    return x * lax.rsqrt(ms + EPS) * g


def _head_rms_t(h, gain_col):
    ms = jnp.mean(h * h, axis=0, keepdims=True)
    return h * lax.rsqrt(ms + EPS) * gain_col


def _rope_t(h, cos, sin):
    h1, h2 = h[:HALF_DIM], h[HALF_DIM:]
    return jnp.concatenate([h1 * cos - h2 * sin, h2 * cos + h1 * sin], axis=0)


def _split_bf16(x):
    hi = x.astype(BF16)
    return hi, (x - hi.astype(F32)).astype(BF16)


def _pair_operand(q_bf, slot):
    z = jnp.zeros_like(q_bf)
    return jnp.concatenate([q_bf, z] if slot == 0 else [z, q_bf], axis=0)


def _mem_attention_t(qm_t, mq_gain, mk_ref, mvt_ref):
    scores = []
    for h in range(MEM_HEADS):
        pair, slot = divmod(h, 2)
        q = _head_rms_t(qm_t[h * HEAD_DIM:(h + 1) * HEAD_DIM], mq_gain)
        q2 = _pair_operand((q * (SM_SCALE * LOG2_E)).astype(BF16), slot)
        k_pair = mk_ref[0, :, pair * LANES:(pair + 1) * LANES]
        scores.append(jnp.dot(k_pair, q2, preferred_element_type=F32))
    probs = [jnp.exp2(s - jnp.max(s, axis=0, keepdims=True)).astype(BF16) for s in scores]
    outs = []
    for h in range(MEM_HEADS):
        v_t = mvt_ref[0, h * HEAD_DIM:(h + 1) * HEAD_DIM, :]
        ones = jnp.ones((SUM_ROWS, v_t.shape[1]), BF16)
        r = jnp.dot(jnp.concatenate([v_t, ones], axis=0), probs[h], preferred_element_type=F32)
        outs.append(r[:HEAD_DIM] / r[HEAD_DIM:HEAD_DIM + 1])
    return jnp.concatenate(outs, axis=0).T


def _mem_kv_kernel(mem_ref, g_ref, wkv_t_ref, kg_ref, mk_ref, mvt_ref):
    mem_n = _rms_rows(mem_ref[0], g_ref[...]).astype(BF16)
    kv_t = lax.dot_general(wkv_t_ref[...], mem_n, NT_DIMS, preferred_element_type=F32)
    k_heads = [_head_rms_t(kv_t[h * HEAD_DIM:(h + 1) * HEAD_DIM], kg_ref[...]) for h in range(MEM_HEADS)]
    mk_ref[0] = jnp.concatenate(k_heads, axis=0).T.astype(BF16)
    mvt_ref[0] = kv_t[MEM_WIDTH:].astype(BF16)


def _mem_kv(mem, g, wkv_t, k_gain):
    B, M, D = mem.shape
    return pl.pallas_call(
        _mem_kv_kernel,
        grid=(B, 1),
        in_specs=[
            pl.BlockSpec((1, M, D), lambda b, _: (b, 0, 0)),
            pl.BlockSpec((1, D), lambda b, _: (0, 0)),
            pl.BlockSpec((2 * MEM_WIDTH, D), lambda b, _: (0, 0)),
            pl.BlockSpec((HEAD_DIM, 1), lambda b, _: (0, 0)),
        ],
        out_specs=[
            pl.BlockSpec((1, M, MEM_WIDTH), lambda b, _: (b, 0, 0)),
            pl.BlockSpec((1, MEM_WIDTH, M), lambda b, _: (b, 0, 0)),
        ],
        out_shape=[
            jax.ShapeDtypeStruct((B, M, MEM_WIDTH), BF16),
            jax.ShapeDtypeStruct((B, MEM_WIDTH, M), BF16),
        ],
        compiler_params=_params(),
        name="mem_kv",
    )(mem, g, wkv_t, k_gain)


def _pool_in_kernel(x_ref, g_ref, w_tok_ref, wm_t_ref, wg_ref, scale_ref, mqg_ref, mk_ref, mvt_ref,
                    tok_ref, mo_ref, halo_ref):
    t = pl.program_id(1)
    T = x_ref.shape[1]

    @pl.when(t == 0)
    def _():
        halo_ref[...] = jnp.zeros_like(halo_ref)

    u = _rms_rows(x_ref[0], g_ref[...]).astype(BF16)
    h = jnp.dot(u, w_tok_ref[...], preferred_element_type=F32)
    qm_t = lax.dot_general(wm_t_ref[...], u, NT_DIMS, preferred_element_type=F32)

    ext = jnp.concatenate([halo_ref[...], h], axis=0)
    halo_ref[...] = h[T - POOL_HALO:]
    s2 = ext + pltpu.roll(ext, 1, 0)
    s4 = s2 + pltpu.roll(s2, 2, 0)
    s8 = s4 + pltpu.roll(s4, 4, 0)
    s16 = s8 + pltpu.roll(s8, 8, 0)
    lane = lax.broadcasted_iota(jnp.int32, (T, TOK_WIDTH), 1)
    pos = t * T + lax.broadcasted_iota(jnp.int32, (T, TOK_WIDTH), 0)
    sums = (s2, s4, s8, s16)
    win_sum = sums[-1][POOL_HALO:]
    window = jnp.full((T, TOK_WIDTH), POOL_WINDOWS[-1], jnp.int32)
    for gi in range(len(POOL_WINDOWS) - 2, -1, -1):
        in_group = lane < (gi + 1) * POOL_GROUP_WIDTH
        win_sum = jnp.where(in_group, sums[gi][POOL_HALO:], win_sum)
        window = jnp.where(in_group, POOL_WINDOWS[gi], window)
    count = jnp.minimum(pos + 1, window).astype(F32)
    d = (win_sum / count - h).astype(BF16)
    y = jnp.dot(d, wg_ref[...], preferred_element_type=F32) * scale_ref[...]
    tok_ref[0] = y.astype(BF16)

    mo_ref[0] = _mem_attention_t(qm_t, mqg_ref[...], mk_ref, mvt_ref).astype(BF16)


def _pool_in(x, g, w_tok, wm_t, wg, scale, mq_gain, mk, mvt):
    B, S, D = x.shape
    M = mk.shape[1]
    T = min(TOKEN_TILE, S)
    const = lambda b, t: (0, 0)
    return pl.pallas_call(
        _pool_in_kernel,
        grid=(B, S // T),
        in_specs=[
            pl.BlockSpec((1, T, D), lambda b, t: (b, t, 0)),
            pl.BlockSpec((1, D), const),
            pl.BlockSpec((D, TOK_WIDTH), const),
            pl.BlockSpec((MEM_WIDTH, D), const),
            pl.BlockSpec((TOK_WIDTH, TOK_WIDTH), const),
            pl.BlockSpec((1, TOK_WIDTH), const),
            pl.BlockSpec((HEAD_DIM, 1), const),
            pl.BlockSpec((1, M, MEM_WIDTH), lambda b, t: (b, 0, 0)),
            pl.BlockSpec((1, MEM_WIDTH, M), lambda b, t: (b, 0, 0)),
        ],
        out_specs=[
            pl.BlockSpec((1, T, TOK_WIDTH), lambda b, t: (b, t, 0)),
            pl.BlockSpec((1, T, MEM_WIDTH), lambda b, t: (b, t, 0)),
        ],
        out_shape=[
            jax.ShapeDtypeStruct((B, S, TOK_WIDTH), BF16),
            jax.ShapeDtypeStruct((B, S, MEM_WIDTH), BF16),
        ],
        scratch_shapes=[pltpu.VMEM((POOL_HALO, TOK_WIDTH), F32)],
        compiler_params=_params(),
        name="pool_in",
    )(x, g, w_tok, wm_t, wg, scale, mq_gain, mk, mvt)


def _selection_bias(gate, t, T):
    nb, width = gate.shape
    blk = lax.broadcasted_iota(jnp.int32, (nb, width), 0)
    pos = t * T + lax.rem(lax.broadcasted_iota(jnp.int32, (nb, width), 1), T)
    own = lax.shift_right_logical(pos, MOBA_BLOCK.bit_length() - 1)
    g = jnp.where(blk < own, gate, -jnp.inf)
    bias = jnp.full((nb, width), NEG, F32)
    for _ in range(MOBA_TOPK):
        mx = jnp.max(g, axis=0, keepdims=True)
        idx = jnp.min(jnp.where(g == mx, blk, nb), axis=0, keepdims=True)
        idx = jnp.where(mx > -jnp.inf, idx, -1)
        pick = blk == idx
        bias = jnp.where(pick, 0.0, bias)
        g = jnp.where(pick, -jnp.inf, g)
    return bias


def _moba_in_kernel(x_ref, g_ref, wqkv_t_ref, wm_t_ref, cos_ref, sin_ref, qg_ref, kg_ref, mqg_ref,
                    mk_ref, mvt_ref, qt_ref, k_ref, vt_ref, bias_ref, mo_ref, kmean_ref):
    t = pl.program_id(1)
    T = x_ref.shape[1]
    blocks_per_tile = T // MOBA_BLOCK

    @pl.when(t == 0)
    def _():
        kmean_ref[...] = jnp.zeros_like(kmean_ref)

    u = _rms_rows(x_ref[0], g_ref[...]).astype(BF16)
    cos, sin = cos_ref[...], sin_ref[...]

    def proj_t(lo, hi):
        return lax.dot_general(wqkv_t_ref[lo:hi, :], u, NT_DIMS, preferred_element_type=F32)

    k_t = proj_t(TOK_WIDTH, 2 * TOK_WIDTH)
    q_t = proj_t(0, TOK_WIDTH)

    k_heads = [_rope_t(_head_rms_t(k_t[h * HEAD_DIM:(h + 1) * HEAD_DIM], kg_ref[...]), cos, sin)
               for h in range(MOBA_HEADS)]
    k_nat = jnp.concatenate(k_heads, axis=0).T
    k_ref[0] = k_nat.astype(BF16)
    for c in range(blocks_per_tile):
        kmean_ref[pl.ds(t * blocks_per_tile + c, 1), :] = jnp.mean(
            k_nat[c * MOBA_BLOCK:(c + 1) * MOBA_BLOCK], axis=0, keepdims=True)

    km_hi, km_lo = _split_bf16(kmean_ref[...])
    heads_per_chunk = MOBA_HEADS // VALUE_CHUNKS
    chunk_rows = TOK_WIDTH // VALUE_CHUNKS
    for chunk in range(VALUE_CHUNKS):
        lo = 2 * TOK_WIDTH + chunk * chunk_rows
        v_t = proj_t(lo, lo + chunk_rows)
        for c in range(blocks_per_tile):
            vt_ref[0, c, chunk * chunk_rows:(chunk + 1) * chunk_rows, :] = (
                v_t[:, c * MOBA_BLOCK:(c + 1) * MOBA_BLOCK].astype(BF16))
        heads = range(chunk * heads_per_chunk, (chunk + 1) * heads_per_chunk)
        gates = []
        for h in heads:
            rows = slice(h * HEAD_DIM, (h + 1) * HEAD_DIM)
            pair, slot = divmod(h, 2)
            lanes = slice(pair * LANES, (pair + 1) * LANES)
            q = _rope_t(_head_rms_t(q_t[rows], qg_ref[...]), cos, sin)
            qt_ref[0, rows, :] = (q * (SM_SCALE * LOG2_E)).astype(BF16)
            q_hi, q_lo = _split_bf16(q)
            q_hi2, q_lo2 = _pair_operand(q_hi, slot), _pair_operand(q_lo, slot)
            gates.append(jnp.dot(jnp.concatenate([km_hi[:, lanes], km_lo[:, lanes]], axis=1),
                                 jnp.concatenate([q_hi2, q_hi2], axis=0), preferred_element_type=F32)
                         + jnp.dot(km_hi[:, lanes], q_lo2, preferred_element_type=F32))
        bias = _selection_bias(jnp.concatenate(gates, axis=1), t, T)
        for n, h in enumerate(heads):
            bias_ref[0, h] = bias[:, n * T:(n + 1) * T]

    qm_t = lax.dot_general(wm_t_ref[...], u, NT_DIMS, preferred_element_type=F32)
    mo_ref[0] = _mem_attention_t(qm_t, mqg_ref[...], mk_ref, mvt_ref).astype(BF16)


def _moba_in(x, g, wqkv_t, wm_t, cos_t, sin_t, q_gain, k_gain, mq_gain, mk, mvt):
    B, S, D = x.shape
    M = mk.shape[1]
    T = min(TOKEN_TILE, S)
    nb = S // MOBA_BLOCK
    const = lambda b, t: (0, 0)
    return pl.pallas_call(
        _moba_in_kernel,
        grid=(B, S // T),
        in_specs=[
            pl.BlockSpec((1, T, D), lambda b, t: (b, t, 0)),
            pl.BlockSpec((1, D), const),
            pl.BlockSpec((3 * TOK_WIDTH, D), const),
            pl.BlockSpec((MEM_WIDTH, D), const),
            pl.BlockSpec((HALF_DIM, T), lambda b, t: (0, t)),
            pl.BlockSpec((HALF_DIM, T), lambda b, t: (0, t)),
            pl.BlockSpec((HEAD_DIM, 1), const),
            pl.BlockSpec((HEAD_DIM, 1), const),
            pl.BlockSpec((HEAD_DIM, 1), const),
            pl.BlockSpec((1, M, MEM_WIDTH), lambda b, t: (b, 0, 0)),
            pl.BlockSpec((1, MEM_WIDTH, M), lambda b, t: (b, 0, 0)),
        ],
        out_specs=[
            pl.BlockSpec((1, TOK_WIDTH, T), lambda b, t: (b, 0, t)),
            pl.BlockSpec((1, T, TOK_WIDTH), lambda b, t: (b, t, 0)),
            pl.BlockSpec((1, T // MOBA_BLOCK, TOK_WIDTH, MOBA_BLOCK), lambda b, t: (b, t, 0, 0)),
            pl.BlockSpec((1, MOBA_HEADS, nb, T), lambda b, t: (b, 0, 0, t)),
            pl.BlockSpec((1, T, MEM_WIDTH), lambda b, t: (b, t, 0)),
        ],
        out_shape=[
            jax.ShapeDtypeStruct((B, TOK_WIDTH, S), BF16),
            jax.ShapeDtypeStruct((B, S, TOK_WIDTH), BF16),
            jax.ShapeDtypeStruct((B, nb, TOK_WIDTH, MOBA_BLOCK), BF16),
            jax.ShapeDtypeStruct((B, MOBA_HEADS, nb, S), F32),
            jax.ShapeDtypeStruct((B, S, MEM_WIDTH), BF16),
        ],
        scratch_shapes=[pltpu.VMEM((nb, TOK_WIDTH), F32)],
        compiler_params=_params(),
        name="moba_in",
    )(x, g, wqkv_t, wm_t, cos_t, sin_t, q_gain, k_gain, mq_gain, mk, mvt)


def _moba_attn_kernel(qt_ref, k_ref, vt_ref, bias_ref, onehot_ref, o_ref, *scratch):
    s_refs = (scratch[0:2], scratch[2:4])
    bm_refs = (scratch[4:6], scratch[6:8])
    q_scr, bias_scr, state_m, state_acc = scratch[8:12]
    TQ = MOBA_BLOCK
    group_keys = KV_GROUP * MOBA_BLOCK
    total_groups = k_ref.shape[1] // group_keys
    first_tile = pl.program_id(2) * TILES_PER_STEP

    live = [(first_tile + s + KV_GROUP - 1) // KV_GROUP for s in range(TILES_PER_STEP)]
    starts = [sum(live[:s]) for s in range(TILES_PER_STEP)]
    total_items = sum(live)

    def locate(e):
        slot = jnp.int32(0)
        start = jnp.int32(0)
        for s in range(1, TILES_PER_STEP):
            here = e >= starts[s]
            slot = slot + here.astype(jnp.int32)
            start = jnp.where(here, starts[s], start)
        return slot, e - start, e >= total_items

    zeros_tail = jnp.zeros((MXU_DEPTH - LANES - BIAS_ROWS, TQ), BF16)
    ones_rows = jnp.ones((SUM_ROWS, 1), BF16)

    def pv_operand(v_t):
        return jnp.concatenate([v_t, jnp.broadcast_to(ones_rows, (SUM_ROWS, v_t.shape[1]))], axis=0)

    def produce(item, parity, a):
        slot, group, past_end = item
        gc = jnp.minimum(group, total_groups - 1)
        k_g = k_ref[0, pl.ds(pl.multiple_of(gc * group_keys, group_keys), group_keys), :]
        lhs = jnp.concatenate([k_g, onehot_ref[gc % 2]], axis=1)
        penalty = jnp.where(past_end, NEG, 0.0)
        slab_row = pl.multiple_of((gc // 2) * 2 * KV_GROUP, 2 * KV_GROUP)
        slab = bias_scr[slot, a, pl.ds(slab_row, 2 * KV_GROUP), :] + penalty
        slab = jnp.concatenate([slab, jnp.zeros_like(slab)], axis=0).astype(BF16)
        rhs = jnp.concatenate([q_scr[slot, a], slab, zeros_tail], axis=0)
        s = jnp.dot(lhs, rhs, preferred_element_type=F32)
        s_refs[parity][a][...] = s
        bm_refs[parity][a][...] = jnp.max(s, axis=0, keepdims=True)

    def consume(item, parity, a):
        slot, group, _ = item
        j0 = jnp.minimum(group, total_groups - 1) * KV_GROUP
        rows = slice(a * HEAD_DIM, (a + 1) * HEAD_DIM)
        v_g = jnp.concatenate([vt_ref[0, j0 + c, rows, :] for c in range(KV_GROUP)], axis=1)
        m = state_m[slot, a]
        m_new = jnp.maximum(m, bm_refs[parity][a][...])
        alpha = jnp.exp2(m - m_new)
        p = jnp.exp2(s_refs[parity][a][...] - m_new).astype(BF16)
        pv = jnp.dot(pv_operand(v_g), p, preferred_element_type=F32)
        state_m[slot, a] = m_new
        state_acc[slot, a] = alpha * state_acc[slot, a] + pv

    key_idx = lax.broadcasted_iota(jnp.int32, (MOBA_BLOCK, TQ), 0)
    qry_idx = lax.broadcasted_iota(jnp.int32, (MOBA_BLOCK, TQ), 1)
    causal = key_idx <= qry_idx
    own = []
    for s in range(TILES_PER_STEP):
        cols = slice(s * TQ, (s + 1) * TQ)
        q_pair = qt_ref[0, :, cols]
        row = lax.broadcasted_iota(jnp.int32, q_pair.shape, 0)
        zero = jnp.zeros_like(q_pair)
        q_heads = (jnp.where(row < HEAD_DIM, q_pair, zero), jnp.where(row >= HEAD_DIM, q_pair, zero))
        k_own = k_ref[0, pl.ds(pl.multiple_of((first_tile + s) * MOBA_BLOCK, MOBA_BLOCK), MOBA_BLOCK), :]
        for a in range(2):
            q_scr[s, a] = q_heads[a]
            bias_scr[s, a] = bias_ref[0, a, :, cols]
        own.append([jnp.where(causal, jnp.dot(k_own, q_heads[a], preferred_element_type=F32), NEG)
                    for a in range(2)])
    first_item = locate(jnp.int32(0))
    for a in range(2):
        produce(first_item, 0, a)
    for s in range(TILES_PER_STEP):
        for a in range(2):
            rows = slice(a * HEAD_DIM, (a + 1) * HEAD_DIM)
            m = jnp.max(own[s][a], axis=0, keepdims=True)
            p = jnp.exp2(own[s][a] - m).astype(BF16)
            state_m[s, a] = m
            state_acc[s, a] = jnp.dot(pv_operand(vt_ref[0, first_tile + s, rows, :]), p,
                                      preferred_element_type=F32)

    def run_items(items_per_step, first, n_steps):
        def body(step, carry):
            e = first + items_per_step * step
            for sub in range(items_per_step):
                cur = sub % 2
                this_item, next_item = locate(e + sub), locate(e + sub + 1)
                for a in range(2):
                    produce(next_item, 1 - cur, a)
                    consume(this_item, cur, a)
            return carry
        lax.fori_loop(0, n_steps, body, 0)

    done = 0
    for size in ITEM_STEPS[:-1]:
        n_steps = (total_items - done) // size
        run_items(size, done, n_steps)
        done = done + n_steps * size
    run_items(ITEM_STEPS[-1], done, (total_items - done + ITEM_STEPS[-1] - 1) // ITEM_STEPS[-1])

    for s in range(TILES_PER_STEP):
        heads = []
        for a in range(2):
            acc = state_acc[s, a]
            heads.append(acc[:HEAD_DIM] / acc[HEAD_DIM:HEAD_DIM + 1])
        o_ref[0, s * TQ:(s + 1) * TQ, :] = jnp.concatenate(heads, axis=0).T.astype(BF16)


def _block_onehot():
    key_block = jnp.arange(KV_GROUP * MOBA_BLOCK, dtype=jnp.int32) // MOBA_BLOCK
    col = jnp.arange(LANES, dtype=jnp.int32)
    parity = jnp.arange(2, dtype=jnp.int32)
    hit = col[None, None, :] == (parity[:, None, None] * KV_GROUP + key_block[None, :, None])
    return hit.astype(BF16)


def _moba_attn(qt, k, vt, bias):
    B, _, S = qt.shape
    nb = S // MOBA_BLOCK
    tq = TILES_PER_STEP * MOBA_BLOCK
    assert nb % TILES_PER_STEP == 0
    assert 2 * KV_GROUP == F32_SUBLANES and nb % F32_SUBLANES == 0, "bias slabs are whole f32 sublane tiles"
    assert all(size % 2 == 0 for size in ITEM_STEPS), "score buffers alternate by item parity"
    pairs = MOBA_HEADS // 2
    group_keys = KV_GROUP * MOBA_BLOCK
    return pl.pallas_call(
        _moba_attn_kernel,
        grid=(B, pairs, nb // TILES_PER_STEP),
        in_specs=[
            pl.BlockSpec((1, LANES, tq), lambda b, p, i: (b, p, i)),
            pl.BlockSpec((1, S, LANES), lambda b, p, i: (b, 0, p)),
            pl.BlockSpec((1, nb, LANES, MOBA_BLOCK), lambda b, p, i: (b, 0, p, 0)),
            pl.BlockSpec((1, 2, nb, tq), lambda b, p, i: (b, p, 0, i)),
            pl.BlockSpec((2, group_keys, LANES), lambda b, p, i: (0, 0, 0)),
        ],
        out_specs=pl.BlockSpec((1, tq, LANES), lambda b, p, i: (b, i, p)),
        out_shape=jax.ShapeDtypeStruct((B, S, TOK_WIDTH), BF16),
        scratch_shapes=([pltpu.VMEM((group_keys, MOBA_BLOCK), F32)] * 4 + [pltpu.VMEM((1, MOBA_BLOCK), F32)] * 4
                        + [pltpu.VMEM((TILES_PER_STEP, 2, LANES, MOBA_BLOCK), BF16),
                           pltpu.VMEM((TILES_PER_STEP, 2, nb, MOBA_BLOCK), F32),
                           pltpu.VMEM((TILES_PER_STEP, 2, 1, MOBA_BLOCK), F32),
                           pltpu.VMEM((TILES_PER_STEP, 2, HEAD_DIM + SUM_ROWS, MOBA_BLOCK), F32)]),
        compiler_params=pltpu.CompilerParams(
            dimension_semantics=("arbitrary", "arbitrary", "arbitrary"), vmem_limit_bytes=VMEM_LIMIT_BYTES),
        name="moba_attn",
    )(qt, k, vt, bias, _block_onehot())


def _post_kernel(x_ref, tok_ref, mo_ref, wo_ref, g_ref, w1_ref, w2_ref, o_ref):
    mixed = jnp.concatenate([tok_ref[0], mo_ref[0]], axis=-1)
    x1 = x_ref[0] + jnp.dot(mixed, wo_ref[...], preferred_element_type=F32)
    u = _rms_rows(x1, g_ref[...]).astype(BF16)
    h = jnp.dot(u, w1_ref[...], preferred_element_type=F32)
    h = jnp.square(jnp.maximum(h, 0.0)).astype(BF16)
    o_ref[0] = x1 + jnp.dot(h, w2_ref[...], preferred_element_type=F32)


def _post(x, tok, mo, wo, g, w1, w2):
    B, S, D = x.shape
    T = min(TOKEN_TILE, S)
    const = lambda b, t: (0, 0)
    resident = functools.partial(pl.BlockSpec, index_map=const, pipeline_mode=pl.Buffered(1))
    return pl.pallas_call(
        _post_kernel,
        grid=(B, S // T),
        in_specs=[
            pl.BlockSpec((1, T, D), lambda b, t: (b, t, 0)),
            pl.BlockSpec((1, T, TOK_WIDTH), lambda b, t: (b, t, 0)),
            pl.BlockSpec((1, T, MEM_WIDTH), lambda b, t: (b, t, 0)),
            resident((D, D)),
            pl.BlockSpec((1, D), const),
            resident((D, D_FF)),
            resident((D_FF, D)),
        ],
        out_specs=pl.BlockSpec((1, T, D), lambda b, t: (b, t, 0)),
        out_shape=jax.ShapeDtypeStruct((B, S, D), F32),
        compiler_params=_params(),
        name="post",
    )(x, tok, mo, wo, g, w1, w2)


def _rope_tables_t(seq_len):
    pos = jnp.arange(seq_len, dtype=F32)
    inv = ROPE_THETA ** (-jnp.arange(0, HEAD_DIM, 2, dtype=F32) / HEAD_DIM)
    ang = pos[:, None] * inv[None, :]
    return jnp.cos(ang).T, jnp.sin(ang).T


def _block_diag(w_group):
    G, C, _ = w_group.shape
    eye = jnp.eye(G, dtype=w_group.dtype)
    return (eye[:, None, :, None] * w_group[:, :, None, :]).reshape(G * C, G * C)


def kernel(x, mem, g_mix, g_mem, g_mlp, w_in_pool, w_pool_group, pool_scale, w_in_moba, moba_q_gain,
           moba_k_gain, w_mem_kv, mem_q_gain, mem_k_gain, w_out, w_ff1, w_ff2):
    depth = g_mix.shape[0]
    S = x.shape[1]
    cos_t, sin_t = _rope_tables_t(S)
    col = lambda v: v.reshape(-1, 1)
    row = lambda v: v.reshape(1, -1)
    for i in range(depth):
        j = i // 2
        mk, mvt = _mem_kv(mem, row(g_mem[i]), w_mem_kv[i].T.astype(BF16), col(mem_k_gain[i]))
        if i % 2 == 0:
            w = w_in_pool[j]
            tok, mo = _pool_in(
                x, row(g_mix[i]), w[:, :TOK_WIDTH].astype(BF16), w[:, TOK_WIDTH:].T.astype(BF16),
                _block_diag(w_pool_group[j]).astype(BF16), row(pool_scale[j]), col(mem_q_gain[i]), mk, mvt)
        else:
            w = w_in_moba[j]
            qt, k, vt, bias, mo = _moba_in(
                x, row(g_mix[i]), w[:, :3 * TOK_WIDTH].T.astype(BF16), w[:, 3 * TOK_WIDTH:].T.astype(BF16),
                cos_t, sin_t, col(moba_q_gain[j]), col(moba_k_gain[j]), col(mem_q_gain[i]), mk, mvt)
            tok = _moba_attn(qt, k, vt, bias)
        x = _post(x, tok, mo, w_out[i].astype(BF16), row(g_mlp[i]), w_ff1[i].astype(BF16), w_ff2[i].astype(BF16))
    return x
```

```python
import functools

import jax
import jax.numpy as jnp
from jax import lax
from jax.experimental import pallas as pl
from jax.experimental.pallas import tpu as pltpu

D_MODEL = 1024
HEAD_DIM = 64
HALF_DIM = HEAD_DIM // 2
MEM_HEADS = 4
MEM_WIDTH = MEM_HEADS * HEAD_DIM
TOK_WIDTH = D_MODEL - MEM_WIDTH
MOBA_HEADS = TOK_WIDTH // HEAD_DIM
MOBA_BLOCK = 256
MOBA_TOPK = 3
POOL_WINDOWS = (2, 4, 8, 16)
POOL_GROUP_WIDTH = TOK_WIDTH // len(POOL_WINDOWS)
POOL_HALO = 16
D_FF = 4 * D_MODEL
ROPE_THETA = 10000.0
EPS = 1e-6
NEG = -1e30
SM_SCALE = HEAD_DIM ** -0.5
LOG2_E = 1.4426950408889634

LANES = 128
F32_SUBLANES = 8
BIAS_ROWS = 16
SUM_ROWS = 16
MXU_DEPTH = 256
VMEM_LIMIT_BYTES = 56 * 1024 * 1024

TOKEN_TILE = 512
VALUE_CHUNKS = 3
TILES_PER_STEP = 16
KV_GROUP = 4
ITEM_STEPS = (32, 8, 4, 2)

F32 = jnp.float32
BF16 = jnp.bfloat16
NT_DIMS = (((1,), (1,)), ((), ()))


def _params():
    return pltpu.CompilerParams(
        dimension_semantics=("arbitrary", "arbitrary"), vmem_limit_bytes=VMEM_LIMIT_BYTES)


def _rms_rows(x, g):
    ms = jnp.mean(x * x, axis=-1, keepdims=True)
    return x * lax.rsqrt(ms + EPS) * g


def _head_rms_t(h, gain_col):
    ms = jnp.mean(h * h, axis=0, keepdims=True)
    return h * lax.rsqrt(ms + EPS) * gain_col


def _rope_t(h, cos, sin):
    h1, h2 = h[:HALF_DIM], h[HALF_DIM:]
    return jnp.concatenate([h1 * cos - h2 * sin, h2 * cos + h1 * sin], axis=0)


def _split_bf16(x):
    hi = x.astype(BF16)
    return hi, (x - hi.astype(F32)).astype(BF16)


def _pair_operand(q_bf, slot):
    z = jnp.zeros_like(q_bf)
    return jnp.concatenate([q_bf, z] if slot == 0 else [z, q_bf], axis=0)


def _mem_attention_t(qm_t, mq_gain, mk_ref, mvt_ref):
    scores = []
    for h in range(MEM_HEADS):
        pair, slot = divmod(h, 2)
        q = _head_rms_t(qm_t[h * HEAD_DIM:(h + 1) * HEAD_DIM], mq_gain)
        q2 = _pair_operand((q * (SM_SCALE * LOG2_E)).astype(BF16), slot)
        k_pair = mk_ref[0, :, pair * LANES:(pair + 1) * LANES]
        scores.append(jnp.dot(k_pair, q2, preferred_element_type=F32))
    probs = [jnp.exp2(s - jnp.max(s, axis=0, keepdims=True)).astype(BF16) for s in scores]
    outs = []
    for h in range(MEM_HEADS):
        v_t = mvt_ref[0, h * HEAD_DIM:(h + 1) * HEAD_DIM, :]
        ones = jnp.ones((SUM_ROWS, v_t.shape[1]), BF16)
        r = jnp.dot(jnp.concatenate([v_t, ones], axis=0), probs[h], preferred_element_type=F32)
        outs.append(r[:HEAD_DIM] / r[HEAD_DIM:HEAD_DIM + 1])
    return jnp.concatenate(outs, axis=0).T


def _mem_kv_kernel(mem_ref, g_ref, wkv_t_ref, kg_ref, mk_ref, mvt_ref):
    mem_n = _rms_rows(mem_ref[0], g_ref[...]).astype(BF16)
    kv_t = lax.dot_general(wkv_t_ref[...], mem_n, NT_DIMS, preferred_element_type=F32)
    k_heads = [_head_rms_t(kv_t[h * HEAD_DIM:(h + 1) * HEAD_DIM], kg_ref[...]) for h in range(MEM_HEADS)]
    mk_ref[0] = jnp.concatenate(k_heads, axis=0).T.astype(BF16)
    mvt_ref[0] = kv_t[MEM_WIDTH:].astype(BF16)


def _mem_kv(mem, g, wkv_t, k_gain):
    B, M, D = mem.shape
    return pl.pallas_call(
        _mem_kv_kernel,
        grid=(B, 1),
        in_specs=[
            pl.BlockSpec((1, M, D), lambda b, _: (b, 0, 0)),
            pl.BlockSpec((1, D), lambda b, _: (0, 0)),
            pl.BlockSpec((2 * MEM_WIDTH, D), lambda b, _: (0, 0)),
            pl.BlockSpec((HEAD_DIM, 1), lambda b, _: (0, 0)),
        ],
        out_specs=[
            pl.BlockSpec((1, M, MEM_WIDTH), lambda b, _: (b, 0, 0)),
            pl.BlockSpec((1, MEM_WIDTH, M), lambda b, _: (b, 0, 0)),
        ],
        out_shape=[
            jax.ShapeDtypeStruct((B, M, MEM_WIDTH), BF16),
            jax.ShapeDtypeStruct((B, MEM_WIDTH, M), BF16),
        ],
        compiler_params=_params(),
        name="mem_kv",
    )(mem, g, wkv_t, k_gain)


def _pool_in_kernel(x_ref, g_ref, w_tok_ref, wm_t_ref, wg_ref, scale_ref, mqg_ref, mk_ref, mvt_ref,
                    tok_ref, mo_ref, halo_ref):
    t = pl.program_id(1)
    T = x_ref.shape[1]

    @pl.when(t == 0)
    def _():
        halo_ref[...] = jnp.zeros_like(halo_ref)

    u = _rms_rows(x_ref[0], g_ref[...]).astype(BF16)
    h = jnp.dot(u, w_tok_ref[...], preferred_element_type=F32)
    qm_t = lax.dot_general(wm_t_ref[...], u, NT_DIMS, preferred_element_type=F32)

    ext = jnp.concatenate([halo_ref[...], h], axis=0)
    halo_ref[...] = h[T - POOL_HALO:]
    s2 = ext + pltpu.roll(ext, 1, 0)
    s4 = s2 + pltpu.roll(s2, 2, 0)
    s8 = s4 + pltpu.roll(s4, 4, 0)
    s16 = s8 + pltpu.roll(s8, 8, 0)
    lane = lax.broadcasted_iota(jnp.int32, (T, TOK_WIDTH), 1)
    pos = t * T + lax.broadcasted_iota(jnp.int32, (T, TOK_WIDTH), 0)
    sums = (s2, s4, s8, s16)
    win_sum = sums[-1][POOL_HALO:]
    window = jnp.full((T, TOK_WIDTH), POOL_WINDOWS[-1], jnp.int32)
    for gi in range(len(POOL_WINDOWS) - 2, -1, -1):
        in_group = lane < (gi + 1) * POOL_GROUP_WIDTH
        win_sum = jnp.where(in_group, sums[gi][POOL_HALO:], win_sum)
        window = jnp.where(in_group, POOL_WINDOWS[gi], window)
    count = jnp.minimum(pos + 1, window).astype(F32)
    d = (win_sum / count - h).astype(BF16)
    y = jnp.dot(d, wg_ref[...], preferred_element_type=F32) * scale_ref[...]
    tok_ref[0] = y.astype(BF16)

    mo_ref[0] = _mem_attention_t(qm_t, mqg_ref[...], mk_ref, mvt_ref).astype(BF16)


def _pool_in(x, g, w_tok, wm_t, wg, scale, mq_gain, mk, mvt):
    B, S, D = x.shape
    M = mk.shape[1]
    T = min(TOKEN_TILE, S)
    const = lambda b, t: (0, 0)
    return pl.pallas_call(
        _pool_in_kernel,
        grid=(B, S // T),
        in_specs=[
            pl.BlockSpec((1, T, D), lambda b, t: (b, t, 0)),
            pl.BlockSpec((1, D), const),
            pl.BlockSpec((D, TOK_WIDTH), const),
            pl.BlockSpec((MEM_WIDTH, D), const),
            pl.BlockSpec((TOK_WIDTH, TOK_WIDTH), const),
            pl.BlockSpec((1, TOK_WIDTH), const),
            pl.BlockSpec((HEAD_DIM, 1), const),
            pl.BlockSpec((1, M, MEM_WIDTH), lambda b, t: (b, 0, 0)),
            pl.BlockSpec((1, MEM_WIDTH, M), lambda b, t: (b, 0, 0)),
        ],
        out_specs=[
            pl.BlockSpec((1, T, TOK_WIDTH), lambda b, t: (b, t, 0)),
            pl.BlockSpec((1, T, MEM_WIDTH), lambda b, t: (b, t, 0)),
        ],
        out_shape=[
            jax.ShapeDtypeStruct((B, S, TOK_WIDTH), BF16),
            jax.ShapeDtypeStruct((B, S, MEM_WIDTH), BF16),
        ],
        scratch_shapes=[pltpu.VMEM((POOL_HALO, TOK_WIDTH), F32)],
        compiler_params=_params(),
        name="pool_in",
    )(x, g, w_tok, wm_t, wg, scale, mq_gain, mk, mvt)


def _selection_bias(gate, t, T):
    nb, width = gate.shape
    blk = lax.broadcasted_iota(jnp.int32, (nb, width), 0)
    pos = t * T + lax.rem(lax.broadcasted_iota(jnp.int32, (nb, width), 1), T)
    own = lax.shift_right_logical(pos, MOBA_BLOCK.bit_length() - 1)
    g = jnp.where(blk < own, gate, -jnp.inf)
    bias = jnp.full((nb, width), NEG, F32)
    for _ in range(MOBA_TOPK):
        mx = jnp.max(g, axis=0, keepdims=True)
        idx = jnp.min(jnp.where(g == mx, blk, nb), axis=0, keepdims=True)
        idx = jnp.where(mx > -jnp.inf, idx, -1)
        pick = blk == idx
        bias = jnp.where(pick, 0.0, bias)
        g = jnp.where(pick, -jnp.inf, g)
    return bias


def _moba_in_kernel(x_ref, g_ref, wqkv_t_ref, wm_t_ref, cos_ref, sin_ref, qg_ref, kg_ref, mqg_ref,
                    mk_ref, mvt_ref, qt_ref, k_ref, vt_ref, bias_ref, mo_ref, kmean_ref):
    t = pl.program_id(1)
    T = x_ref.shape[1]
    blocks_per_tile = T // MOBA_BLOCK

    @pl.when(t == 0)
    def _():
        kmean_ref[...] = jnp.zeros_like(kmean_ref)

    u = _rms_rows(x_ref[0], g_ref[...]).astype(BF16)
    cos, sin = cos_ref[...], sin_ref[...]

    def proj_t(lo, hi):
        return lax.dot_general(wqkv_t_ref[lo:hi, :], u, NT_DIMS, preferred_element_type=F32)

    k_t = proj_t(TOK_WIDTH, 2 * TOK_WIDTH)
    q_t = proj_t(0, TOK_WIDTH)

    k_heads = [_rope_t(_head_rms_t(k_t[h * HEAD_DIM:(h + 1) * HEAD_DIM], kg_ref[...]), cos, sin)
               for h in range(MOBA_HEADS)]
    k_nat = jnp.concatenate(k_heads, axis=0).T
    k_ref[0] = k_nat.astype(BF16)
    for c in range(blocks_per_tile):
        kmean_ref[pl.ds(t * blocks_per_tile + c, 1), :] = jnp.mean(
            k_nat[c * MOBA_BLOCK:(c + 1) * MOBA_BLOCK], axis=0, keepdims=True)

    km_hi, km_lo = _split_bf16(kmean_ref[...])
    heads_per_chunk = MOBA_HEADS // VALUE_CHUNKS
    chunk_rows = TOK_WIDTH // VALUE_CHUNKS
    for chunk in range(VALUE_CHUNKS):
        lo = 2 * TOK_WIDTH + chunk * chunk_rows
        v_t = proj_t(lo, lo + chunk_rows)
        for c in range(blocks_per_tile):
            vt_ref[0, c, chunk * chunk_rows:(chunk + 1) * chunk_rows, :] = (
                v_t[:, c * MOBA_BLOCK:(c + 1) * MOBA_BLOCK].astype(BF16))
        heads = range(chunk * heads_per_chunk, (chunk + 1) * heads_per_chunk)
        gates = []
        for h in heads:
            rows = slice(h * HEAD_DIM, (h + 1) * HEAD_DIM)
            pair, slot = divmod(h, 2)
            lanes = slice(pair * LANES, (pair + 1) * LANES)
            q = _rope_t(_head_rms_t(q_t[rows], qg_ref[...]), cos, sin)
            qt_ref[0, rows, :] = (q * (SM_SCALE * LOG2_E)).astype(BF16)
            q_hi, q_lo = _split_bf16(q)
            q_hi2, q_lo2 = _pair_operand(q_hi, slot), _pair_operand(q_lo, slot)
            gates.append(jnp.dot(jnp.concatenate([km_hi[:, lanes], km_lo[:, lanes]], axis=1),
                                 jnp.concatenate([q_hi2, q_hi2], axis=0), preferred_element_type=F32)
                         + jnp.dot(km_hi[:, lanes], q_lo2, preferred_element_type=F32))
        bias = _selection_bias(jnp.concatenate(gates, axis=1), t, T)
        for n, h in enumerate(heads):
            bias_ref[0, h] = bias[:, n * T:(n + 1) * T]

    qm_t = lax.dot_general(wm_t_ref[...], u, NT_DIMS, preferred_element_type=F32)
    mo_ref[0] = _mem_attention_t(qm_t, mqg_ref[...], mk_ref, mvt_ref).astype(BF16)


def _moba_in(x, g, wqkv_t, wm_t, cos_t, sin_t, q_gain, k_gain, mq_gain, mk, mvt):
    B, S, D = x.shape
    M = mk.shape[1]
    T = min(TOKEN_TILE, S)
    nb = S // MOBA_BLOCK
    const = lambda b, t: (0, 0)
    return pl.pallas_call(
        _moba_in_kernel,
        grid=(B, S // T),
        in_specs=[
            pl.BlockSpec((1, T, D), lambda b, t: (b, t, 0)),
            pl.BlockSpec((1, D), const),
            pl.BlockSpec((3 * TOK_WIDTH, D), const),
            pl.BlockSpec((MEM_WIDTH, D), const),
            pl.BlockSpec((HALF_DIM, T), lambda b, t: (0, t)),
            pl.BlockSpec((HALF_DIM, T), lambda b, t: (0, t)),
            pl.BlockSpec((HEAD_DIM, 1), const),
            pl.BlockSpec((HEAD_DIM, 1), const),
            pl.BlockSpec((HEAD_DIM, 1), const),
            pl.BlockSpec((1, M, MEM_WIDTH), lambda b, t: (b, 0, 0)),
            pl.BlockSpec((1, MEM_WIDTH, M), lambda b, t: (b, 0, 0)),
        ],
        out_specs=[
            pl.BlockSpec((1, TOK_WIDTH, T), lambda b, t: (b, 0, t)),
            pl.BlockSpec((1, T, TOK_WIDTH), lambda b, t: (b, t, 0)),
            pl.BlockSpec((1, T // MOBA_BLOCK, TOK_WIDTH, MOBA_BLOCK), lambda b, t: (b, t, 0, 0)),
            pl.BlockSpec((1, MOBA_HEADS, nb, T), lambda b, t: (b, 0, 0, t)),
            pl.BlockSpec((1, T, MEM_WIDTH), lambda b, t: (b, t, 0)),
        ],
        out_shape=[
            jax.ShapeDtypeStruct((B, TOK_WIDTH, S), BF16),
            jax.ShapeDtypeStruct((B, S, TOK_WIDTH), BF16),
            jax.ShapeDtypeStruct((B, nb, TOK_WIDTH, MOBA_BLOCK), BF16),
            jax.ShapeDtypeStruct((B, MOBA_HEADS, nb, S), F32),
            jax.ShapeDtypeStruct((B, S, MEM_WIDTH), BF16),
        ],
        scratch_shapes=[pltpu.VMEM((nb, TOK_WIDTH), F32)],
        compiler_params=_params(),
        name="moba_in",
    )(x, g, wqkv_t, wm_t, cos_t, sin_t, q_gain, k_gain, mq_gain, mk, mvt)


def _moba_attn_kernel(qt_ref, k_ref, vt_ref, bias_ref, onehot_ref, o_ref, *scratch):
    s_refs = (scratch[0:2], scratch[2:4])
    bm_refs = (scratch[4:6], scratch[6:8])
    q_scr, bias_scr, state_m, state_acc = scratch[8:12]
    TQ = MOBA_BLOCK
    group_keys = KV_GROUP * MOBA_BLOCK
    total_groups = k_ref.shape[1] // group_keys
    first_tile = pl.program_id(2) * TILES_PER_STEP

    own_group = [first_tile // KV_GROUP + s // KV_GROUP for s in range(TILES_PER_STEP)]
    live = own_group
    starts = [sum(live[:s]) for s in range(TILES_PER_STEP)]
    total_items = sum(live)

    def locate(e):
        slot = jnp.int32(0)
        start = jnp.int32(0)
        for s in range(1, TILES_PER_STEP):
            here = e >= starts[s]
            slot = slot + here.astype(jnp.int32)
            start = jnp.where(here, starts[s], start)
        return slot, e - start, e >= total_items

    zeros_tail = jnp.zeros((MXU_DEPTH - LANES - BIAS_ROWS, TQ), BF16)
    ones_rows = jnp.ones((SUM_ROWS, 1), BF16)

    def pv_operand(v_t):
        return jnp.concatenate([v_t, jnp.broadcast_to(ones_rows, (SUM_ROWS, v_t.shape[1]))], axis=0)

    def produce(item, parity, a):
        slot, group, past_end = item
        gc = jnp.minimum(group, total_groups - 1)
        k_g = k_ref[0, pl.ds(pl.multiple_of(gc * group_keys, group_keys), group_keys), :]
        lhs = jnp.concatenate([k_g, onehot_ref[gc % 2]], axis=1)
        penalty = jnp.where(past_end, NEG, 0.0)
        slab_row = pl.multiple_of((gc // 2) * 2 * KV_GROUP, 2 * KV_GROUP)
        slab = bias_scr[slot, a, pl.ds(slab_row, 2 * KV_GROUP), :] + penalty
        slab = jnp.concatenate([slab, jnp.zeros_like(slab)], axis=0).astype(BF16)
        rhs = jnp.concatenate([q_scr[slot, a], slab, zeros_tail], axis=0)
        s = jnp.dot(lhs, rhs, preferred_element_type=F32)
        s_refs[parity][a][...] = s
        bm_refs[parity][a][...] = jnp.max(s, axis=0, keepdims=True)

    def consume(item, parity, a):
        slot, group, _ = item
        j0 = jnp.minimum(group, total_groups - 1) * KV_GROUP
        rows = slice(a * HEAD_DIM, (a + 1) * HEAD_DIM)
        v_g = jnp.concatenate([vt_ref[0, j0 + c, rows, :] for c in range(KV_GROUP)], axis=1)
        m = state_m[slot, a]
        m_new = jnp.maximum(m, bm_refs[parity][a][...])
        alpha = jnp.exp2(m - m_new)
        p = jnp.exp2(s_refs[parity][a][...] - m_new).astype(BF16)
        pv = jnp.dot(pv_operand(v_g), p, preferred_element_type=F32)
        state_m[slot, a] = m_new
        state_acc[slot, a] = alpha * state_acc[slot, a] + pv

    key_idx = lax.broadcasted_iota(jnp.int32, (MOBA_BLOCK, TQ), 0)
    qry_idx = lax.broadcasted_iota(jnp.int32, (MOBA_BLOCK, TQ), 1)
    causal = key_idx <= qry_idx
    own = []
    for s in range(TILES_PER_STEP):
        cols = slice(s * TQ, (s + 1) * TQ)
        q_pair = qt_ref[0, :, cols]
        row = lax.broadcasted_iota(jnp.int32, q_pair.shape, 0)
        zero = jnp.zeros_like(q_pair)
        q_heads = (jnp.where(row < HEAD_DIM, q_pair, zero), jnp.where(row >= HEAD_DIM, q_pair, zero))
        k_own = k_ref[0, pl.ds(pl.multiple_of((first_tile + s) * MOBA_BLOCK, MOBA_BLOCK), MOBA_BLOCK), :]
        for a in range(2):
            q_scr[s, a] = q_heads[a]
            bias_scr[s, a] = bias_ref[0, a, :, cols]
        scores = [jnp.where(causal, jnp.dot(k_own, q_heads[a], preferred_element_type=F32), NEG)
                  for a in range(2)]
        extra = s % KV_GROUP
        if extra:
            g = own_group[s]
            keys = extra * MOBA_BLOCK
            k_past = k_ref[0, pl.ds(pl.multiple_of(g * group_keys, group_keys), keys), :]
            lhs = jnp.concatenate([k_past, onehot_ref[g % 2, 0:keys, :]], axis=1)
            slab_row = pl.multiple_of((g // 2) * 2 * KV_GROUP, 2 * KV_GROUP)
            for a in range(2):
                slab = bias_ref[0, a, pl.ds(slab_row, 2 * KV_GROUP), cols]
                slab = jnp.concatenate([slab, jnp.zeros_like(slab)], axis=0).astype(BF16)
                rhs = jnp.concatenate([q_heads[a], slab, zeros_tail], axis=0)
                past = jnp.dot(lhs, rhs, preferred_element_type=F32)
                scores[a] = jnp.concatenate([past, scores[a]], axis=0)
        own.append(scores)
    first_item = locate(jnp.int32(0))
    for a in range(2):
        produce(first_item, 0, a)
    for s in range(TILES_PER_STEP):
        extra = s % KV_GROUP
        for a in range(2):
            rows = slice(a * HEAD_DIM, (a + 1) * HEAD_DIM)
            m = jnp.max(own[s][a], axis=0, keepdims=True)
            p = jnp.exp2(own[s][a] - m).astype(BF16)
            v_t = jnp.concatenate([vt_ref[0, first_tile + s - extra + c, rows, :] for c in range(extra + 1)], axis=1)
            state_m[s, a] = m
            state_acc[s, a] = jnp.dot(pv_operand(v_t), p, preferred_element_type=F32)

    def run_items(items_per_step, first, n_steps):
        def body(step, carry):
            e = first + items_per_step * step
            for sub in range(items_per_step):
                cur = sub % 2
                this_item, next_item = locate(e + sub), locate(e + sub + 1)
                for a in range(2):
                    produce(next_item, 1 - cur, a)
                    consume(this_item, cur, a)
            return carry
        lax.fori_loop(0, n_steps, body, 0)

    done = 0
    for size in ITEM_STEPS[:-1]:
        n_steps = (total_items - done) // size
        run_items(size, done, n_steps)
        done = done + n_steps * size
    run_items(ITEM_STEPS[-1], done, (total_items - done + ITEM_STEPS[-1] - 1) // ITEM_STEPS[-1])

    for s in range(TILES_PER_STEP):
        heads = []
        for a in range(2):
            acc = state_acc[s, a]
            heads.append(acc[:HEAD_DIM] / acc[HEAD_DIM:HEAD_DIM + 1])
        o_ref[0, s * TQ:(s + 1) * TQ, :] = jnp.concatenate(heads, axis=0).T.astype(BF16)


def _block_onehot():
    key_block = jnp.arange(KV_GROUP * MOBA_BLOCK, dtype=jnp.int32) // MOBA_BLOCK
    col = jnp.arange(LANES, dtype=jnp.int32)
    parity = jnp.arange(2, dtype=jnp.int32)
    hit = col[None, None, :] == (parity[:, None, None] * KV_GROUP + key_block[None, :, None])
    return hit.astype(BF16)


def _moba_attn(qt, k, vt, bias):
    B, _, S = qt.shape
    nb = S // MOBA_BLOCK
    tq = TILES_PER_STEP * MOBA_BLOCK
    assert nb % TILES_PER_STEP == 0 and TILES_PER_STEP % KV_GROUP == 0
    assert 2 * KV_GROUP == F32_SUBLANES and nb % F32_SUBLANES == 0, "bias slabs are whole f32 sublane tiles"
    assert all(size % 2 == 0 for size in ITEM_STEPS), "score buffers alternate by item parity"
    pairs = MOBA_HEADS // 2
    group_keys = KV_GROUP * MOBA_BLOCK
    return pl.pallas_call(
        _moba_attn_kernel,
        grid=(B, pairs, nb // TILES_PER_STEP),
        in_specs=[
            pl.BlockSpec((1, LANES, tq), lambda b, p, i: (b, p, i)),
            pl.BlockSpec((1, S, LANES), lambda b, p, i: (b, 0, p)),
            pl.BlockSpec((1, nb, LANES, MOBA_BLOCK), lambda b, p, i: (b, 0, p, 0)),
            pl.BlockSpec((1, 2, nb, tq), lambda b, p, i: (b, p, 0, i)),
            pl.BlockSpec((2, group_keys, LANES), lambda b, p, i: (0, 0, 0)),
        ],
        out_specs=pl.BlockSpec((1, tq, LANES), lambda b, p, i: (b, i, p)),
        out_shape=jax.ShapeDtypeStruct((B, S, TOK_WIDTH), BF16),
        scratch_shapes=([pltpu.VMEM((group_keys, MOBA_BLOCK), F32)] * 4 + [pltpu.VMEM((1, MOBA_BLOCK), F32)] * 4
                        + [pltpu.VMEM((TILES_PER_STEP, 2, LANES, MOBA_BLOCK), BF16),
                           pltpu.VMEM((TILES_PER_STEP, 2, nb, MOBA_BLOCK), F32),
                           pltpu.VMEM((TILES_PER_STEP, 2, 1, MOBA_BLOCK), F32),
                           pltpu.VMEM((TILES_PER_STEP, 2, HEAD_DIM + SUM_ROWS, MOBA_BLOCK), F32)]),
        compiler_params=pltpu.CompilerParams(
            dimension_semantics=("arbitrary", "arbitrary", "arbitrary"), vmem_limit_bytes=VMEM_LIMIT_BYTES),
        name="moba_attn",
    )(qt, k, vt, bias, _block_onehot())


def _post_kernel(x_ref, tok_ref, mo_ref, wo_ref, g_ref, w1_ref, w2_ref, o_ref):
    mixed = jnp.concatenate([tok_ref[0], mo_ref[0]], axis=-1)
    x1 = x_ref[0] + jnp.dot(mixed, wo_ref[...], preferred_element_type=F32)
    u = _rms_rows(x1, g_ref[...]).astype(BF16)
    h = jnp.dot(u, w1_ref[...], preferred_element_type=F32)
    h = jnp.square(jnp.maximum(h, 0.0)).astype(BF16)
    o_ref[0] = x1 + jnp.dot(h, w2_ref[...], preferred_element_type=F32)


def _post(x, tok, mo, wo, g, w1, w2):
    B, S, D = x.shape
    T = min(TOKEN_TILE, S)
    const = lambda b, t: (0, 0)
    resident = functools.partial(pl.BlockSpec, index_map=const, pipeline_mode=pl.Buffered(1))
    return pl.pallas_call(
        _post_kernel,
        grid=(B, S // T),
        in_specs=[
            pl.BlockSpec((1, T, D), lambda b, t: (b, t, 0)),
            pl.BlockSpec((1, T, TOK_WIDTH), lambda b, t: (b, t, 0)),
            pl.BlockSpec((1, T, MEM_WIDTH), lambda b, t: (b, t, 0)),
            resident((D, D)),
            pl.BlockSpec((1, D), const),
            resident((D, D_FF)),
            resident((D_FF, D)),
        ],
        out_specs=pl.BlockSpec((1, T, D), lambda b, t: (b, t, 0)),
        out_shape=jax.ShapeDtypeStruct((B, S, D), F32),
        compiler_params=_params(),
        name="post",
    )(x, tok, mo, wo, g, w1, w2)


def _rope_tables_t(seq_len):
    pos = jnp.arange(seq_len, dtype=F32)
    inv = ROPE_THETA ** (-jnp.arange(0, HEAD_DIM, 2, dtype=F32) / HEAD_DIM)
    ang = pos[:, None] * inv[None, :]
    return jnp.cos(ang).T, jnp.sin(ang).T


def _block_diag(w_group):
    G, C, _ = w_group.shape
    eye = jnp.eye(G, dtype=w_group.dtype)
    return (eye[:, None, :, None] * w_group[:, :, None, :]).reshape(G * C, G * C)


def kernel(x, mem, g_mix, g_mem, g_mlp, w_in_pool, w_pool_group, pool_scale, w_in_moba, moba_q_gain,
           moba_k_gain, w_mem_kv, mem_q_gain, mem_k_gain, w_out, w_ff1, w_ff2):
    depth = g_mix.shape[0]
    S = x.shape[1]
    cos_t, sin_t = _rope_tables_t(S)
    col = lambda v: v.reshape(-1, 1)
    row = lambda v: v.reshape(1, -1)
    for i in range(depth):
        j = i // 2
        mk, mvt = _mem_kv(mem, row(g_mem[i]), w_mem_kv[i].T.astype(BF16), col(mem_k_gain[i]))
        if i % 2 == 0:
            w = w_in_pool[j]
            tok, mo = _pool_in(
                x, row(g_mix[i]), w[:, :TOK_WIDTH].astype(BF16), w[:, TOK_WIDTH:].T.astype(BF16),
                _block_diag(w_pool_group[j]).astype(BF16), row(pool_scale[j]), col(mem_q_gain[i]), mk, mvt)
        else:
            w = w_in_moba[j]
            qt, k, vt, bias, mo = _moba_in(
                x, row(g_mix[i]), w[:, :3 * TOK_WIDTH].T.astype(BF16), w[:, 3 * TOK_WIDTH:].T.astype(BF16),
                cos_t, sin_t, col(moba_q_gain[j]), col(moba_k_gain[j]), col(mem_q_gain[i]), mk, mvt)
            tok = _moba_attn(qt, k, vt, bias)
        x = _post(x, tok, mo, w_out[i].astype(BF16), row(g_mlp[i]), w_ff1[i].astype(BF16), w_ff2[i].astype(BF16))
    return x
```

```python
import functools

import jax
import jax.numpy as jnp
from jax import lax
from jax.experimental import pallas as pl
from jax.experimental.pallas import tpu as pltpu

D_MODEL = 1024
HEAD_DIM = 64
HALF_DIM = HEAD_DIM // 2
MEM_HEADS = 4
MEM_WIDTH = MEM_HEADS * HEAD_DIM
TOK_WIDTH = D_MODEL - MEM_WIDTH
MOBA_HEADS = TOK_WIDTH // HEAD_DIM
MOBA_BLOCK = 256
MOBA_TOPK = 3
POOL_WINDOWS = (2, 4, 8, 16)
POOL_GROUP_WIDTH = TOK_WIDTH // len(POOL_WINDOWS)
POOL_HALO = 16
D_FF = 4 * D_MODEL
ROPE_THETA = 10000.0
EPS = 1e-6
NEG = -1e30
SM_SCALE = HEAD_DIM ** -0.5
LOG2_E = 1.4426950408889634

LANES = 128
F32_SUBLANES = 8
BIAS_ROWS = 16
SUM_ROWS = 16
MXU_DEPTH = 256
VMEM_LIMIT_BYTES = 56 * 1024 * 1024

TOKEN_TILE = 512
VALUE_CHUNKS = 3
TILES_PER_STEP = 16
KV_GROUP = 4
ITEM_STEPS = (32, 24, 2)

F32 = jnp.float32
BF16 = jnp.bfloat16
NT_DIMS = (((1,), (1,)), ((), ()))


def _params():
    return pltpu.CompilerParams(
        dimension_semantics=("arbitrary", "arbitrary"), vmem_limit_bytes=VMEM_LIMIT_BYTES)


def _rms_rows(x, g):
    ms = jnp.mean(x * x, axis=-1, keepdims=True)
    return x * lax.rsqrt(ms + EPS) * g


def _head_rms_t(h, gain_col):
    ms = jnp.mean(h * h, axis=0, keepdims=True)
    return h * lax.rsqrt(ms + EPS) * gain_col


def _rope_t(h, cos, sin):
    h1, h2 = h[:HALF_DIM], h[HALF_DIM:]
    return jnp.concatenate([h1 * cos - h2 * sin, h2 * cos + h1 * sin], axis=0)


def _split_bf16(x):
    hi = x.astype(BF16)
    return hi, (x - hi.astype(F32)).astype(BF16)


def _pair_operand(q_bf, slot):
    z = jnp.zeros_like(q_bf)
    return jnp.concatenate([q_bf, z] if slot == 0 else [z, q_bf], axis=0)


def _mem_attention_t(qm_t, mq_gain, mk_ref, mvt_ref):
    scores = []
    for h in range(MEM_HEADS):
        pair, slot = divmod(h, 2)
        q = _head_rms_t(qm_t[h * HEAD_DIM:(h + 1) * HEAD_DIM], mq_gain)
        q2 = _pair_operand((q * (SM_SCALE * LOG2_E)).astype(BF16), slot)
        k_pair = mk_ref[0, :, pair * LANES:(pair + 1) * LANES]
        scores.append(jnp.dot(k_pair, q2, preferred_element_type=F32))
    probs = [jnp.exp2(s - jnp.max(s, axis=0, keepdims=True)).astype(BF16) for s in scores]
    outs = []
    for h in range(MEM_HEADS):
        v_t = mvt_ref[0, h * HEAD_DIM:(h + 1) * HEAD_DIM, :]
        ones = jnp.ones((SUM_ROWS, v_t.shape[1]), BF16)
        r = jnp.dot(jnp.concatenate([v_t, ones], axis=0), probs[h], preferred_element_type=F32)
        outs.append(r[:HEAD_DIM] / r[HEAD_DIM:HEAD_DIM + 1])
    return jnp.concatenate(outs, axis=0).T


def _mem_kv_kernel(mem_ref, g_ref, wkv_t_ref, kg_ref, mk_ref, mvt_ref):
    mem_n = _rms_rows(mem_ref[0], g_ref[...]).astype(BF16)
    kv_t = lax.dot_general(wkv_t_ref[...], mem_n, NT_DIMS, preferred_element_type=F32)
    k_heads = [_head_rms_t(kv_t[h * HEAD_DIM:(h + 1) * HEAD_DIM], kg_ref[...]) for h in range(MEM_HEADS)]
    mk_ref[0] = jnp.concatenate(k_heads, axis=0).T.astype(BF16)
    mvt_ref[0] = kv_t[MEM_WIDTH:].astype(BF16)


def _mem_kv(mem, g, wkv_t, k_gain):
    B, M, D = mem.shape
    return pl.pallas_call(
        _mem_kv_kernel,
        grid=(B, 1),
        in_specs=[
            pl.BlockSpec((1, M, D), lambda b, _: (b, 0, 0)),
            pl.BlockSpec((1, D), lambda b, _: (0, 0)),
            pl.BlockSpec((2 * MEM_WIDTH, D), lambda b, _: (0, 0)),
            pl.BlockSpec((HEAD_DIM, 1), lambda b, _: (0, 0)),
        ],
        out_specs=[
            pl.BlockSpec((1, M, MEM_WIDTH), lambda b, _: (b, 0, 0)),
            pl.BlockSpec((1, MEM_WIDTH, M), lambda b, _: (b, 0, 0)),
        ],
        out_shape=[
            jax.ShapeDtypeStruct((B, M, MEM_WIDTH), BF16),
            jax.ShapeDtypeStruct((B, MEM_WIDTH, M), BF16),
        ],
        compiler_params=_params(),
        name="mem_kv",
    )(mem, g, wkv_t, k_gain)


def _pool_in_kernel(x_ref, g_ref, w_tok_ref, wm_t_ref, wg_ref, scale_ref, mqg_ref, mk_ref, mvt_ref,
                    tok_ref, mo_ref, halo_ref):
    t = pl.program_id(1)
    T = x_ref.shape[1]

    @pl.when(t == 0)
    def _():
        halo_ref[...] = jnp.zeros_like(halo_ref)

    u = _rms_rows(x_ref[0], g_ref[...]).astype(BF16)
    h = jnp.dot(u, w_tok_ref[...], preferred_element_type=F32)
    qm_t = lax.dot_general(wm_t_ref[...], u, NT_DIMS, preferred_element_type=F32)

    ext = jnp.concatenate([halo_ref[...], h], axis=0)
    halo_ref[...] = h[T - POOL_HALO:]
    s2 = ext + pltpu.roll(ext, 1, 0)
    s4 = s2 + pltpu.roll(s2, 2, 0)
    s8 = s4 + pltpu.roll(s4, 4, 0)
    s16 = s8 + pltpu.roll(s8, 8, 0)
    lane = lax.broadcasted_iota(jnp.int32, (T, TOK_WIDTH), 1)
    pos = t * T + lax.broadcasted_iota(jnp.int32, (T, TOK_WIDTH), 0)
    sums = (s2, s4, s8, s16)
    win_sum = sums[-1][POOL_HALO:]
    window = jnp.full((T, TOK_WIDTH), POOL_WINDOWS[-1], jnp.int32)
    for gi in range(len(POOL_WINDOWS) - 2, -1, -1):
        in_group = lane < (gi + 1) * POOL_GROUP_WIDTH
        win_sum = jnp.where(in_group, sums[gi][POOL_HALO:], win_sum)
        window = jnp.where(in_group, POOL_WINDOWS[gi], window)
    count = jnp.minimum(pos + 1, window).astype(F32)
    d = (win_sum / count - h).astype(BF16)
    y = jnp.dot(d, wg_ref[...], preferred_element_type=F32) * scale_ref[...]
    tok_ref[0] = y.astype(BF16)

    mo_ref[0] = _mem_attention_t(qm_t, mqg_ref[...], mk_ref, mvt_ref).astype(BF16)


def _pool_in(x, g, w_tok, wm_t, wg, scale, mq_gain, mk, mvt):
    B, S, D = x.shape
    M = mk.shape[1]
    T = min(TOKEN_TILE, S)
    const = lambda b, t: (0, 0)
    return pl.pallas_call(
        _pool_in_kernel,
        grid=(B, S // T),
        in_specs=[
            pl.BlockSpec((1, T, D), lambda b, t: (b, t, 0)),
            pl.BlockSpec((1, D), const),
            pl.BlockSpec((D, TOK_WIDTH), const),
            pl.BlockSpec((MEM_WIDTH, D), const),
            pl.BlockSpec((TOK_WIDTH, TOK_WIDTH), const),
            pl.BlockSpec((1, TOK_WIDTH), const),
            pl.BlockSpec((HEAD_DIM, 1), const),
            pl.BlockSpec((1, M, MEM_WIDTH), lambda b, t: (b, 0, 0)),
            pl.BlockSpec((1, MEM_WIDTH, M), lambda b, t: (b, 0, 0)),
        ],
        out_specs=[
            pl.BlockSpec((1, T, TOK_WIDTH), lambda b, t: (b, t, 0)),
            pl.BlockSpec((1, T, MEM_WIDTH), lambda b, t: (b, t, 0)),
        ],
        out_shape=[
            jax.ShapeDtypeStruct((B, S, TOK_WIDTH), BF16),
            jax.ShapeDtypeStruct((B, S, MEM_WIDTH), BF16),
        ],
        scratch_shapes=[pltpu.VMEM((POOL_HALO, TOK_WIDTH), F32)],
        compiler_params=_params(),
        name="pool_in",
    )(x, g, w_tok, wm_t, wg, scale, mq_gain, mk, mvt)


def _selection_bias(gate, t, T):
    nb, width = gate.shape
    blk = lax.broadcasted_iota(jnp.int32, (nb, width), 0)
    pos = t * T + lax.rem(lax.broadcasted_iota(jnp.int32, (nb, width), 1), T)
    own = lax.shift_right_logical(pos, MOBA_BLOCK.bit_length() - 1)
    g = jnp.where(blk < own, gate, -jnp.inf)
    bias = jnp.full((nb, width), NEG, F32)
    for _ in range(MOBA_TOPK):
        mx = jnp.max(g, axis=0, keepdims=True)
        idx = jnp.min(jnp.where(g == mx, blk, nb), axis=0, keepdims=True)
        idx = jnp.where(mx > -jnp.inf, idx, -1)
        pick = blk == idx
        bias = jnp.where(pick, 0.0, bias)
        g = jnp.where(pick, -jnp.inf, g)
    return bias


def _moba_in_kernel(x_ref, g_ref, wqkv_t_ref, wm_t_ref, cos_ref, sin_ref, qg_ref, kg_ref, mqg_ref,
                    mk_ref, mvt_ref, qt_ref, k_ref, vt_ref, bias_ref, mo_ref, kmean_ref):
    t = pl.program_id(1)
    T = x_ref.shape[1]
    blocks_per_tile = T // MOBA_BLOCK

    @pl.when(t == 0)
    def _():
        kmean_ref[...] = jnp.zeros_like(kmean_ref)

    u = _rms_rows(x_ref[0], g_ref[...]).astype(BF16)
    cos, sin = cos_ref[...], sin_ref[...]

    def proj_t(lo, hi):
        return lax.dot_general(wqkv_t_ref[lo:hi, :], u, NT_DIMS, preferred_element_type=F32)

    k_t = proj_t(TOK_WIDTH, 2 * TOK_WIDTH)
    q_t = proj_t(0, TOK_WIDTH)

    k_heads = [_rope_t(_head_rms_t(k_t[h * HEAD_DIM:(h + 1) * HEAD_DIM], kg_ref[...]), cos, sin)
               for h in range(MOBA_HEADS)]
    k_nat = jnp.concatenate(k_heads, axis=0).T
    k_ref[0] = k_nat.astype(BF16)
    for c in range(blocks_per_tile):
        kmean_ref[pl.ds(t * blocks_per_tile + c, 1), :] = jnp.mean(
            k_nat[c * MOBA_BLOCK:(c + 1) * MOBA_BLOCK], axis=0, keepdims=True)

    km_hi, km_lo = _split_bf16(kmean_ref[...])
    heads_per_chunk = MOBA_HEADS // VALUE_CHUNKS
    chunk_rows = TOK_WIDTH // VALUE_CHUNKS
    for chunk in range(VALUE_CHUNKS):
        lo = 2 * TOK_WIDTH + chunk * chunk_rows
        v_t = proj_t(lo, lo + chunk_rows)
        for c in range(blocks_per_tile):
            vt_ref[0, c, chunk * chunk_rows:(chunk + 1) * chunk_rows, :] = (
                v_t[:, c * MOBA_BLOCK:(c + 1) * MOBA_BLOCK].astype(BF16))
        heads = range(chunk * heads_per_chunk, (chunk + 1) * heads_per_chunk)
        gates = []
        for h in heads:
            rows = slice(h * HEAD_DIM, (h + 1) * HEAD_DIM)
            pair, slot = divmod(h, 2)
            lanes = slice(pair * LANES, (pair + 1) * LANES)
            q = _rope_t(_head_rms_t(q_t[rows], qg_ref[...]), cos, sin)
            qt_ref[0, rows, :] = (q * (SM_SCALE * LOG2_E)).astype(BF16)
            q_hi, q_lo = _split_bf16(q)
            q_hi2, q_lo2 = _pair_operand(q_hi, slot), _pair_operand(q_lo, slot)
            gates.append(jnp.dot(jnp.concatenate([km_hi[:, lanes], km_lo[:, lanes]], axis=1),
                                 jnp.concatenate([q_hi2, q_hi2], axis=0), preferred_element_type=F32)
                         + jnp.dot(km_hi[:, lanes], q_lo2, preferred_element_type=F32))
        bias = _selection_bias(jnp.concatenate(gates, axis=1), t, T)
        for n, h in enumerate(heads):
            bias_ref[0, h] = bias[:, n * T:(n + 1) * T]

    qm_t = lax.dot_general(wm_t_ref[...], u, NT_DIMS, preferred_element_type=F32)
    mo_ref[0] = _mem_attention_t(qm_t, mqg_ref[...], mk_ref, mvt_ref).astype(BF16)


def _moba_in(x, g, wqkv_t, wm_t, cos_t, sin_t, q_gain, k_gain, mq_gain, mk, mvt):
    B, S, D = x.shape
    M = mk.shape[1]
    T = min(TOKEN_TILE, S)
    nb = S // MOBA_BLOCK
    const = lambda b, t: (0, 0)
    return pl.pallas_call(
        _moba_in_kernel,
        grid=(B, S // T),
        in_specs=[
            pl.BlockSpec((1, T, D), lambda b, t: (b, t, 0)),
            pl.BlockSpec((1, D), const),
            pl.BlockSpec((3 * TOK_WIDTH, D), const),
            pl.BlockSpec((MEM_WIDTH, D), const),
            pl.BlockSpec((HALF_DIM, T), lambda b, t: (0, t)),
            pl.BlockSpec((HALF_DIM, T), lambda b, t: (0, t)),
            pl.BlockSpec((HEAD_DIM, 1), const),
            pl.BlockSpec((HEAD_DIM, 1), const),
            pl.BlockSpec((HEAD_DIM, 1), const),
            pl.BlockSpec((1, M, MEM_WIDTH), lambda b, t: (b, 0, 0)),
            pl.BlockSpec((1, MEM_WIDTH, M), lambda b, t: (b, 0, 0)),
        ],
        out_specs=[
            pl.BlockSpec((1, TOK_WIDTH, T), lambda b, t: (b, 0, t)),
            pl.BlockSpec((1, T, TOK_WIDTH), lambda b, t: (b, t, 0)),
            pl.BlockSpec((1, T // MOBA_BLOCK, TOK_WIDTH, MOBA_BLOCK), lambda b, t: (b, t, 0, 0)),
            pl.BlockSpec((1, MOBA_HEADS, nb, T), lambda b, t: (b, 0, 0, t)),
            pl.BlockSpec((1, T, MEM_WIDTH), lambda b, t: (b, t, 0)),
        ],
        out_shape=[
            jax.ShapeDtypeStruct((B, TOK_WIDTH, S), BF16),
            jax.ShapeDtypeStruct((B, S, TOK_WIDTH), BF16),
            jax.ShapeDtypeStruct((B, nb, TOK_WIDTH, MOBA_BLOCK), BF16),
            jax.ShapeDtypeStruct((B, MOBA_HEADS, nb, S), F32),
            jax.ShapeDtypeStruct((B, S, MEM_WIDTH), BF16),
        ],
        scratch_shapes=[pltpu.VMEM((nb, TOK_WIDTH), F32)],
        compiler_params=_params(),
        name="moba_in",
    )(x, g, wqkv_t, wm_t, cos_t, sin_t, q_gain, k_gain, mq_gain, mk, mvt)


def _moba_attn_kernel(qt_ref, k_ref, vt_ref, bias_ref, onehot_ref, o_ref, *scratch):
    s_refs = (scratch[0:2], scratch[2:4])
    bm_refs = (scratch[4:6], scratch[6:8])
    q_scr, bias_scr, state_m, state_acc = scratch[8:12]
    TQ = MOBA_BLOCK
    group_keys = KV_GROUP * MOBA_BLOCK
    total_groups = k_ref.shape[1] // group_keys
    first_tile = pl.program_id(2) * TILES_PER_STEP

    own_group = [first_tile // KV_GROUP + s // KV_GROUP for s in range(TILES_PER_STEP)]
    live = own_group
    starts = [sum(live[:s]) for s in range(TILES_PER_STEP)]
    total_items = sum(live)

    def locate(e):
        slot = jnp.int32(0)
        start = jnp.int32(0)
        for s in range(1, TILES_PER_STEP):
            here = e >= starts[s]
            slot = slot + here.astype(jnp.int32)
            start = jnp.where(here, starts[s], start)
        return slot, e - start, e >= total_items

    zeros_tail = jnp.zeros((MXU_DEPTH - LANES - BIAS_ROWS, TQ), BF16)
    ones_rows = jnp.ones((SUM_ROWS, 1), BF16)

    def pv_operand(v_t):
        return jnp.concatenate([v_t, jnp.broadcast_to(ones_rows, (SUM_ROWS, v_t.shape[1]))], axis=0)

    def produce(item, parity, a):
        slot, group, past_end = item
        gc = jnp.minimum(group, total_groups - 1)
        k_g = k_ref[0, pl.ds(pl.multiple_of(gc * group_keys, group_keys), group_keys), :]
        lhs = jnp.concatenate([k_g, onehot_ref[gc % 2]], axis=1)
        penalty = jnp.where(past_end, NEG, 0.0)
        slab_row = pl.multiple_of((gc // 2) * 2 * KV_GROUP, 2 * KV_GROUP)
        slab = bias_scr[slot, a, pl.ds(slab_row, 2 * KV_GROUP), :] + penalty
        slab = jnp.concatenate([slab, jnp.zeros_like(slab)], axis=0).astype(BF16)
        rhs = jnp.concatenate([q_scr[slot, a], slab, zeros_tail], axis=0)
        s = jnp.dot(lhs, rhs, preferred_element_type=F32)
        s_refs[parity][a][...] = s
        bm_refs[parity][a][...] = jnp.max(s, axis=0, keepdims=True)

    def consume(item, parity, a):
        slot, group, _ = item
        j0 = jnp.minimum(group, total_groups - 1) * KV_GROUP
        rows = slice(a * HEAD_DIM, (a + 1) * HEAD_DIM)
        v_g = jnp.concatenate([vt_ref[0, j0 + c, rows, :] for c in range(KV_GROUP)], axis=1)
        m = state_m[slot, a]
        m_new = jnp.maximum(m, bm_refs[parity][a][...])
        alpha = jnp.exp2(m - m_new)
        p = jnp.exp2(s_refs[parity][a][...] - m_new).astype(BF16)
        pv = jnp.dot(pv_operand(v_g), p, preferred_element_type=F32)
        state_m[slot, a] = m_new
        state_acc[slot, a] = alpha * state_acc[slot, a] + pv

    key_idx = lax.broadcasted_iota(jnp.int32, (MOBA_BLOCK, TQ), 0)
    qry_idx = lax.broadcasted_iota(jnp.int32, (MOBA_BLOCK, TQ), 1)
    causal = key_idx <= qry_idx
    own = []
    for s in range(TILES_PER_STEP):
        cols = slice(s * TQ, (s + 1) * TQ)
        q_pair = qt_ref[0, :, cols]
        row = lax.broadcasted_iota(jnp.int32, q_pair.shape, 0)
        zero = jnp.zeros_like(q_pair)
        q_heads = (jnp.where(row < HEAD_DIM, q_pair, zero), jnp.where(row >= HEAD_DIM, q_pair, zero))
        k_own = k_ref[0, pl.ds(pl.multiple_of((first_tile + s) * MOBA_BLOCK, MOBA_BLOCK), MOBA_BLOCK), :]
        for a in range(2):
            q_scr[s, a] = q_heads[a]
            bias_scr[s, a] = bias_ref[0, a, :, cols]
        scores = [jnp.where(causal, jnp.dot(k_own, q_heads[a], preferred_element_type=F32), NEG)
                  for a in range(2)]
        extra = s % KV_GROUP
        if extra:
            g = own_group[s]
            keys = extra * MOBA_BLOCK
            k_past = k_ref[0, pl.ds(pl.multiple_of(g * group_keys, group_keys), keys), :]
            lhs = jnp.concatenate([k_past, onehot_ref[g % 2, 0:keys, :]], axis=1)
            slab_row = pl.multiple_of((g // 2) * 2 * KV_GROUP, 2 * KV_GROUP)
            for a in range(2):
                slab = bias_ref[0, a, pl.ds(slab_row, 2 * KV_GROUP), cols]
                slab = jnp.concatenate([slab, jnp.zeros_like(slab)], axis=0).astype(BF16)
                rhs = jnp.concatenate([q_heads[a], slab, zeros_tail], axis=0)
                past = jnp.dot(lhs, rhs, preferred_element_type=F32)
                scores[a] = jnp.concatenate([past, scores[a]], axis=0)
        own.append(scores)
    first_item = locate(jnp.int32(0))
    for a in range(2):
        produce(first_item, 0, a)
    for s in range(TILES_PER_STEP):
        extra = s % KV_GROUP
        for a in range(2):
            rows = slice(a * HEAD_DIM, (a + 1) * HEAD_DIM)
            m = jnp.max(own[s][a], axis=0, keepdims=True)
            p = jnp.exp2(own[s][a] - m).astype(BF16)
            v_t = jnp.concatenate([vt_ref[0, first_tile + s - extra + c, rows, :] for c in range(extra + 1)], axis=1)
            state_m[s, a] = m
            state_acc[s, a] = jnp.dot(pv_operand(v_t), p, preferred_element_type=F32)

    def run_items(items_per_step, first, n_steps):
        def body(step, carry):
            e = first + items_per_step * step
            for sub in range(items_per_step):
                cur = sub % 2
                this_item, next_item = locate(e + sub), locate(e + sub + 1)
                for a in range(2):
                    produce(next_item, 1 - cur, a)
                    consume(this_item, cur, a)
            return carry
        lax.fori_loop(0, n_steps, body, 0)

    done = 0
    for size in ITEM_STEPS[:-1]:
        n_steps = (total_items - done) // size
        run_items(size, done, n_steps)
        done = done + n_steps * size
    run_items(ITEM_STEPS[-1], done, (total_items - done + ITEM_STEPS[-1] - 1) // ITEM_STEPS[-1])

    for s in range(TILES_PER_STEP):
        heads = []
        for a in range(2):
            acc = state_acc[s, a]
            heads.append(acc[:HEAD_DIM] / acc[HEAD_DIM:HEAD_DIM + 1])
        o_ref[0, s * TQ:(s + 1) * TQ, :] = jnp.concatenate(heads, axis=0).T.astype(BF16)


def _block_onehot():
    key_block = jnp.arange(KV_GROUP * MOBA_BLOCK, dtype=jnp.int32) // MOBA_BLOCK
    col = jnp.arange(LANES, dtype=jnp.int32)
    parity = jnp.arange(2, dtype=jnp.int32)
    hit = col[None, None, :] == (parity[:, None, None] * KV_GROUP + key_block[None, :, None])
    return hit.astype(BF16)


def _moba_attn(qt, k, vt, bias):
    B, _, S = qt.shape
    nb = S // MOBA_BLOCK
    tq = TILES_PER_STEP * MOBA_BLOCK
    assert nb % TILES_PER_STEP == 0 and TILES_PER_STEP % KV_GROUP == 0
    assert 2 * KV_GROUP == F32_SUBLANES and nb % F32_SUBLANES == 0, "bias slabs are whole f32 sublane tiles"
    assert all(size % 2 == 0 for size in ITEM_STEPS), "score buffers alternate by item parity"
    pairs = MOBA_HEADS // 2
    group_keys = KV_GROUP * MOBA_BLOCK
    return pl.pallas_call(
        _moba_attn_kernel,
        grid=(B, pairs, nb // TILES_PER_STEP),
        in_specs=[
            pl.BlockSpec((1, LANES, tq), lambda b, p, i: (b, p, i)),
            pl.BlockSpec((1, S, LANES), lambda b, p, i: (b, 0, p)),
            pl.BlockSpec((1, nb, LANES, MOBA_BLOCK), lambda b, p, i: (b, 0, p, 0)),
            pl.BlockSpec((1, 2, nb, tq), lambda b, p, i: (b, p, 0, i)),
            pl.BlockSpec((2, group_keys, LANES), lambda b, p, i: (0, 0, 0)),
        ],
        out_specs=pl.BlockSpec((1, tq, LANES), lambda b, p, i: (b, i, p)),
        out_shape=jax.ShapeDtypeStruct((B, S, TOK_WIDTH), BF16),
        scratch_shapes=([pltpu.VMEM((group_keys, MOBA_BLOCK), F32)] * 4 + [pltpu.VMEM((1, MOBA_BLOCK), F32)] * 4
                        + [pltpu.VMEM((TILES_PER_STEP, 2, LANES, MOBA_BLOCK), BF16),
                           pltpu.VMEM((TILES_PER_STEP, 2, nb, MOBA_BLOCK), F32),
                           pltpu.VMEM((TILES_PER_STEP, 2, 1, MOBA_BLOCK), F32),
                           pltpu.VMEM((TILES_PER_STEP, 2, HEAD_DIM + SUM_ROWS, MOBA_BLOCK), F32)]),
        compiler_params=pltpu.CompilerParams(
            dimension_semantics=("arbitrary", "arbitrary", "arbitrary"), vmem_limit_bytes=VMEM_LIMIT_BYTES),
        name="moba_attn",
    )(qt, k, vt, bias, _block_onehot())


def _post_kernel(x_ref, tok_ref, mo_ref, wo_ref, g_ref, w1_ref, w2_ref, o_ref):
    mixed = jnp.concatenate([tok_ref[0], mo_ref[0]], axis=-1)
    x1 = x_ref[0] + jnp.dot(mixed, wo_ref[...], preferred_element_type=F32)
    u = _rms_rows(x1, g_ref[...]).astype(BF16)
    h = jnp.dot(u, w1_ref[...], preferred_element_type=F32)
    h = jnp.square(jnp.maximum(h, 0.0)).astype(BF16)
    o_ref[0] = x1 + jnp.dot(h, w2_ref[...], preferred_element_type=F32)


def _post(x, tok, mo, wo, g, w1, w2):
    B, S, D = x.shape
    T = min(TOKEN_TILE, S)
    const = lambda b, t: (0, 0)
    resident = functools.partial(pl.BlockSpec, index_map=const, pipeline_mode=pl.Buffered(1))
    return pl.pallas_call(
        _post_kernel,
        grid=(B, S // T),
        in_specs=[
            pl.BlockSpec((1, T, D), lambda b, t: (b, t, 0)),
            pl.BlockSpec((1, T, TOK_WIDTH), lambda b, t: (b, t, 0)),
            pl.BlockSpec((1, T, MEM_WIDTH), lambda b, t: (b, t, 0)),
            resident((D, D)),
            pl.BlockSpec((1, D), const),
            resident((D, D_FF)),
            resident((D_FF, D)),
        ],
        out_specs=pl.BlockSpec((1, T, D), lambda b, t: (b, t, 0)),
        out_shape=jax.ShapeDtypeStruct((B, S, D), F32),
        compiler_params=_params(),
        name="post",
    )(x, tok, mo, wo, g, w1, w2)


def _rope_tables_t(seq_len):
    pos = jnp.arange(seq_len, dtype=F32)
    inv = ROPE_THETA ** (-jnp.arange(0, HEAD_DIM, 2, dtype=F32) / HEAD_DIM)
    ang = pos[:, None] * inv[None, :]
    return jnp.cos(ang).T, jnp.sin(ang).T


def _block_diag(w_group):
    G, C, _ = w_group.shape
    eye = jnp.eye(G, dtype=w_group.dtype)
    return (eye[:, None, :, None] * w_group[:, :, None, :]).reshape(G * C, G * C)


def kernel(x, mem, g_mix, g_mem, g_mlp, w_in_pool, w_pool_group, pool_scale, w_in_moba, moba_q_gain,
           moba_k_gain, w_mem_kv, mem_q_gain, mem_k_gain, w_out, w_ff1, w_ff2):
    depth = g_mix.shape[0]
    S = x.shape[1]
    cos_t, sin_t = _rope_tables_t(S)
    col = lambda v: v.reshape(-1, 1)
    row = lambda v: v.reshape(1, -1)
    for i in range(depth):
        j = i // 2
        mk, mvt = _mem_kv(mem, row(g_mem[i]), w_mem_kv[i].T.astype(BF16), col(mem_k_gain[i]))
        if i % 2 == 0:
            w = w_in_pool[j]
            tok, mo = _pool_in(
                x, row(g_mix[i]), w[:, :TOK_WIDTH].astype(BF16), w[:, TOK_WIDTH:].T.astype(BF16),
                _block_diag(w_pool_group[j]).astype(BF16), row(pool_scale[j]), col(mem_q_gain[i]), mk, mvt)
        else:
            w = w_in_moba[j]
            qt, k, vt, bias, mo = _moba_in(
                x, row(g_mix[i]), w[:, :3 * TOK_WIDTH].T.astype(BF16), w[:, 3 * TOK_WIDTH:].T.astype(BF16),
                cos_t, sin_t, col(moba_q_gain[j]), col(moba_k_gain[j]), col(mem_q_gain[i]), mk, mvt)
            tok = _moba_attn(qt, k, vt, bias)
        x = _post(x, tok, mo, w_out[i].astype(BF16), row(g_mlp[i]), w_ff1[i].astype(BF16), w_ff2[i].astype(BF16))
    return x
```

```python
import functools

import jax
import jax.numpy as jnp
from jax import lax
from jax.experimental import pallas as pl
from jax.experimental.pallas import tpu as pltpu

D_MODEL = 1024
HEAD_DIM = 64
HALF_DIM = HEAD_DIM // 2
MEM_HEADS = 4
MEM_WIDTH = MEM_HEADS * HEAD_DIM
TOK_WIDTH = D_MODEL - MEM_WIDTH
MOBA_HEADS = TOK_WIDTH // HEAD_DIM
MOBA_BLOCK = 256
MOBA_TOPK = 3
POOL_WINDOWS = (2, 4, 8, 16)
POOL_GROUP_WIDTH = TOK_WIDTH // len(POOL_WINDOWS)
POOL_HALO = 16
D_FF = 4 * D_MODEL
ROPE_THETA = 10000.0
EPS = 1e-6
NEG = -1e30
SM_SCALE = HEAD_DIM ** -0.5
LOG2_E = 1.4426950408889634

LANES = 128
F32_SUBLANES = 8
BIAS_ROWS = 16
SUM_ROWS = 16
MXU_DEPTH = 256
VMEM_LIMIT_BYTES = 56 * 1024 * 1024

TOKEN_TILE = 512
VALUE_CHUNKS = 3
TILES_PER_STEP = 16
KV_GROUP = 4
ITEM_STEPS = (32, 8, 4, 2)

F32 = jnp.float32
BF16 = jnp.bfloat16
NT_DIMS = (((1,), (1,)), ((), ()))


def _params():
    return pltpu.CompilerParams(
        dimension_semantics=("arbitrary", "arbitrary"), vmem_limit_bytes=VMEM_LIMIT_BYTES)


def _rms_rows(x, g):
    ms = jnp.mean(x * x, axis=-1, keepdims=True)
    return x * lax.rsqrt(ms + EPS) * g


def _head_rms_t(h, gain_col):
    ms = jnp.mean(h * h, axis=0, keepdims=True)
    return h * lax.rsqrt(ms + EPS) * gain_col


def _rope_t(h, cos, sin):
    h1, h2 = h[:HALF_DIM], h[HALF_DIM:]
    return jnp.concatenate([h1 * cos - h2 * sin, h2 * cos + h1 * sin], axis=0)


def _split_bf16(x):
    hi = x.astype(BF16)
    return hi, (x - hi.astype(F32)).astype(BF16)


def _pair_operand(q_bf, slot):
    z = jnp.zeros_like(q_bf)
    return jnp.concatenate([q_bf, z] if slot == 0 else [z, q_bf], axis=0)


def _mem_attention_t(qm_t, mq_gain, mk_ref, mvt_ref):
    scores = []
    for h in range(MEM_HEADS):
        pair, slot = divmod(h, 2)
        q = _head_rms_t(qm_t[h * HEAD_DIM:(h + 1) * HEAD_DIM], mq_gain)
        q2 = _pair_operand((q * (SM_SCALE * LOG2_E)).astype(BF16), slot)
        k_pair = mk_ref[0, :, pair * LANES:(pair + 1) * LANES]
        scores.append(jnp.dot(k_pair, q2, preferred_element_type=F32))
    probs = [jnp.exp2(s - jnp.max(s, axis=0, keepdims=True)).astype(BF16) for s in scores]
    outs = []
    for h in range(MEM_HEADS):
        v_t = mvt_ref[0, h * HEAD_DIM:(h + 1) * HEAD_DIM, :]
        ones = jnp.ones((SUM_ROWS, v_t.shape[1]), BF16)
        r = jnp.dot(jnp.concatenate([v_t, ones], axis=0), probs[h], preferred_element_type=F32)
        outs.append(r[:HEAD_DIM] / r[HEAD_DIM:HEAD_DIM + 1])
    return jnp.concatenate(outs, axis=0).T


def _mem_kv_kernel(mem_ref, g_ref, wkv_t_ref, kg_ref, mk_ref, mvt_ref):
    mem_n = _rms_rows(mem_ref[0], g_ref[...]).astype(BF16)
    kv_t = lax.dot_general(wkv_t_ref[...], mem_n, NT_DIMS, preferred_element_type=F32)
    k_heads = [_head_rms_t(kv_t[h * HEAD_DIM:(h + 1) * HEAD_DIM], kg_ref[...]) for h in range(MEM_HEADS)]
    mk_ref[0] = jnp.concatenate(k_heads, axis=0).T.astype(BF16)
    mvt_ref[0] = kv_t[MEM_WIDTH:].astype(BF16)


def _mem_kv(mem, g, wkv_t, k_gain):
    B, M, D = mem.shape
    return pl.pallas_call(
        _mem_kv_kernel,
        grid=(B, 1),
        in_specs=[
            pl.BlockSpec((1, M, D), lambda b, _: (b, 0, 0)),
            pl.BlockSpec((1, D), lambda b, _: (0, 0)),
            pl.BlockSpec((2 * MEM_WIDTH, D), lambda b, _: (0, 0)),
            pl.BlockSpec((HEAD_DIM, 1), lambda b, _: (0, 0)),
        ],
        out_specs=[
            pl.BlockSpec((1, M, MEM_WIDTH), lambda b, _: (b, 0, 0)),
            pl.BlockSpec((1, MEM_WIDTH, M), lambda b, _: (b, 0, 0)),
        ],
        out_shape=[
            jax.ShapeDtypeStruct((B, M, MEM_WIDTH), BF16),
            jax.ShapeDtypeStruct((B, MEM_WIDTH, M), BF16),
        ],
        compiler_params=_params(),
        name="mem_kv",
    )(mem, g, wkv_t, k_gain)


def _pool_in_kernel(x_ref, g_ref, w_tok_ref, wm_t_ref, wg_ref, scale_ref, mqg_ref, mk_ref, mvt_ref,
                    tok_ref, mo_ref, halo_ref):
    t = pl.program_id(1)
    T = x_ref.shape[1]

    @pl.when(t == 0)
    def _():
        halo_ref[...] = jnp.zeros_like(halo_ref)

    u = _rms_rows(x_ref[0], g_ref[...]).astype(BF16)
    h = jnp.dot(u, w_tok_ref[...], preferred_element_type=F32)
    qm_t = lax.dot_general(wm_t_ref[...], u, NT_DIMS, preferred_element_type=F32)

    ext = jnp.concatenate([halo_ref[...], h], axis=0)
    halo_ref[...] = h[T - POOL_HALO:]
    s2 = ext + pltpu.roll(ext, 1, 0)
    s4 = s2 + pltpu.roll(s2, 2, 0)
    s8 = s4 + pltpu.roll(s4, 4, 0)
    s16 = s8 + pltpu.roll(s8, 8, 0)
    lane = lax.broadcasted_iota(jnp.int32, (T, TOK_WIDTH), 1)
    pos = t * T + lax.broadcasted_iota(jnp.int32, (T, TOK_WIDTH), 0)
    sums = (s2, s4, s8, s16)
    win_sum = sums[-1][POOL_HALO:]
    window = jnp.full((T, TOK_WIDTH), POOL_WINDOWS[-1], jnp.int32)
    for gi in range(len(POOL_WINDOWS) - 2, -1, -1):
        in_group = lane < (gi + 1) * POOL_GROUP_WIDTH
        win_sum = jnp.where(in_group, sums[gi][POOL_HALO:], win_sum)
        window = jnp.where(in_group, POOL_WINDOWS[gi], window)
    count = jnp.minimum(pos + 1, window).astype(F32)
    d = (win_sum / count - h).astype(BF16)
    half = TOK_WIDTH // 2
    y = jnp.concatenate(
        [jnp.dot(d[:, lo:lo + half], wg_ref[lo:lo + half, lo:lo + half], preferred_element_type=F32)
         for lo in (0, half)], axis=1) * scale_ref[...]
    tok_ref[0] = y.astype(BF16)

    mo_ref[0] = _mem_attention_t(qm_t, mqg_ref[...], mk_ref, mvt_ref).astype(BF16)


def _pool_in(x, g, w_tok, wm_t, wg, scale, mq_gain, mk, mvt):
    B, S, D = x.shape
    M = mk.shape[1]
    T = min(TOKEN_TILE, S)
    const = lambda b, t: (0, 0)
    return pl.pallas_call(
        _pool_in_kernel,
        grid=(B, S // T),
        in_specs=[
            pl.BlockSpec((1, T, D), lambda b, t: (b, t, 0)),
            pl.BlockSpec((1, D), const),
            pl.BlockSpec((D, TOK_WIDTH), const),
            pl.BlockSpec((MEM_WIDTH, D), const),
            pl.BlockSpec((TOK_WIDTH, TOK_WIDTH), const),
            pl.BlockSpec((1, TOK_WIDTH), const),
            pl.BlockSpec((HEAD_DIM, 1), const),
            pl.BlockSpec((1, M, MEM_WIDTH), lambda b, t: (b, 0, 0)),
            pl.BlockSpec((1, MEM_WIDTH, M), lambda b, t: (b, 0, 0)),
        ],
        out_specs=[
            pl.BlockSpec((1, T, TOK_WIDTH), lambda b, t: (b, t, 0)),
            pl.BlockSpec((1, T, MEM_WIDTH), lambda b, t: (b, t, 0)),
        ],
        out_shape=[
            jax.ShapeDtypeStruct((B, S, TOK_WIDTH), BF16),
            jax.ShapeDtypeStruct((B, S, MEM_WIDTH), BF16),
        ],
        scratch_shapes=[pltpu.VMEM((POOL_HALO, TOK_WIDTH), F32)],
        compiler_params=_params(),
        name="pool_in",
    )(x, g, w_tok, wm_t, wg, scale, mq_gain, mk, mvt)


def _selection_bias(gate, t, T):
    nb, width = gate.shape
    blk = lax.broadcasted_iota(jnp.int32, (nb, width), 0)
    pos = t * T + lax.rem(lax.broadcasted_iota(jnp.int32, (nb, width), 1), T)
    own = lax.shift_right_logical(pos, MOBA_BLOCK.bit_length() - 1)
    g = jnp.where(blk < own, gate, -jnp.inf)
    bias = jnp.full((nb, width), NEG, F32)
    for _ in range(MOBA_TOPK):
        mx = jnp.max(g, axis=0, keepdims=True)
        idx = jnp.min(jnp.where(g == mx, blk, nb), axis=0, keepdims=True)
        idx = jnp.where(mx > -jnp.inf, idx, -1)
        pick = blk == idx
        bias = jnp.where(pick, 0.0, bias)
        g = jnp.where(pick, -jnp.inf, g)
    return bias


def _moba_in_kernel(x_ref, g_ref, wqkv_t_ref, wm_t_ref, cos_ref, sin_ref, qg_ref, kg_ref, mqg_ref,
                    mk_ref, mvt_ref, qt_ref, k_ref, vt_ref, bias_ref, mo_ref, kmean_ref):
    t = pl.program_id(1)
    T = x_ref.shape[1]
    blocks_per_tile = T // MOBA_BLOCK

    @pl.when(t == 0)
    def _():
        kmean_ref[...] = jnp.zeros_like(kmean_ref)

    u = _rms_rows(x_ref[0], g_ref[...]).astype(BF16)
    cos, sin = cos_ref[...], sin_ref[...]

    def proj_t(lo, hi):
        return lax.dot_general(wqkv_t_ref[lo:hi, :], u, NT_DIMS, preferred_element_type=F32)

    k_t = proj_t(TOK_WIDTH, 2 * TOK_WIDTH)
    q_t = proj_t(0, TOK_WIDTH)

    k_heads = [_rope_t(_head_rms_t(k_t[h * HEAD_DIM:(h + 1) * HEAD_DIM], kg_ref[...]), cos, sin)
               for h in range(MOBA_HEADS)]
    k_nat = jnp.concatenate(k_heads, axis=0).T
    k_ref[0] = k_nat.astype(BF16)
    for c in range(blocks_per_tile):
        kmean_ref[pl.ds(t * blocks_per_tile + c, 1), :] = jnp.mean(
            k_nat[c * MOBA_BLOCK:(c + 1) * MOBA_BLOCK], axis=0, keepdims=True)

    km_hi, km_lo = _split_bf16(kmean_ref[...])
    heads_per_chunk = MOBA_HEADS // VALUE_CHUNKS
    chunk_rows = TOK_WIDTH // VALUE_CHUNKS
    for chunk in range(VALUE_CHUNKS):
        lo = 2 * TOK_WIDTH + chunk * chunk_rows
        v_t = proj_t(lo, lo + chunk_rows)
        for c in range(blocks_per_tile):
            vt_ref[0, c, chunk * chunk_rows:(chunk + 1) * chunk_rows, :] = (
                v_t[:, c * MOBA_BLOCK:(c + 1) * MOBA_BLOCK].astype(BF16))
        heads = range(chunk * heads_per_chunk, (chunk + 1) * heads_per_chunk)
        gates = []
        for h in heads:
            rows = slice(h * HEAD_DIM, (h + 1) * HEAD_DIM)
            pair, slot = divmod(h, 2)
            lanes = slice(pair * LANES, (pair + 1) * LANES)
            q = _rope_t(_head_rms_t(q_t[rows], qg_ref[...]), cos, sin)
            qt_ref[0, rows, :] = (q * (SM_SCALE * LOG2_E)).astype(BF16)
            q_hi, q_lo = _split_bf16(q)
            q_hi2, q_lo2 = _pair_operand(q_hi, slot), _pair_operand(q_lo, slot)
            gates.append(jnp.dot(jnp.concatenate([km_hi[:, lanes], km_lo[:, lanes]], axis=1),
                                 jnp.concatenate([q_hi2, q_hi2], axis=0), preferred_element_type=F32)
                         + jnp.dot(km_hi[:, lanes], q_lo2, preferred_element_type=F32))
        bias = _selection_bias(jnp.concatenate(gates, axis=1), t, T)
        for n, h in enumerate(heads):
            bias_ref[0, h] = bias[:, n * T:(n + 1) * T]

    qm_t = lax.dot_general(wm_t_ref[...], u, NT_DIMS, preferred_element_type=F32)
    mo_ref[0] = _mem_attention_t(qm_t, mqg_ref[...], mk_ref, mvt_ref).astype(BF16)


def _moba_in(x, g, wqkv_t, wm_t, cos_t, sin_t, q_gain, k_gain, mq_gain, mk, mvt):
    B, S, D = x.shape
    M = mk.shape[1]
    T = min(TOKEN_TILE, S)
    nb = S // MOBA_BLOCK
    const = lambda b, t: (0, 0)
    return pl.pallas_call(
        _moba_in_kernel,
        grid=(B, S // T),
        in_specs=[
            pl.BlockSpec((1, T, D), lambda b, t: (b, t, 0)),
            pl.BlockSpec((1, D), const),
            pl.BlockSpec((3 * TOK_WIDTH, D), const),
            pl.BlockSpec((MEM_WIDTH, D), const),
            pl.BlockSpec((HALF_DIM, T), lambda b, t: (0, t)),
            pl.BlockSpec((HALF_DIM, T), lambda b, t: (0, t)),
            pl.BlockSpec((HEAD_DIM, 1), const),
            pl.BlockSpec((HEAD_DIM, 1), const),
            pl.BlockSpec((HEAD_DIM, 1), const),
            pl.BlockSpec((1, M, MEM_WIDTH), lambda b, t: (b, 0, 0)),
            pl.BlockSpec((1, MEM_WIDTH, M), lambda b, t: (b, 0, 0)),
        ],
        out_specs=[
            pl.BlockSpec((1, TOK_WIDTH, T), lambda b, t: (b, 0, t)),
            pl.BlockSpec((1, T, TOK_WIDTH), lambda b, t: (b, t, 0)),
            pl.BlockSpec((1, T // MOBA_BLOCK, TOK_WIDTH, MOBA_BLOCK), lambda b, t: (b, t, 0, 0)),
            pl.BlockSpec((1, MOBA_HEADS, nb, T), lambda b, t: (b, 0, 0, t)),
            pl.BlockSpec((1, T, MEM_WIDTH), lambda b, t: (b, t, 0)),
        ],
        out_shape=[
            jax.ShapeDtypeStruct((B, TOK_WIDTH, S), BF16),
            jax.ShapeDtypeStruct((B, S, TOK_WIDTH), BF16),
            jax.ShapeDtypeStruct((B, nb, TOK_WIDTH, MOBA_BLOCK), BF16),
            jax.ShapeDtypeStruct((B, MOBA_HEADS, nb, S), F32),
            jax.ShapeDtypeStruct((B, S, MEM_WIDTH), BF16),
        ],
        scratch_shapes=[pltpu.VMEM((nb, TOK_WIDTH), F32)],
        compiler_params=_params(),
        name="moba_in",
    )(x, g, wqkv_t, wm_t, cos_t, sin_t, q_gain, k_gain, mq_gain, mk, mvt)


def _moba_attn_kernel(qt_ref, k_ref, vt_ref, bias_ref, onehot_ref, o_ref, *scratch):
    s_refs = (scratch[0:2], scratch[2:4])
    bm_refs = (scratch[4:6], scratch[6:8])
    q_scr, bias_scr, state_m, state_acc = scratch[8:12]
    TQ = MOBA_BLOCK
    group_keys = KV_GROUP * MOBA_BLOCK
    total_groups = k_ref.shape[1] // group_keys
    first_tile = pl.program_id(2) * TILES_PER_STEP

    own_group = [first_tile // KV_GROUP + s // KV_GROUP for s in range(TILES_PER_STEP)]
    live = own_group
    starts = [sum(live[:s]) for s in range(TILES_PER_STEP)]
    total_items = sum(live)

    def locate(e):
        slot = jnp.int32(0)
        start = jnp.int32(0)
        for s in range(1, TILES_PER_STEP):
            here = e >= starts[s]
            slot = slot + here.astype(jnp.int32)
            start = jnp.where(here, starts[s], start)
        return slot, e - start, e >= total_items

    zeros_tail = jnp.zeros((MXU_DEPTH - LANES - BIAS_ROWS, TQ), BF16)
    ones_rows = jnp.ones((SUM_ROWS, 1), BF16)

    def pv_operand(v_t):
        return jnp.concatenate([v_t, jnp.broadcast_to(ones_rows, (SUM_ROWS, v_t.shape[1]))], axis=0)

    def produce(item, parity, a):
        slot, group, past_end = item
        gc = jnp.minimum(group, total_groups - 1)
        k_g = k_ref[0, pl.ds(pl.multiple_of(gc * group_keys, group_keys), group_keys), :]
        lhs = jnp.concatenate([k_g, onehot_ref[gc % 2]], axis=1)
        penalty = jnp.where(past_end, NEG, 0.0)
        slab_row = pl.multiple_of((gc // 2) * 2 * KV_GROUP, 2 * KV_GROUP)
        slab = bias_scr[slot, a, pl.ds(slab_row, 2 * KV_GROUP), :] + penalty
        slab = jnp.concatenate([slab, jnp.zeros_like(slab)], axis=0).astype(BF16)
        rhs = jnp.concatenate([q_scr[slot, a], slab, zeros_tail], axis=0)
        s = jnp.dot(lhs, rhs, preferred_element_type=F32)
        s_refs[parity][a][...] = s
        bm_refs[parity][a][...] = jnp.max(s, axis=0, keepdims=True)

    def consume(item, parity, a):
        slot, group, _ = item
        j0 = jnp.minimum(group, total_groups - 1) * KV_GROUP
        rows = slice(a * HEAD_DIM, (a + 1) * HEAD_DIM)
        v_g = jnp.concatenate([vt_ref[0, j0 + c, rows, :] for c in range(KV_GROUP)], axis=1)
        m = state_m[slot, a]
        m_new = jnp.maximum(m, bm_refs[parity][a][...])
        alpha = jnp.exp2(m - m_new)
        p = jnp.exp2(s_refs[parity][a][...] - m_new).astype(BF16)
        pv = jnp.dot(pv_operand(v_g), p, preferred_element_type=F32)
        state_m[slot, a] = m_new
        state_acc[slot, a] = alpha * state_acc[slot, a] + pv

    key_idx = lax.broadcasted_iota(jnp.int32, (MOBA_BLOCK, TQ), 0)
    qry_idx = lax.broadcasted_iota(jnp.int32, (MOBA_BLOCK, TQ), 1)
    causal = key_idx <= qry_idx
    own = []
    for s in range(TILES_PER_STEP):
        cols = slice(s * TQ, (s + 1) * TQ)
        q_pair = qt_ref[0, :, cols]
        row = lax.broadcasted_iota(jnp.int32, q_pair.shape, 0)
        zero = jnp.zeros_like(q_pair)
        q_heads = (jnp.where(row < HEAD_DIM, q_pair, zero), jnp.where(row >= HEAD_DIM, q_pair, zero))
        k_own = k_ref[0, pl.ds(pl.multiple_of((first_tile + s) * MOBA_BLOCK, MOBA_BLOCK), MOBA_BLOCK), :]
        for a in range(2):
            q_scr[s, a] = q_heads[a]
            bias_scr[s, a] = bias_ref[0, a, :, cols]
        scores = [jnp.where(causal, jnp.dot(k_own, q_heads[a], preferred_element_type=F32), NEG)
                  for a in range(2)]
        extra = s % KV_GROUP
        if extra:
            g = own_group[s]
            keys = extra * MOBA_BLOCK
            k_past = k_ref[0, pl.ds(pl.multiple_of(g * group_keys, group_keys), keys), :]
            lhs = jnp.concatenate([k_past, onehot_ref[g % 2, 0:keys, :]], axis=1)
            slab_row = pl.multiple_of((g // 2) * 2 * KV_GROUP, 2 * KV_GROUP)
            for a in range(2):
                slab = bias_ref[0, a, pl.ds(slab_row, 2 * KV_GROUP), cols]
                slab = jnp.concatenate([slab, jnp.zeros_like(slab)], axis=0).astype(BF16)
                rhs = jnp.concatenate([q_heads[a], slab, zeros_tail], axis=0)
                past = jnp.dot(lhs, rhs, preferred_element_type=F32)
                scores[a] = jnp.concatenate([past, scores[a]], axis=0)
        own.append(scores)
    first_item = locate(jnp.int32(0))
    for a in range(2):
        produce(first_item, 0, a)
    for s in range(TILES_PER_STEP):
        extra = s % KV_GROUP
        for a in range(2):
            rows = slice(a * HEAD_DIM, (a + 1) * HEAD_DIM)
            m = jnp.max(own[s][a], axis=0, keepdims=True)
            p = jnp.exp2(own[s][a] - m).astype(BF16)
            v_t = jnp.concatenate([vt_ref[0, first_tile + s - extra + c, rows, :] for c in range(extra + 1)], axis=1)
            state_m[s, a] = m
            state_acc[s, a] = jnp.dot(pv_operand(v_t), p, preferred_element_type=F32)

    def run_items(items_per_step, first, n_steps):
        def body(step, carry):
            e = first + items_per_step * step
            for sub in range(items_per_step):
                cur = sub % 2
                this_item, next_item = locate(e + sub), locate(e + sub + 1)
                for a in range(2):
                    produce(next_item, 1 - cur, a)
                    consume(this_item, cur, a)
            return carry
        lax.fori_loop(0, n_steps, body, 0)

    done = 0
    for size in ITEM_STEPS[:-1]:
        n_steps = (total_items - done) // size
        run_items(size, done, n_steps)
        done = done + n_steps * size
    run_items(ITEM_STEPS[-1], done, (total_items - done + ITEM_STEPS[-1] - 1) // ITEM_STEPS[-1])

    for s in range(TILES_PER_STEP):
        heads = []
        for a in range(2):
            acc = state_acc[s, a]
            heads.append(acc[:HEAD_DIM] / acc[HEAD_DIM:HEAD_DIM + 1])
        o_ref[0, s * TQ:(s + 1) * TQ, :] = jnp.concatenate(heads, axis=0).T.astype(BF16)


def _block_onehot():
    key_block = jnp.arange(KV_GROUP * MOBA_BLOCK, dtype=jnp.int32) // MOBA_BLOCK
    col = jnp.arange(LANES, dtype=jnp.int32)
    parity = jnp.arange(2, dtype=jnp.int32)
    hit = col[None, None, :] == (parity[:, None, None] * KV_GROUP + key_block[None, :, None])
    return hit.astype(BF16)


def _moba_attn(qt, k, vt, bias):
    B, _, S = qt.shape
    nb = S // MOBA_BLOCK
    tq = TILES_PER_STEP * MOBA_BLOCK
    assert nb % TILES_PER_STEP == 0 and TILES_PER_STEP % KV_GROUP == 0
    assert 2 * KV_GROUP == F32_SUBLANES and nb % F32_SUBLANES == 0, "bias slabs are whole f32 sublane tiles"
    assert all(size % 2 == 0 for size in ITEM_STEPS), "score buffers alternate by item parity"
    pairs = MOBA_HEADS // 2
    group_keys = KV_GROUP * MOBA_BLOCK
    return pl.pallas_call(
        _moba_attn_kernel,
        grid=(B, pairs, nb // TILES_PER_STEP),
        in_specs=[
            pl.BlockSpec((1, LANES, tq), lambda b, p, i: (b, p, i)),
            pl.BlockSpec((1, S, LANES), lambda b, p, i: (b, 0, p)),
            pl.BlockSpec((1, nb, LANES, MOBA_BLOCK), lambda b, p, i: (b, 0, p, 0)),
            pl.BlockSpec((1, 2, nb, tq), lambda b, p, i: (b, p, 0, i)),
            pl.BlockSpec((2, group_keys, LANES), lambda b, p, i: (0, 0, 0)),
        ],
        out_specs=pl.BlockSpec((1, tq, LANES), lambda b, p, i: (b, i, p)),
        out_shape=jax.ShapeDtypeStruct((B, S, TOK_WIDTH), BF16),
        scratch_shapes=([pltpu.VMEM((group_keys, MOBA_BLOCK), F32)] * 4 + [pltpu.VMEM((1, MOBA_BLOCK), F32)] * 4
                        + [pltpu.VMEM((TILES_PER_STEP, 2, LANES, MOBA_BLOCK), BF16),
                           pltpu.VMEM((TILES_PER_STEP, 2, nb, MOBA_BLOCK), F32),
                           pltpu.VMEM((TILES_PER_STEP, 2, 1, MOBA_BLOCK), F32),
                           pltpu.VMEM((TILES_PER_STEP, 2, HEAD_DIM + SUM_ROWS, MOBA_BLOCK), F32)]),
        compiler_params=pltpu.CompilerParams(
            dimension_semantics=("arbitrary", "arbitrary", "arbitrary"), vmem_limit_bytes=VMEM_LIMIT_BYTES),
        name="moba_attn",
    )(qt, k, vt, bias, _block_onehot())


def _post_kernel(x_ref, tok_ref, mo_ref, wo_ref, g_ref, w1_ref, w2_ref, o_ref):
    mixed = jnp.concatenate([tok_ref[0], mo_ref[0]], axis=-1)
    x1 = x_ref[0] + jnp.dot(mixed, wo_ref[...], preferred_element_type=F32)
    u = _rms_rows(x1, g_ref[...]).astype(BF16)
    h = jnp.dot(u, w1_ref[...], preferred_element_type=F32)
    h = jnp.square(jnp.maximum(h, 0.0)).astype(BF16)
    o_ref[0] = x1 + jnp.dot(h, w2_ref[...], preferred_element_type=F32)


def _post(x, tok, mo, wo, g, w1, w2):
    B, S, D = x.shape
    T = min(TOKEN_TILE, S)
    const = lambda b, t: (0, 0)
    resident = functools.partial(pl.BlockSpec, index_map=const, pipeline_mode=pl.Buffered(1))
    return pl.pallas_call(
        _post_kernel,
        grid=(B, S // T),
        in_specs=[
            pl.BlockSpec((1, T, D), lambda b, t: (b, t, 0)),
            pl.BlockSpec((1, T, TOK_WIDTH), lambda b, t: (b, t, 0)),
            pl.BlockSpec((1, T, MEM_WIDTH), lambda b, t: (b, t, 0)),
            resident((D, D)),
            pl.BlockSpec((1, D), const),
            resident((D, D_FF)),
            resident((D_FF, D)),
        ],
        out_specs=pl.BlockSpec((1, T, D), lambda b, t: (b, t, 0)),
        out_shape=jax.ShapeDtypeStruct((B, S, D), F32),
        compiler_params=_params(),
        name="post",
    )(x, tok, mo, wo, g, w1, w2)


def _rope_tables_t(seq_len):
    pos = jnp.arange(seq_len, dtype=F32)
    inv = ROPE_THETA ** (-jnp.arange(0, HEAD_DIM, 2, dtype=F32) / HEAD_DIM)
    ang = pos[:, None] * inv[None, :]
    return jnp.cos(ang).T, jnp.sin(ang).T


def _block_diag(w_group):
    G, C, _ = w_group.shape
    eye = jnp.eye(G, dtype=w_group.dtype)
    return (eye[:, None, :, None] * w_group[:, :, None, :]).reshape(G * C, G * C)


def kernel(x, mem, g_mix, g_mem, g_mlp, w_in_pool, w_pool_group, pool_scale, w_in_moba, moba_q_gain,
           moba_k_gain, w_mem_kv, mem_q_gain, mem_k_gain, w_out, w_ff1, w_ff2):
    depth = g_mix.shape[0]
    S = x.shape[1]
    cos_t, sin_t = _rope_tables_t(S)
    col = lambda v: v.reshape(-1, 1)
    row = lambda v: v.reshape(1, -1)
    for i in range(depth):
        j = i // 2
        mk, mvt = _mem_kv(mem, row(g_mem[i]), w_mem_kv[i].T.astype(BF16), col(mem_k_gain[i]))
        if i % 2 == 0:
            w = w_in_pool[j]
            tok, mo = _pool_in(
                x, row(g_mix[i]), w[:, :TOK_WIDTH].astype(BF16), w[:, TOK_WIDTH:].T.astype(BF16),
                _block_diag(w_pool_group[j]).astype(BF16), row(pool_scale[j]), col(mem_q_gain[i]), mk, mvt)
        else:
            w = w_in_moba[j]
            qt, k, vt, bias, mo = _moba_in(
                x, row(g_mix[i]), w[:, :3 * TOK_WIDTH].T.astype(BF16), w[:, 3 * TOK_WIDTH:].T.astype(BF16),
                cos_t, sin_t, col(moba_q_gain[j]), col(moba_k_gain[j]), col(mem_q_gain[i]), mk, mvt)
            tok = _moba_attn(qt, k, vt, bias)
        x = _post(x, tok, mo, w_out[i].astype(BF16), row(g_mlp[i]), w_ff1[i].astype(BF16), w_ff2[i].astype(BF16))
    return x
```
